```python
import jax
import jax.numpy as jnp
from jax import lax
import numpy as np

D_MODEL = 1024
BATCH = 16
SEQ = 4096
DEPTH = 2
DEC_BATCH = 32
DEC_SEQ = 64
PAST_LEN = 1024

CHUNK = 64
SUB = 16
P_DIM = 256
HG_HEAD_DIM = 128
HG_WIDTH = D_MODEL // 2
HG_HEADS = HG_WIDTH // HG_HEAD_DIM
RW_HEAD_DIM = 64
RW_WIDTH = D_MODEL - HG_WIDTH
RW_HEADS = RW_WIDTH // RW_HEAD_DIM
RW_DECAY_LORA = 64
RW_A_LORA = 64
RW_GATE_LORA = 128
HG_PROJ = 4 * HG_WIDTH
RW_PROJ = 3 * RW_WIDTH + RW_DECAY_LORA + RW_A_LORA + RW_GATE_LORA
IN_PROJ = HG_PROJ + RW_PROJ
MIX_WIDTH = HG_WIDTH + RW_WIDTH
D_FF = ((8 * D_MODEL // 3 + 127) // 128) * 128
CONV_W = 3
NORM_EPS = 1e-6
RW_GN_EPS = 64e-5
F32 = jnp.float32

kernel_name = 'hgrn2_rwkv7_convffn_stream_step'


def rmsnorm(x, g):
    xf = x.astype(F32)
    y = xf * lax.rsqrt(jnp.mean(xf * xf, axis=-1, keepdims=True) + NORM_EPS)
    return (y * g.astype(F32)).astype(x.dtype)


def gla_chunked(q, k, v, logf, s0):
    bsz, t_len, nh, dk = q.shape
    dv = v.shape[-1]
    n_chunks = -(-t_len // CHUNK)
    pad = n_chunks * CHUNK - t_len
    ns = CHUNK // SUB

    def to_chunks(a):
        a = jnp.pad(a, ((0, 0), (0, pad), (0, 0), (0, 0)))
        a = a.reshape(bsz, n_chunks, CHUNK, nh, a.shape[-1])
        return a.transpose(1, 0, 3, 2, 4)

    causal_sub = jnp.tril(jnp.ones((SUB, SUB), dtype=bool))[..., None]
    earlier = (jnp.arange(ns)[:, None] > jnp.arange(ns)[None, :])[..., None]
    eye_sub = jnp.eye(ns, dtype=F32)[:, None, :, None]

    def step(state, inp):
        qc, kc, vc, gc = inp
        b = jnp.cumsum(gc, axis=-2)
        bs = b.reshape(bsz, nh, ns, SUB, dk)
        qs = qc.reshape(bsz, nh, ns, SUB, dk)
        ks = kc.reshape(bsz, nh, ns, SUB, dk)
        e = bs[..., -1, :]
        r = jnp.concatenate([jnp.zeros_like(e[..., :1, :]), e[..., :-1, :]], axis=-2)
        qe = qs * jnp.exp(bs - r[..., None, :])
        ke = ks * jnp.exp(e[..., None, :] - bs)
        gap = r[..., :, None, :] - e[..., None, :, :]
        m = jnp.where(earlier, jnp.exp(jnp.where(earlier, gap, 0.0)), 0.0)
        a_off = jnp.einsum('bhitd,bhijd,bhjsd->bhitjs', qe, m, ke)
        dd = bs[..., :, None, :] - bs[..., None, :, :]
        dec = jnp.where(causal_sub, jnp.exp(jnp.where(causal_sub, dd, 0.0)), 0.0)
        a_diag = jnp.einsum('bhitd,bhisd,bhitsd->bhits', qs, ks, dec)
        a = (a_off + a_diag[:, :, :, :, None, :] * eye_sub).reshape(bsz, nh, CHUNK, CHUNK)
        o = jnp.einsum('bhts,bhse->bhte', a, vc) + jnp.einsum('bhtd,bhde->bhte', qc * jnp.exp(b), state)
        b_last = b[..., -1, :]
        state = jnp.exp(b_last)[..., None] * state + jnp.einsum(
            'bhsd,bhse->bhde', kc * jnp.exp(b_last[..., None, :] - b), vc)
        return state, o

    s_fin, o = lax.scan(step, s0, (to_chunks(q), to_chunks(k), to_chunks(v), to_chunks(logf)))
    o = o.transpose(1, 0, 3, 2, 4).reshape(bsz, n_chunks * CHUNK, nh, dv)[:, :t_len]
    return o, s_fin


def hgrn2_mix(z, lb, norm_g, s0):
    bsz, t_len, _ = z.shape
    zq, zf, zi, zg = jnp.split(z.astype(F32), 4, axis=-1)
    q = jax.nn.silu(zq)
    log_lb = jnp.log(jnp.maximum(lb.astype(F32), 0.0))
    logf = jnp.logaddexp(jax.nn.log_sigmoid(zf), log_lb + jax.nn.log_sigmoid(-zf))
    k = -jnp.expm1(logf)
    heads = lambda a: a.reshape(bsz, t_len, HG_HEADS, HG_HEAD_DIM)
    o, s_fin = gla_chunked(heads(q), heads(k), heads(zi), heads(logf), s0.astype(F32))
    o = o * lax.rsqrt(jnp.mean(o * o, axis=-1, keepdims=True) + NORM_EPS)
    o = o.reshape(bsz, t_len, HG_WIDTH) * norm_g.astype(F32) * jax.nn.silu(zg)
    return o, s_fin


def rwkv7_mix(z, shift_prev, s0, mu, w0, w2, a0, a2, g2, k_k, k_a, r_k, ln_w, ln_b):
    bsz, t_len, _ = z.shape
    zf = z.astype(F32)
    z_prev = jnp.concatenate([shift_prev[:, None, :].astype(F32), zf[:, :-1]], axis=1)
    zm = zf + (z_prev - zf) * mu.astype(F32)
    splits = np.cumsum([RW_WIDTH, RW_WIDTH, RW_WIDTH, RW_DECAY_LORA, RW_A_LORA]).tolist()
    r, k, v, wd, ad, gd = jnp.split(zm, splits, axis=-1)
    w = -jax.nn.softplus(-(w0.astype(F32) + jnp.tanh(wd) @ w2.astype(F32))) - 0.5
    decay = jnp.exp(-jnp.exp(w))
    a = jax.nn.sigmoid(a0.astype(F32) + ad @ a2.astype(F32))
    g = jax.nn.sigmoid(gd) @ g2.astype(F32)
    heads = lambda x_: x_.reshape(bsz, t_len, RW_HEADS, RW_HEAD_DIM)
    kk = heads(k * k_k.astype(F32))
    kk = kk / jnp.maximum(jnp.sqrt(jnp.sum(kk * kk, axis=-1, keepdims=True)), 1e-12)
    k = k * (1.0 + (a - 1.0) * k_a.astype(F32))
    r_h, k_h, v_h, w_h, a_h = heads(r), heads(k), heads(v), heads(decay), heads(a)
    tm = lambda x_: x_.transpose(1, 0, 2, 3)

    def step(state, inp):
        r_t, w_t, k_t, v_t, kk_t, a_t = inp
        sa = jnp.einsum('bhvk,bhk->bhv', state, -kk_t)
        state = (state * w_t[:, :, None, :] + sa[..., None] * (kk_t * a_t)[:, :, None, :]
                 + v_t[..., None] * k_t[:, :, None, :])
        return state, jnp.einsum('bhvk,bhk->bhv', state, r_t)

    s_fin, y = lax.scan(step, s0.astype(F32), (tm(r_h), tm(w_h), tm(k_h), tm(v_h), tm(kk), tm(a_h)))
    y = y.transpose(1, 0, 2, 3)
    mean = jnp.mean(y, axis=-1, keepdims=True)
    var = jnp.mean(jnp.square(y - mean), axis=-1, keepdims=True)
    y = (y - mean) * lax.rsqrt(var + RW_GN_EPS)
    y = y * ln_w.astype(F32).reshape(RW_HEADS, RW_HEAD_DIM) + ln_b.astype(F32).reshape(RW_HEADS, RW_HEAD_DIM)
    y = y + jnp.sum(r_h * k_h * r_k.astype(F32), axis=-1, keepdims=True) * v_h
    out = y.reshape(bsz, t_len, RW_WIDTH) * g
    return out, s_fin, z[:, -1]


def conv_ffn(x, w_in, conv_w, conv_b, w_out, prev):
    t_len = x.shape[1]
    u = x @ w_in
    ug, uv = u[..., :D_FF], u[..., D_FF:]
    up = jnp.concatenate([prev.astype(ug.dtype), ug], axis=1)
    c = conv_b
    for j in range(CONV_W):
        c = c + conv_w[j] * up[:, j:j + t_len]
    hidden = jax.nn.gelu(c, approximate=False) * uv
    return hidden @ w_out, up[:, t_len:]


def trunk(x, p, s_hg, s_rw, s_sh, s_cv, wts):
    lb_all = jnp.cumsum(jax.nn.softmax(wts['lb_raw'].astype(F32), axis=0), axis=0)
    lb_all = lb_all - lb_all[:1]
    h = x
    new_hg, new_rw, new_sh, new_cv = [], [], [], []
    for i in range(DEPTH):
        zz = rmsnorm(h, wts['norm_mix'][i]) @ wts['w_in'][i]
        o_hg, n_hg = hgrn2_mix(zz[..., :HG_PROJ], lb_all[i], wts['hg_norm'][i], s_hg[i])
        o_rw, n_rw, n_sh = rwkv7_mix(zz[..., HG_PROJ:], s_sh[i], s_rw[i], wts['rw_mu'][i],
                                     wts['rw_w0'][i], wts['rw_w2'][i], wts['rw_a0'][i], wts['rw_a2'][i],
                                     wts['rw_g2'][i], wts['rw_k_k'][i], wts['rw_k_a'][i], wts['rw_r_k'][i],
                                     wts['rw_ln_w'][i], wts['rw_ln_b'][i])
        mix = jnp.concatenate([o_hg, o_rw], axis=-1).astype(h.dtype)
        h = h + mix @ wts['w_out'][i]
        f, n_cv = conv_ffn(rmsnorm(h, wts['norm_ffn'][i]), wts['ffn_in'][i], wts['ffn_conv_w'][i],
                           wts['ffn_conv_b'][i], wts['ffn_out'][i], s_cv[i])
        h = h + f
        gate = jax.nn.sigmoid(rmsnorm(h, wts['norm_ple'][i]) @ wts['ple_gate'][i])
        h = h + gate * (p[i] @ wts['ple_proj'][i])
        new_hg.append(n_hg.astype(s_hg.dtype))
        new_rw.append(n_rw.astype(s_rw.dtype))
        new_sh.append(n_sh.astype(s_sh.dtype))
        new_cv.append(n_cv.astype(s_cv.dtype))
    y = rmsnorm(h, wts['final_norm']).astype(x.dtype)
    return y, jnp.stack(new_hg), jnp.stack(new_rw), jnp.stack(new_sh), jnp.stack(new_cv)


def setup_inputs(seed: int = 0) -> dict:
    key = jax.random.key(seed)
    ks = jax.random.split(key, 34)
    nrm = lambda k, shape, s: jax.random.normal(k, shape, F32) * s
    gain = lambda k, shape: 1.0 + 0.02 * jax.random.normal(k, shape, F32)
    return {
        'x_prompt': nrm(ks[0], (BATCH, SEQ, D_MODEL), 1.0),
        'x_sample': nrm(ks[1], (DEC_BATCH, DEC_SEQ, D_MODEL), 1.0),
        'state_hgrn': nrm(ks[2], (DEPTH, DEC_BATCH, HG_HEADS, HG_HEAD_DIM, HG_HEAD_DIM), 0.5),
        'state_rwkv': nrm(ks[3], (DEPTH, DEC_BATCH, RW_HEADS, RW_HEAD_DIM, RW_HEAD_DIM), 0.5),
        'state_shift': nrm(ks[4], (DEPTH, DEC_BATCH, RW_PROJ), 1.0),
        'state_conv': nrm(ks[5], (DEPTH, DEC_BATCH, CONV_W - 1, D_FF), 1.0),
        'p_prompt': nrm(ks[6], (DEPTH, BATCH, SEQ, P_DIM), 1.0),
        'p_sample': nrm(ks[7], (DEPTH, DEC_BATCH, DEC_SEQ, P_DIM), 1.0),
        'norm_mix': gain(ks[8], (DEPTH, D_MODEL)),
        'w_in': nrm(ks[9], (DEPTH, D_MODEL, IN_PROJ), D_MODEL ** -0.5),
        'lb_raw': nrm(ks[10], (DEPTH, HG_WIDTH), 0.5),
        'hg_norm': gain(ks[11], (DEPTH, HG_WIDTH)),
        'rw_mu': jax.random.uniform(ks[12], (DEPTH, RW_PROJ), F32),
        'rw_w0': jax.random.uniform(ks[13], (DEPTH, RW_WIDTH), F32, -3.0, 0.5),
        'rw_w2': nrm(ks[14], (DEPTH, RW_DECAY_LORA, RW_WIDTH), RW_DECAY_LORA ** -0.5),
        'rw_a0': nrm(ks[15], (DEPTH, RW_WIDTH), 0.1),
        'rw_a2': nrm(ks[16], (DEPTH, RW_A_LORA, RW_WIDTH), RW_A_LORA ** -0.5),
        'rw_g2': nrm(ks[17], (DEPTH, RW_GATE_LORA, RW_WIDTH), RW_GATE_LORA ** -0.5),
        'rw_k_k': 0.85 + nrm(ks[18], (DEPTH, RW_WIDTH), 0.05),
        'rw_k_a': 1.0 + nrm(ks[19], (DEPTH, RW_WIDTH), 0.05),
        'rw_r_k': nrm(ks[20], (DEPTH, RW_HEADS, RW_HEAD_DIM), 0.1),
        'rw_ln_w': gain(ks[21], (DEPTH, RW_WIDTH)),
        'rw_ln_b': nrm(ks[22], (DEPTH, RW_WIDTH), 0.01),
        'w_out': nrm(ks[23], (DEPTH, MIX_WIDTH, D_MODEL), MIX_WIDTH ** -0.5),
        'norm_ffn': gain(ks[24], (DEPTH, D_MODEL)),
        'ffn_in': nrm(ks[25], (DEPTH, D_MODEL, 2 * D_FF), D_MODEL ** -0.5),
        'ffn_conv_w': nrm(ks[26], (DEPTH, CONV_W, D_FF), CONV_W ** -0.5),
        'ffn_conv_b': nrm(ks[27], (DEPTH, D_FF), 0.01),
        'ffn_out': nrm(ks[28], (DEPTH, D_FF, D_MODEL), D_FF ** -0.5),
        'norm_ple': gain(ks[29], (DEPTH, D_MODEL)),
        'ple_gate': nrm(ks[30], (DEPTH, D_MODEL, D_MODEL), D_MODEL ** -0.5),
        'ple_proj': nrm(ks[31], (DEPTH, P_DIM, D_MODEL), P_DIM ** -0.5),
        'final_norm': gain(ks[32], (D_MODEL,)),
    }


def reference(x_prompt, x_sample, state_hgrn, state_rwkv, state_shift, state_conv, p_prompt, p_sample,
              norm_mix, w_in, lb_raw, hg_norm, rw_mu, rw_w0, rw_w2, rw_a0, rw_a2, rw_g2, rw_k_k, rw_k_a,
              rw_r_k, rw_ln_w, rw_ln_b, w_out, norm_ffn, ffn_in, ffn_conv_w, ffn_conv_b, ffn_out,
              norm_ple, ple_gate, ple_proj, final_norm):
    wts = dict(norm_mix=norm_mix, w_in=w_in, lb_raw=lb_raw, hg_norm=hg_norm, rw_mu=rw_mu, rw_w0=rw_w0,
               rw_w2=rw_w2, rw_a0=rw_a0, rw_a2=rw_a2, rw_g2=rw_g2, rw_k_k=rw_k_k, rw_k_a=rw_k_a,
               rw_r_k=rw_r_k, rw_ln_w=rw_ln_w, rw_ln_b=rw_ln_b, w_out=w_out, norm_ffn=norm_ffn,
               ffn_in=ffn_in, ffn_conv_w=ffn_conv_w, ffn_conv_b=ffn_conv_b, ffn_out=ffn_out,
               norm_ple=norm_ple, ple_gate=ple_gate, ple_proj=ple_proj, final_norm=final_norm)
    nb = x_prompt.shape[0]
    z_hg = jnp.zeros((DEPTH, nb) + state_hgrn.shape[2:], state_hgrn.dtype)
    z_rw = jnp.zeros((DEPTH, nb) + state_rwkv.shape[2:], state_rwkv.dtype)
    z_sh = jnp.zeros((DEPTH, nb) + state_shift.shape[2:], state_shift.dtype)
    z_cv = jnp.zeros((DEPTH, nb) + state_conv.shape[2:], state_conv.dtype)
    y_prompt, hg_p, rw_p, sh_p, cv_p = trunk(x_prompt, p_prompt, z_hg, z_rw, z_sh, z_cv, wts)
    y_sample, hg_s, rw_s, sh_s, cv_s = trunk(x_sample, p_sample, state_hgrn, state_rwkv,
                                             state_shift, state_conv, wts)
    return (y_prompt, y_sample, hg_p, rw_p, sh_p, cv_p, hg_s, rw_s, sh_s, cv_s)
```

```python
import functools

import jax
import jax.numpy as jnp
from jax import lax
from jax.experimental import pallas as pl
from jax.experimental.pallas import tpu as pltpu

F32 = jnp.float32
BF16 = jnp.bfloat16

NORM_EPS = 1e-6
RW_GN_EPS = 64e-5
HG_HEAD_DIM = 128
RW_HEAD_DIM = 64
RW_LORA_W = 64
CONV_W = 3
HG_BLOCK = 16
RW_CHUNK = 64
RW_SUB = 16
FF_COLS = 256
LANES = 128
SUBLANES = 8
VMEM_LIMIT = 56 * 1024 * 1024

NN = ((1,), (0,))
NT = ((1,), (1,))
TN = ((0,), (0,))


def _dot(a, b, dims=NN):
    return lax.dot_general(a, b, (dims, ((), ())), preferred_element_type=F32)


def _split2(x):
    hi = x.astype(BF16)
    lo = (x - hi.astype(F32)).astype(BF16)
    return hi, lo


def _split3(x):
    hi = x.astype(BF16)
    r1 = x - hi.astype(F32)
    mid = r1.astype(BF16)
    lo = (r1 - mid.astype(F32)).astype(BF16)
    return hi, mid, lo


def _dot3(a, b, dims=NN):
    ah, al = _split2(a)
    bh, bl = _split2(b)
    return _dot(ah, bh, dims) + (_dot(ah, bl, dims) + _dot(al, bh, dims))


def _dot_xr(x, b_exact, dims=NN):
    x0, x1, x2 = _split3(x)
    return _dot(x0, b_exact, dims) + (_dot(x1, b_exact, dims) + _dot(x2, b_exact, dims))


def _dot_xl(a_exact, x, dims=NN):
    x0, x1, x2 = _split3(x)
    return _dot(a_exact, x0, dims) + (_dot(a_exact, x1, dims) + _dot(a_exact, x2, dims))


def _rmsnorm(x, g):
    return x * lax.rsqrt(jnp.mean(x * x, axis=-1, keepdims=True) + NORM_EPS) * g


def _iota2(shape, axis):
    return lax.broadcasted_iota(jnp.int32, shape, axis)


def _const_spec(shape):
    zeros = (0,) * len(shape)
    return pl.BlockSpec(shape, lambda *_: zeros, pipeline_mode=pl.Buffered(1))


def _inproj_kernel(x_ref, g_ref, whg_ref, wrw_ref, zhg_ref, zrw_ref):
    xn = _rmsnorm(x_ref[...], g_ref[...]).astype(BF16)
    zhg_ref[...] = _dot(xn, whg_ref[...])
    zrw_ref[...] = _dot(xn, wrw_ref[...])


def _in_proj(x2, g, w_hg, w_rw):
    n, d = x2.shape
    tm = min(512, n)
    assert n % tm == 0
    return pl.pallas_call(
        _inproj_kernel,
        grid=(n // tm,),
        in_specs=[pl.BlockSpec((tm, d), lambda i: (i, 0)),
                  _const_spec((1, d)),
                  _const_spec(w_hg.shape),
                  _const_spec(w_rw.shape)],
        out_specs=[pl.BlockSpec((tm, w_hg.shape[1]), lambda i: (i, 0)),
                   pl.BlockSpec((tm, w_rw.shape[1]), lambda i: (i, 0))],
        out_shape=[jax.ShapeDtypeStruct((n, w_hg.shape[1]), F32),
                   jax.ShapeDtypeStruct((n, w_rw.shape[1]), F32)],
        compiler_params=pltpu.CompilerParams(dimension_semantics=("parallel",),
                                             vmem_limit_bytes=VMEM_LIMIT),
        name="in_proj",
    )(x2, g, w_hg, w_rw)


def _hgrn_kernel(z_ref, loglb_ref, omlb_ref, gn_ref, s0_ref, o_ref, sfin_ref,
                 st_ref, q_ref, k_ref, bl_ref, oacc_ref, *, n_heads, tile):
    dk = HG_HEAD_DIM
    width = n_heads * dk
    blk = HG_BLOCK
    t_idx = pl.program_id(1)

    @pl.when(t_idx == 0)
    def _():
        for h in range(n_heads):
            st_ref[h] = s0_ref[0, h].T

    zq = z_ref[0, :, 0:width]
    zf = z_ref[0, :, width:2 * width]
    q_ref[...] = zq * jax.nn.sigmoid(zq)
    soft = jnp.log1p(jnp.exp(-jnp.abs(zf)))
    log_sig_pos = jnp.minimum(zf, 0.0) - soft
    log_sig_neg = jnp.minimum(-zf, 0.0) - soft
    lo_term = loglb_ref[...] + log_sig_neg
    top = jnp.maximum(log_sig_pos, lo_term)
    logf = top + jnp.log1p(jnp.exp(-jnp.abs(log_sig_pos - lo_term)))
    k_ref[...] = omlb_ref[...] * jax.nn.sigmoid(-zf)
    row = _iota2((tile, tile), 0)
    col = _iota2((tile, tile), 1)
    tri = ((row // blk == col // blk) & (col <= row)).astype(BF16)
    bl_ref[...] = _dot_xl(tri, logf)

    ones = jnp.ones((dk, dk), BF16)
    t_row = _iota2((blk, dk), 0)

    def block_step(i, carry):
        r0 = pl.multiple_of(i * blk, blk)
        for h in range(n_heads):
            lanes = slice(h * dk, (h + 1) * dk)
            qb = q_ref[pl.ds(r0, blk), lanes]
            kb = k_ref[pl.ds(r0, blk), lanes]
            bb = bl_ref[pl.ds(r0, blk), lanes]
            vb = z_ref[0, pl.ds(r0, blk), 2 * width + h * dk:2 * width + (h + 1) * dk]
            e_end = bb[blk - 1:blk, :]
            st = st_ref[h]
            o_inter = _dot(qb * jnp.exp(bb), st, NT)
            parts = []
            for s in range(blk):
                causal = t_row >= s
                dd = jnp.where(causal, bb - bb[s:s + 1, :], 0.0)
                parts.append(jnp.where(causal, qb * kb[s:s + 1, :] * jnp.exp(dd), 0.0))
            pmat = jnp.concatenate(parts, axis=0)
            p_hi, p_lo = _split2(pmat)
            rsum = _dot(p_hi, ones) + _dot(p_lo, ones)
            o_diag = rsum[0:blk, :] * vb[0:1, :]
            for s in range(1, blk):
                o_diag = o_diag + rsum[s * blk:(s + 1) * blk, :] * vb[s:s + 1, :]
            oacc_ref[pl.ds(r0, blk), lanes] = o_inter + o_diag
            ke = kb * jnp.exp(e_end - bb)
            st_ref[h] = st * jnp.exp(e_end) + _dot(vb, ke, TN)
        return carry

    lax.fori_loop(0, tile // blk, block_step, 0)

    zg = z_ref[0, :, 3 * width:4 * width]
    gate = gn_ref[...] * (zg * jax.nn.sigmoid(zg))
    for h in range(n_heads):
        lanes = slice(h * dk, (h + 1) * dk)
        o = oacc_ref[:, lanes]
        o = o * lax.rsqrt(jnp.mean(o * o, axis=-1, keepdims=True) + NORM_EPS)
        o_ref[0, :, lanes] = o * gate[:, lanes]

    for h in range(n_heads):
        sfin_ref[0, h] = st_ref[h].T


def _hgrn2(z, log_lb, one_m_lb, norm_g, s0):
    bsz, t_len, proj = z.shape
    width = proj // 4
    n_heads = width // HG_HEAD_DIM
    tile = min(128, t_len)
    assert t_len % tile == 0 and tile % HG_BLOCK == 0
    kern = functools.partial(_hgrn_kernel, n_heads=n_heads, tile=tile)
    state_spec = pl.BlockSpec((1, n_heads, HG_HEAD_DIM, HG_HEAD_DIM), lambda b, t: (b, 0, 0, 0))
    return pl.pallas_call(
        kern,
        grid=(bsz, t_len // tile),
        in_specs=[pl.BlockSpec((1, tile, proj), lambda b, t: (b, t, 0)),
                  _const_spec((1, width)), _const_spec((1, width)), _const_spec((1, width)),
                  state_spec],
        out_specs=[pl.BlockSpec((1, tile, width), lambda b, t: (b, t, 0)), state_spec],
        out_shape=[jax.ShapeDtypeStruct((bsz, t_len, width), F32),
                   jax.ShapeDtypeStruct(s0.shape, F32)],
        scratch_shapes=[pltpu.VMEM((n_heads, HG_HEAD_DIM, HG_HEAD_DIM), F32),
                        pltpu.VMEM((tile, width), F32), pltpu.VMEM((tile, width), F32),
                        pltpu.VMEM((tile, width), F32), pltpu.VMEM((tile, width), F32)],
        compiler_params=pltpu.CompilerParams(dimension_semantics=("parallel", "arbitrary"),
                                             vmem_limit_bytes=VMEM_LIMIT),
        name="hgrn2",
    )(z, log_lb, one_m_lb, norm_g, s0)


def _unit_lower_inverse(n_strict, eye):
    size = n_strict.shape[0]
    row = _iota2((size, size), 0)
    col = _iota2((size, size), 1)
    same = (row // RW_SUB) == (col // RW_SUB)
    d = jnp.where(same, n_strict, 0.0)
    l_off = n_strict - d
    t_d = eye - d
    d_pow = d
    steps = RW_SUB.bit_length() - 2
    for _ in range(steps):
        d_pow = _dot3(d_pow, d_pow)
        t_d = t_d + _dot3(t_d, d_pow)
    p = _dot3(t_d, l_off)
    p2 = _dot3(p, p)
    imp = eye - p
    t_p = imp + _dot3(imp, p2)
    return _dot3(t_p, t_d)


def _rwkv_kernel(z_ref, shift_ref, s0_ref, mu_ref, w0_ref, w2_ref, a0_ref, a2_ref, g2_ref,
                 kk_ref, ka_ref, rk_ref, lnw_ref, lnb_ref, o_ref, sfin_ref,
                 s_ref, carry_ref, zs_ref, y_ref, *, n_heads, chunk):
    hd = RW_HEAD_DIM
    width = n_heads * hd
    t_idx = pl.program_id(1)

    @pl.when(t_idx == 0)
    def _():
        s_ref[...] = s0_ref[0]
        carry_ref[SUBLANES - 1:SUBLANES, :] = shift_ref[0]

    z = z_ref[0]
    zs_ref[0:SUBLANES, :] = carry_ref[...]
    zs_ref[SUBLANES:SUBLANES + chunk, :] = z
    carry_ref[...] = z[chunk - SUBLANES:chunk, :]
    z_prev = zs_ref[SUBLANES - 1:SUBLANES - 1 + chunk, :]
    zm = z + (z_prev - z) * mu_ref[...]

    r = zm[:, 0:width]
    k = zm[:, width:2 * width]
    v = zm[:, 2 * width:3 * width]
    lora_in = zm[:, 3 * width:3 * width + 2 * RW_LORA_W]
    gd = zm[:, 3 * width + 2 * RW_LORA_W:]

    w_pre = w0_ref[...] + _dot3(jnp.tanh(lora_in), w2_ref[...])
    w_raw = -jax.nn.softplus(-w_pre) - 0.5
    logw = -jnp.exp(w_raw)
    a = jax.nn.sigmoid(a0_ref[...] + _dot3(lora_in, a2_ref[...]))
    g = _dot(jax.nn.sigmoid(gd).astype(BF16), g2_ref[...])

    hrow = _iota2((width, width), 0) // hd
    hcol = _iota2((width, width), 1) // hd
    head_ones = (hrow == hcol).astype(BF16)

    kk = k * kk_ref[...]
    kk = kk / jnp.maximum(jnp.sqrt(_dot_xr(kk * kk, head_ones)), 1e-12)
    k2 = k * (1.0 + (a - 1.0) * ka_ref[...])

    row = _iota2((chunk, chunk), 0)
    col = _iota2((chunk, chunk), 1)
    incl = col <= row
    strict = col < row
    eye = (row == col).astype(F32)
    c = _dot_xl(incl.astype(BF16), logw)
    c_last = c[chunk - 1:chunk, :]
    e_neg = jnp.exp(-c)
    e_end = jnp.exp(c_last - c)
    beta = kk * a
    k_t = kk * jnp.exp(c - logw)
    r_t = r * jnp.exp(c)
    b_h = beta * e_neg
    k_h = k2 * e_neg
    b_l = beta * e_end
    k_l = k2 * e_end
    g_last = jnp.exp(c_last)

    for h in range(n_heads):
        lanes = slice(h * hd, (h + 1) * hd)
        kt_h, rt_h, bh_h, kh_h = k_t[:, lanes], r_t[:, lanes], b_h[:, lanes], k_h[:, lanes]
        v_h, bl_h, kl_h = v[:, lanes], b_l[:, lanes], k_l[:, lanes]
        a_ab = jnp.where(strict, _dot3(kt_h, bh_h, NT), 0.0)
        a_ak = jnp.where(strict, _dot3(kt_h, kh_h, NT), 0.0)
        a_rb = jnp.where(incl, _dot3(rt_h, bh_h, NT), 0.0)
        a_rk = jnp.where(incl, _dot3(rt_h, kh_h, NT), 0.0)
        t_inv = _unit_lower_inverse(a_ab, eye)
        w_t = _dot3(t_inv, kt_h)
        u_loc = _dot3(t_inv, _dot3(a_ak, v_h))
        s_h = s_ref[h]
        u = -(_dot3(w_t, s_h, NT) + u_loc)
        y_ref[:, lanes] = _dot3(rt_h, s_h, NT) + _dot3(a_rb, u) + _dot3(a_rk, v_h)
        s_ref[h] = s_h * g_last[:, lanes] + _dot3(u, bl_h, TN) + _dot3(v_h, kl_h, TN)

    y = y_ref[...]
    mean = _dot_xr(y, head_ones) * (1.0 / hd)
    yc = y - mean
    var = _dot_xr(yc * yc, head_ones) * (1.0 / hd)
    yn = yc * lax.rsqrt(var + RW_GN_EPS) * lnw_ref[...] + lnb_ref[...]
    bonus = _dot_xr(r * k2 * rk_ref[...], head_ones) * v
    o_ref[0] = (yn + bonus) * g
    sfin_ref[0] = s_ref[...]


def _rwkv7(z, shift_prev, s0, mu, w0, w2p, a0, a2p, g2, k_k, k_a, r_k, ln_w, ln_b):
    bsz, t_len, proj = z.shape
    n_heads = s0.shape[1]
    width = n_heads * RW_HEAD_DIM
    chunk = RW_CHUNK
    assert t_len % chunk == 0
    kern = functools.partial(_rwkv_kernel, n_heads=n_heads, chunk=chunk)
    state_spec = pl.BlockSpec((1, n_heads, RW_HEAD_DIM, RW_HEAD_DIM), lambda b, t: (b, 0, 0, 0))
    vec = lambda a: _const_spec(a.shape)
    return pl.pallas_call(
        kern,
        grid=(bsz, t_len // chunk),
        in_specs=[pl.BlockSpec((1, chunk, proj), lambda b, t: (b, t, 0)),
                  pl.BlockSpec((1, 1, proj), lambda b, t: (b, 0, 0)),
                  state_spec,
                  vec(mu), vec(w0), vec(w2p), vec(a0), vec(a2p), vec(g2),
                  vec(k_k), vec(k_a), vec(r_k), vec(ln_w), vec(ln_b)],
        out_specs=[pl.BlockSpec((1, chunk, width), lambda b, t: (b, t, 0)), state_spec],
        out_shape=[jax.ShapeDtypeStruct((bsz, t_len, width), F32),
                   jax.ShapeDtypeStruct(s0.shape, F32)],
        scratch_shapes=[pltpu.VMEM((n_heads, RW_HEAD_DIM, RW_HEAD_DIM), F32),
                        pltpu.VMEM((SUBLANES, proj), F32),
                        pltpu.VMEM((SUBLANES + chunk, proj), F32),
                        pltpu.VMEM((chunk, width), F32)],
        compiler_params=pltpu.CompilerParams(dimension_semantics=("parallel", "arbitrary"),
                                             vmem_limit_bytes=VMEM_LIMIT),
        name="rwkv7",
    )(z, shift_prev, s0, mu, w0, w2p, a0, a2p, g2, k_k, k_a, r_k, ln_w, ln_b)


def _gelu_exact(x):
    return 0.5 * x * (1.0 + lax.erf(x * (2.0 ** -0.5)))


def _post_kernel(h_ref, ohg_ref, orw_ref, p_ref, cvprev_ref,
                 wohg_ref, worw_ref, nffn_ref, wg_ref, wv_ref, cw_ref, cb_ref, wo_ref,
                 nple_ref, pgate_ref, pproj_ref, fnorm_ref,
                 hout_ref, cvnew_ref, carry_ref, ugs_ref, *, tm, n_col_blocks, apply_final_norm):
    t_idx = pl.program_id(1)
    cols = FF_COLS

    @pl.when(t_idx == 0)
    def _():
        for j in range(n_col_blocks):
            carry_ref[j, SUBLANES - (CONV_W - 1):SUBLANES, :] = cvprev_ref[0, :, j * cols:(j + 1) * cols]

    h1 = (h_ref[0] + _dot(ohg_ref[0].astype(BF16), wohg_ref[...])
          + _dot(orw_ref[0].astype(BF16), worw_ref[...]))
    xn = _rmsnorm(h1, nffn_ref[...]).astype(BF16)
    acc = jnp.zeros_like(h1)
    for j in range(n_col_blocks):
        ug = _dot(xn, wg_ref[j])
        uv = _dot(xn, wv_ref[j])
        ugs_ref[0:SUBLANES, :] = carry_ref[j]
        ugs_ref[SUBLANES:SUBLANES + tm, :] = ug
        carry_ref[j] = ug[tm - SUBLANES:tm, :]
        cvnew_ref[0, :, j * cols:(j + 1) * cols] = ug[tm - (CONV_W - 1):tm, :]
        cw = cw_ref[j]
        c = (cb_ref[j] + cw[0:1, :] * ugs_ref[SUBLANES - 2:SUBLANES - 2 + tm, :]
             + cw[1:2, :] * ugs_ref[SUBLANES - 1:SUBLANES - 1 + tm, :] + cw[2:3, :] * ug)
        hid = (_gelu_exact(c) * uv).astype(BF16)
        acc = acc + _dot(hid, wo_ref[j])
    h2 = h1 + acc
    gate = jax.nn.sigmoid(_dot(_rmsnorm(h2, nple_ref[...]).astype(BF16), pgate_ref[...]))
    h3 = h2 + gate * _dot(p_ref[0].astype(BF16), pproj_ref[...])
    if apply_final_norm:
        h3 = _rmsnorm(h3, fnorm_ref[...])
    hout_ref[0] = h3


def _post(h, o_hg, o_rw, p, cv_prev, wo_hg, wo_rw, norm_ffn, wg, wv, cw, cb, wo, norm_ple,
          ple_gate, ple_proj, final_norm, apply_final_norm):
    bsz, t_len, d = h.shape
    tm = min(512, t_len)
    assert t_len % tm == 0 and tm >= SUBLANES
    n_col_blocks = wg.shape[0]
    d_ff = n_col_blocks * FF_COLS
    kern = functools.partial(_post_kernel, tm=tm, n_col_blocks=n_col_blocks,
                             apply_final_norm=apply_final_norm)
    tok = lambda w: pl.BlockSpec((1, tm, w), lambda b, t: (b, t, 0))
    cv_spec = pl.BlockSpec((1, CONV_W - 1, d_ff), lambda b, t: (b, 0, 0))
    consts = [wo_hg, wo_rw, norm_ffn, wg, wv, cw, cb, wo, norm_ple, ple_gate, ple_proj, final_norm]
    return pl.pallas_call(
        kern,
        grid=(bsz, t_len // tm),
        in_specs=[tok(d), tok(o_hg.shape[2]), tok(o_rw.shape[2]), tok(p.shape[2]), cv_spec]
                 + [_const_spec(a.shape) for a in consts],
        out_specs=[tok(d), cv_spec],
        out_shape=[jax.ShapeDtypeStruct(h.shape, F32), jax.ShapeDtypeStruct(cv_prev.shape, F32)],
        scratch_shapes=[pltpu.VMEM((n_col_blocks, SUBLANES, FF_COLS), F32),
                        pltpu.VMEM((SUBLANES + tm, FF_COLS), F32)],
        compiler_params=pltpu.CompilerParams(dimension_semantics=("parallel", "arbitrary"),
                                             vmem_limit_bytes=VMEM_LIMIT),
        name="post",
    )(h, o_hg, o_rw, p, cv_prev, *consts)


def _layer_params(i, hg_proj, lb_all, w):
    d_ff = w["ffn_conv_b"].shape[1]
    n_cb = d_ff // FF_COLS
    rw_width = w["rw_w0"].shape[1]
    hg_width = w["hg_norm"].shape[1]
    row = lambda a: a.reshape(1, -1)
    lb = jnp.maximum(lb_all[i], 0.0)
    zeros_lora = jnp.zeros((RW_LORA_W, rw_width), F32)
    ffn_in = w["ffn_in"][i].astype(BF16)
    d_model = ffn_in.shape[0]
    blocked = lambda a: a.reshape(d_model, n_cb, FF_COLS).transpose(1, 0, 2)
    return dict(
        norm_mix=row(w["norm_mix"][i]),
        w_in_hg=w["w_in"][i][:, :hg_proj].astype(BF16),
        w_in_rw=w["w_in"][i][:, hg_proj:].astype(BF16),
        log_lb=row(jnp.log(lb)), one_m_lb=row(1.0 - lb), hg_norm=row(w["hg_norm"][i]),
        mu=row(w["rw_mu"][i]), w0=row(w["rw_w0"][i]),
        w2p=jnp.concatenate([w["rw_w2"][i], zeros_lora], axis=0),
        a0=row(w["rw_a0"][i]),
        a2p=jnp.concatenate([zeros_lora, w["rw_a2"][i]], axis=0),
        g2=w["rw_g2"][i].astype(BF16),
        k_k=row(w["rw_k_k"][i]), k_a=row(w["rw_k_a"][i]), r_k=row(w["rw_r_k"][i]),
        ln_w=row(w["rw_ln_w"][i]), ln_b=row(w["rw_ln_b"][i]),
        wo_hg=w["w_out"][i][:hg_width].astype(BF16), wo_rw=w["w_out"][i][hg_width:].astype(BF16),
        norm_ffn=row(w["norm_ffn"][i]),
        wg=blocked(ffn_in[:, :d_ff]), wv=blocked(ffn_in[:, d_ff:]),
        cw=w["ffn_conv_w"][i].reshape(CONV_W, n_cb, FF_COLS).transpose(1, 0, 2),
        cb=w["ffn_conv_b"][i].reshape(n_cb, 1, FF_COLS),
        wo=w["ffn_out"][i].astype(BF16).reshape(n_cb, FF_COLS, d_model),
        norm_ple=row(w["norm_ple"][i]),
        ple_gate=w["ple_gate"][i].astype(BF16), ple_proj=w["ple_proj"][i].astype(BF16),
        final_norm=row(w["final_norm"]),
    )


def _trunk(x, p, s_hg, s_rw, s_sh, s_cv, layers):
    bsz, t_len, d = x.shape
    depth = len(layers)
    h = x
    new_hg, new_rw, new_sh, new_cv = [], [], [], []
    for i, lp in enumerate(layers):
        z_hg, z_rw = _in_proj(h.reshape(bsz * t_len, d), lp["norm_mix"], lp["w_in_hg"], lp["w_in_rw"])
        z_hg = z_hg.reshape(bsz, t_len, -1)
        z_rw = z_rw.reshape(bsz, t_len, -1)
        o_hg, n_hg = _hgrn2(z_hg, lp["log_lb"], lp["one_m_lb"], lp["hg_norm"], s_hg[i])
        o_rw, n_rw = _rwkv7(z_rw, s_sh[i][:, None, :], s_rw[i], lp["mu"], lp["w0"], lp["w2p"],
                            lp["a0"], lp["a2p"], lp["g2"], lp["k_k"], lp["k_a"], lp["r_k"],
                            lp["ln_w"], lp["ln_b"])
        h, n_cv = _post(h, o_hg, o_rw, p[i], s_cv[i], lp["wo_hg"], lp["wo_rw"], lp["norm_ffn"],
                        lp["wg"], lp["wv"], lp["cw"], lp["cb"], lp["wo"], lp["norm_ple"],
                        lp["ple_gate"], lp["ple_proj"], lp["final_norm"], i == depth - 1)
        new_hg.append(n_hg)
        new_rw.append(n_rw)
        new_sh.append(z_rw[:, -1])
        new_cv.append(n_cv)
    return h, jnp.stack(new_hg), jnp.stack(new_rw), jnp.stack(new_sh), jnp.stack(new_cv)


def kernel(x_prompt, x_sample, state_hgrn, state_rwkv, state_shift, state_conv, p_prompt, p_sample,
           norm_mix, w_in, lb_raw, hg_norm, rw_mu, rw_w0, rw_w2, rw_a0, rw_a2, rw_g2, rw_k_k, rw_k_a,
           rw_r_k, rw_ln_w, rw_ln_b, w_out, norm_ffn, ffn_in, ffn_conv_w, ffn_conv_b, ffn_out,
           norm_ple, ple_gate, ple_proj, final_norm):
    w = dict(norm_mix=norm_mix, w_in=w_in, hg_norm=hg_norm, rw_mu=rw_mu, rw_w0=rw_w0, rw_w2=rw_w2,
             rw_a0=rw_a0, rw_a2=rw_a2, rw_g2=rw_g2, rw_k_k=rw_k_k, rw_k_a=rw_k_a,
             rw_r_k=rw_r_k.reshape(rw_r_k.shape[0], -1), rw_ln_w=rw_ln_w, rw_ln_b=rw_ln_b, w_out=w_out,
             norm_ffn=norm_ffn, ffn_in=ffn_in, ffn_conv_w=ffn_conv_w, ffn_conv_b=ffn_conv_b,
             ffn_out=ffn_out, norm_ple=norm_ple, ple_gate=ple_gate, ple_proj=ple_proj,
             final_norm=final_norm)
    depth = w_in.shape[0]
    hg_proj = 4 * hg_norm.shape[1]
    lb_all = jnp.cumsum(jax.nn.softmax(lb_raw.astype(F32), axis=0), axis=0)
    lb_all = lb_all - lb_all[:1]
    layers = [_layer_params(i, hg_proj, lb_all, w) for i in range(depth)]

    nb = x_prompt.shape[0]
    zeros_like_state = lambda s: jnp.zeros((depth, nb) + s.shape[2:], s.dtype)
    y_p, hg_p, rw_p, sh_p, cv_p = _trunk(x_prompt, p_prompt, zeros_like_state(state_hgrn),
                                         zeros_like_state(state_rwkv), zeros_like_state(state_shift),
                                         zeros_like_state(state_conv), layers)
    y_s, hg_s, rw_s, sh_s, cv_s = _trunk(x_sample, p_sample, state_hgrn, state_rwkv, state_shift,
                                         state_conv, layers)
    return (y_p, y_s, hg_p, rw_p, sh_p, cv_p, hg_s, rw_s, sh_s, cv_s)
```

```python
import functools

import jax
import jax.numpy as jnp
from jax import lax
from jax.experimental import pallas as pl
from jax.experimental.pallas import tpu as pltpu

F32 = jnp.float32
BF16 = jnp.bfloat16

NORM_EPS = 1e-6
RW_GN_EPS = 64e-5
HG_HEAD_DIM = 128
RW_HEAD_DIM = 64
RW_LORA_W = 64
CONV_W = 3
HG_BLOCK = 16
RW_CHUNK = 64
RW_SUB = 16
RW_TILE = 128
FF_COLS = 256
LANES = 128
SUBLANES = 8
VMEM_LIMIT = 56 * 1024 * 1024

NN = ((1,), (0,))
NT = ((1,), (1,))
TN = ((0,), (0,))


def _dot(a, b, dims=NN):
    return lax.dot_general(a, b, (dims, ((), ())), preferred_element_type=F32)


def _dot1(a, b, dims=NN):
    return _dot(a.astype(BF16), b.astype(BF16), dims)


def _split2(x):
    hi = x.astype(BF16)
    lo = (x - hi.astype(F32)).astype(BF16)
    return hi, lo


def _split3(x):
    hi = x.astype(BF16)
    r1 = x - hi.astype(F32)
    mid = r1.astype(BF16)
    lo = (r1 - mid.astype(F32)).astype(BF16)
    return hi, mid, lo


def _dot3(a, b, dims=NN):
    ah, al = _split2(a)
    bh, bl = _split2(b)
    return _dot(ah, bh, dims) + (_dot(ah, bl, dims) + _dot(al, bh, dims))


def _dot_xr(x, b_exact, dims=NN):
    x0, x1, x2 = _split3(x)
    return _dot(x0, b_exact, dims) + (_dot(x1, b_exact, dims) + _dot(x2, b_exact, dims))


def _dot_xl(a_exact, x, dims=NN):
    x0, x1, x2 = _split3(x)
    return _dot(a_exact, x0, dims) + (_dot(a_exact, x1, dims) + _dot(a_exact, x2, dims))


def _rmsnorm(x, g):
    return x * lax.rsqrt(jnp.mean(x * x, axis=-1, keepdims=True) + NORM_EPS) * g


def _iota2(shape, axis):
    return lax.broadcasted_iota(jnp.int32, shape, axis)


def _const_spec(shape):
    zeros = (0,) * len(shape)
    return pl.BlockSpec(shape, lambda *_: zeros, pipeline_mode=pl.Buffered(1))


def _inproj_kernel(x_ref, g_ref, whg_ref, wrw_ref, zhg_ref, zrw_ref):
    xn = _rmsnorm(x_ref[...], g_ref[...]).astype(BF16)
    zhg_ref[...] = _dot(xn, whg_ref[...])
    zrw_ref[...] = _dot(xn, wrw_ref[...])


def _in_proj(x2, g, w_hg, w_rw):
    n, d = x2.shape
    tm = min(512, n)
    assert n % tm == 0
    return pl.pallas_call(
        _inproj_kernel,
        grid=(n // tm,),
        in_specs=[pl.BlockSpec((tm, d), lambda i: (i, 0)),
                  _const_spec((1, d)),
                  _const_spec(w_hg.shape),
                  _const_spec(w_rw.shape)],
        out_specs=[pl.BlockSpec((tm, w_hg.shape[1]), lambda i: (i, 0)),
                   pl.BlockSpec((tm, w_rw.shape[1]), lambda i: (i, 0))],
        out_shape=[jax.ShapeDtypeStruct((n, w_hg.shape[1]), F32),
                   jax.ShapeDtypeStruct((n, w_rw.shape[1]), F32)],
        compiler_params=pltpu.CompilerParams(dimension_semantics=("parallel",),
                                             vmem_limit_bytes=VMEM_LIMIT),
        name="in_proj",
    )(x2, g, w_hg, w_rw)


def _hgrn_kernel(z_ref, loglb_ref, omlb_ref, gn_ref, sel_ref, s0_ref, o_ref, sfin_ref,
                 st_ref, q_ref, k_ref, bl_ref, oacc_ref, p_ref, a_ref, kv_ref, *, n_heads, tile):
    dk = HG_HEAD_DIM
    width = n_heads * dk
    blk = HG_BLOCK
    t_idx = pl.program_id(1)

    @pl.when(t_idx == 0)
    def _():
        for h in range(n_heads):
            st_ref[h] = s0_ref[0, h].T

    zq = z_ref[0, :, 0:width]
    zf = z_ref[0, :, width:2 * width]
    q_ref[...] = zq * jax.nn.sigmoid(zq)
    soft = jnp.log(1.0 + jnp.exp(-jnp.abs(zf)))
    log_sig_pos = jnp.minimum(zf, 0.0) - soft
    log_sig_neg = jnp.minimum(-zf, 0.0) - soft
    lo_term = loglb_ref[...] + log_sig_neg
    top = jnp.maximum(log_sig_pos, lo_term)
    logf = top + jnp.log(1.0 + jnp.exp(-jnp.abs(log_sig_pos - lo_term)))
    k_ref[...] = omlb_ref[...] * jnp.exp(log_sig_neg)
    row = _iota2((tile, tile), 0)
    col = _iota2((tile, tile), 1)
    tri = ((row // blk == col // blk) & (col <= row)).astype(BF16)
    bl_ref[...] = _dot_xl(tri, logf)

    t_row = _iota2((blk, dk), 0)

    def build_step(i, carry):
        r0 = pl.multiple_of(i * blk, blk)
        for h in range(n_heads):
            lanes = slice(h * dk, (h + 1) * dk)
            qb = q_ref[pl.ds(r0, blk), lanes]
            kb = k_ref[pl.ds(r0, blk), lanes]
            bb = bl_ref[pl.ds(r0, blk), lanes]
            vb = z_ref[0, pl.ds(r0, blk), 2 * width + h * dk:2 * width + (h + 1) * dk]
            p0 = pl.multiple_of((i * n_heads + h) * blk, blk)
            for s in range(blk):
                causal = t_row >= s
                dd = jnp.where(causal, bb - bb[s:s + 1, :], 0.0)
                ps = jnp.where(causal, qb * kb[s:s + 1, :] * jnp.exp(dd), 0.0)
                p_ref[pl.ds(p0, blk), s * dk:(s + 1) * dk] = ps.astype(BF16)
            q_ref[pl.ds(r0, blk), lanes] = qb * jnp.exp(bb)
            ke = kb * jnp.exp(bb[blk - 1:blk, :] - bb)
            kv_ref[i * n_heads + h] = _dot1(vb, ke, TN)
        return carry

    lax.fori_loop(0, tile // blk, build_step, 0)
    a_ref[...] = _dot(p_ref[...], sel_ref[...])

    state = [st_ref[h] for h in range(n_heads)]
    for i in range(tile // blk):
        r0 = i * blk
        for h in range(n_heads):
            lanes = slice(h * dk, (h + 1) * dk)
            qe = q_ref[r0:r0 + blk, lanes]
            vb = z_ref[0, r0:r0 + blk, 2 * width + h * dk:2 * width + (h + 1) * dk]
            p0 = (i * n_heads + h) * blk
            a_blk = a_ref[p0:p0 + blk, 0:blk]
            e_end = bl_ref[r0 + blk - 1:r0 + blk, lanes]
            oacc_ref[r0:r0 + blk, lanes] = _dot1(qe, state[h], NT) + _dot1(a_blk, vb)
            state[h] = state[h] * jnp.exp(e_end) + kv_ref[i * n_heads + h]
    for h in range(n_heads):
        st_ref[h] = state[h]

    zg = z_ref[0, :, 3 * width:4 * width]
    gate = gn_ref[...] * (zg * jax.nn.sigmoid(zg))
    for h in range(n_heads):
        lanes = slice(h * dk, (h + 1) * dk)
        o = oacc_ref[:, lanes]
        o = o * lax.rsqrt(jnp.mean(o * o, axis=-1, keepdims=True) + NORM_EPS)
        o_ref[0, :, lanes] = o * gate[:, lanes]

    for h in range(n_heads):
        sfin_ref[0, h] = st_ref[h].T


def _hgrn2(z, log_lb, one_m_lb, norm_g, s0):
    bsz, t_len, proj = z.shape
    width = proj // 4
    n_heads = width // HG_HEAD_DIM
    tile = min(128, t_len)
    assert t_len % tile == 0 and tile % HG_BLOCK == 0
    kern = functools.partial(_hgrn_kernel, n_heads=n_heads, tile=tile)
    state_spec = pl.BlockSpec((1, n_heads, HG_HEAD_DIM, HG_HEAD_DIM), lambda b, t: (b, 0, 0, 0))
    group = jnp.arange(HG_BLOCK * HG_HEAD_DIM, dtype=jnp.int32)[:, None] // HG_HEAD_DIM
    selector = (group == jnp.arange(LANES, dtype=jnp.int32)[None, :]).astype(BF16)
    return pl.pallas_call(
        kern,
        grid=(bsz, t_len // tile),
        in_specs=[pl.BlockSpec((1, tile, proj), lambda b, t: (b, t, 0)),
                  _const_spec((1, width)), _const_spec((1, width)), _const_spec((1, width)),
                  _const_spec(selector.shape), state_spec],
        out_specs=[pl.BlockSpec((1, tile, width), lambda b, t: (b, t, 0)), state_spec],
        out_shape=[jax.ShapeDtypeStruct((bsz, t_len, width), F32),
                   jax.ShapeDtypeStruct(s0.shape, F32)],
        scratch_shapes=[pltpu.VMEM((n_heads, HG_HEAD_DIM, HG_HEAD_DIM), F32),
                        pltpu.VMEM((tile, width), F32), pltpu.VMEM((tile, width), F32),
                        pltpu.VMEM((tile, width), F32), pltpu.VMEM((tile, width), F32),
                        pltpu.VMEM((tile * n_heads, HG_BLOCK * HG_HEAD_DIM), BF16),
                        pltpu.VMEM((tile * n_heads, LANES), F32),
                        pltpu.VMEM((tile // HG_BLOCK * n_heads, HG_HEAD_DIM, HG_HEAD_DIM), F32)],
        compiler_params=pltpu.CompilerParams(dimension_semantics=("parallel", "arbitrary"),
                                             vmem_limit_bytes=VMEM_LIMIT),
        name="hgrn2",
    )(z, log_lb, one_m_lb, norm_g, selector, s0)


def _map(fn, *lists):
    return [fn(*xs) for xs in zip(*lists)]


def _block_diag(y):
    lo = _iota2(y.shape, 1) < RW_HEAD_DIM
    return jnp.concatenate([jnp.where(lo, y, 0.0), jnp.where(lo, 0.0, y)], axis=0).astype(BF16)


def _pair_dot(x, y):
    return _dot(x.astype(BF16), _block_diag(y))


def _pair_dot_nt(x, y):
    return _dot(x.astype(BF16), _block_diag(y), NT)


def _unit_lower_inverses(n_list, eye, same_sub):
    d = [jnp.where(same_sub, n, 0.0) for n in n_list]
    l_off = _map(lambda n, dd: n - dd, n_list, d)
    t_d = [eye - dd for dd in d]
    d_pow = d
    for _ in range(RW_SUB.bit_length() - 2):
        d_pow = _map(_pair_dot, d_pow, d_pow)
        t_d = _map(lambda t, dp: t + _pair_dot(t, dp), t_d, d_pow)
    p = _map(_pair_dot, t_d, l_off)
    p2 = _map(_pair_dot, p, p)
    t_p = _map(lambda pp, pp2: (eye - pp) + _pair_dot(eye - pp, pp2), p, p2)
    return _map(_pair_dot, t_p, t_d)


def _rwkv_kernel(z_ref, shift_ref, s0_ref, mu_ref, w0_ref, w2_ref, a0_ref, a2_ref, g2_ref,
                 kk_ref, ka_ref, rk_ref, lnw_ref, lnb_ref, o_ref, sfin_ref,
                 s_ref, carry_ref, zs_ref, y_ref, *, n_heads, chunk, tile):
    hd = RW_HEAD_DIM
    width = n_heads * hd
    n_chunks = tile // chunk
    t_idx = pl.program_id(1)

    @pl.when(t_idx == 0)
    def _():
        s_ref[...] = s0_ref[0]
        carry_ref[SUBLANES - 1:SUBLANES, :] = shift_ref[0]

    z = z_ref[0]
    zs_ref[0:SUBLANES, :] = carry_ref[...]
    zs_ref[SUBLANES:SUBLANES + tile, :] = z
    carry_ref[...] = z[tile - SUBLANES:tile, :]
    z_prev = zs_ref[SUBLANES - 1:SUBLANES - 1 + tile, :]
    zm = z + (z_prev - z) * mu_ref[...]

    r = zm[:, 0:width]
    k = zm[:, width:2 * width]
    v = zm[:, 2 * width:3 * width]
    lora_in = zm[:, 3 * width:3 * width + 2 * RW_LORA_W]
    gd = zm[:, 3 * width + 2 * RW_LORA_W:]

    w_pre = w0_ref[...] + _dot3(jnp.tanh(lora_in), w2_ref[...])
    w_raw = -jax.nn.softplus(-w_pre) - 0.5
    logw = -jnp.exp(w_raw)
    a = jax.nn.sigmoid(a0_ref[...] + _dot3(lora_in, a2_ref[...]))
    g = _dot(jax.nn.sigmoid(gd).astype(BF16), g2_ref[...])

    hrow = _iota2((width, width), 0) // hd
    hcol = _iota2((width, width), 1) // hd
    head_ones = (hrow == hcol).astype(BF16)

    kk = k * kk_ref[...]
    kk = kk / jnp.maximum(jnp.sqrt(_dot_xr(kk * kk, head_ones)), 1e-12)
    k2 = k * (1.0 + (a - 1.0) * ka_ref[...])
    beta = kk * a

    trow = _iota2((tile, tile), 0)
    tcol = _iota2((tile, tile), 1)
    tri = ((trow // chunk == tcol // chunk) & (tcol <= trow)).astype(BF16)
    c = _dot_xl(tri, logw)
    e_pos = jnp.exp(c)
    e_neg = jnp.exp(-c)
    k_t = kk * jnp.exp(c - logw)
    r_t = r * e_pos
    b_h = beta * e_neg
    k_h = k2 * e_neg

    pw = 2 * hd
    pairs = range(n_heads // 2)
    lanes = [slice(p * pw, (p + 1) * pw) for p in pairs]
    row = _iota2((chunk, pw), 0)
    lane = _iota2((chunk, pw), 1)
    lo_half = lane < hd
    col = jnp.where(lo_half, lane, lane - hd)
    incl2 = jnp.concatenate([col < row, col <= row], axis=0)
    eye = (row == col).astype(F32)
    same_sub = (row // RW_SUB) == (col // RW_SUB)
    row2 = _iota2((pw, pw), 0)
    col2 = _iota2((pw, pw), 1)
    same_head = (row2 // hd) == (col2 // hd)
    eye2 = (row2 == col2).astype(F32)

    chunk_rows = [slice(ci * chunk, (ci + 1) * chunk) for ci in range(n_chunks)]
    c_last = [c[rows][chunk - 1:chunk, :] for rows in chunk_rows]
    e_end = [jnp.exp(cl - c[rows]) for cl, rows in zip(c_last, chunk_rows)]
    b_l_c = [beta[rows] * e for rows, e in zip(chunk_rows, e_end)]
    k_l_c = [k2[rows] * e for rows, e in zip(chunk_rows, e_end)]
    g_last_c = [jnp.exp(cl) for cl in c_last]
    units = [(ci, p) for ci in range(n_chunks) for p in pairs]
    at = lambda x: [x[chunk_rows[ci], lanes[p]] for ci, p in units]
    kt, rt, vv = at(k_t), at(r_t), at(v)
    b_l = [b_l_c[ci][:, lanes[p]] for ci, p in units]
    k_l = [k_l_c[ci][:, lanes[p]] for ci, p in units]
    g_last = [g_last_c[ci][:, lanes[p]] for ci, p in units]
    lhs = _map(lambda x, y: jnp.concatenate([x, y], axis=0), kt, rt)
    g_b = [jnp.where(incl2, _pair_dot_nt(l, x), 0.0) for l, x in zip(lhs, at(b_h))]
    g_k = [jnp.where(incl2, _pair_dot_nt(l, x), 0.0) for l, x in zip(lhs, at(k_h))]
    a_ab = [x[0:chunk] for x in g_b]
    a_rb = [x[chunk:] for x in g_b]
    t_inv = _unit_lower_inverses(a_ab, eye, same_sub)
    gkv = _map(_pair_dot, g_k, vv)
    x_loc = [x[0:chunk] for x in gkv]
    w_t = _map(_pair_dot, t_inv, kt)
    u_loc = _map(_pair_dot, t_inv, x_loc)
    q_c = _map(lambda r_, a_, w_: r_ - _pair_dot(a_, w_), rt, a_rb, w_t)
    y_loc = _map(lambda x, arb, u_: x[chunk:] - _pair_dot(arb, u_), gkv, a_rb, u_loc)
    m_bd = _map(lambda w_, bl_, gl_: jnp.where(same_head, eye2 * gl_ - _dot1(w_, bl_, TN), 0.0),
                w_t, b_l, g_last)
    c_full = _map(lambda v_, u_, kl_, bl_: _dot1(jnp.concatenate([v_, -u_], axis=0),
                                                 jnp.concatenate([kl_, bl_], axis=0), TN),
                  vv, u_loc, k_l, b_l)
    c_add = [jnp.where(lo_half, x[0:hd], x[hd:]) for x in c_full]
    state = [s_ref[p] for p in pairs]
    for idx, (ci, p) in enumerate(units):
        y_ref[chunk_rows[ci], lanes[p]] = _pair_dot_nt(q_c[idx], state[p]) + y_loc[idx]
        state[p] = _dot3(state[p], m_bd[idx]) + c_add[idx]
    for p in pairs:
        s_ref[p] = state[p]

    y = y_ref[...]
    mean = _dot_xr(y, head_ones) * (1.0 / hd)
    yc = y - mean
    var = _dot_xr(yc * yc, head_ones) * (1.0 / hd)
    yn = yc * lax.rsqrt(var + RW_GN_EPS) * lnw_ref[...] + lnb_ref[...]
    bonus = _dot_xr(r * k2 * rk_ref[...], head_ones) * v
    o_ref[0] = (yn + bonus) * g
    sfin_ref[0] = s_ref[...]


def _rwkv7(z, shift_prev, s0, mu, w0, w2p, a0, a2p, g2, k_k, k_a, r_k, ln_w, ln_b):
    bsz, t_len, proj = z.shape
    n_heads = s0.shape[1]
    width = n_heads * RW_HEAD_DIM
    chunk = RW_CHUNK
    tile = min(RW_TILE, t_len)
    assert t_len % tile == 0 and tile % chunk == 0
    kern = functools.partial(_rwkv_kernel, n_heads=n_heads, chunk=chunk, tile=tile)
    n_pairs, hd = n_heads // 2, RW_HEAD_DIM
    assert n_heads % 2 == 0
    s0 = s0.reshape(bsz, n_pairs, 2, hd, hd).transpose(0, 1, 3, 2, 4).reshape(bsz, n_pairs, hd, 2 * hd)
    state_spec = pl.BlockSpec((1, n_pairs, hd, 2 * hd), lambda b, t: (b, 0, 0, 0))
    vec = lambda a: _const_spec(a.shape)
    o_rw, s_fin = pl.pallas_call(
        kern,
        grid=(bsz, t_len // tile),
        in_specs=[pl.BlockSpec((1, tile, proj), lambda b, t: (b, t, 0)),
                  pl.BlockSpec((1, 1, proj), lambda b, t: (b, 0, 0)),
                  state_spec,
                  vec(mu), vec(w0), vec(w2p), vec(a0), vec(a2p), vec(g2),
                  vec(k_k), vec(k_a), vec(r_k), vec(ln_w), vec(ln_b)],
        out_specs=[pl.BlockSpec((1, tile, width), lambda b, t: (b, t, 0)), state_spec],
        out_shape=[jax.ShapeDtypeStruct((bsz, t_len, width), F32),
                   jax.ShapeDtypeStruct(s0.shape, F32)],
        scratch_shapes=[pltpu.VMEM((n_pairs, hd, 2 * hd), F32),
                        pltpu.VMEM((SUBLANES, proj), F32),
                        pltpu.VMEM((SUBLANES + tile, proj), F32),
                        pltpu.VMEM((tile, width), F32)],
        compiler_params=pltpu.CompilerParams(dimension_semantics=("parallel", "arbitrary"),
                                             vmem_limit_bytes=VMEM_LIMIT),
        name="rwkv7",
    )(z, shift_prev, s0, mu, w0, w2p, a0, a2p, g2, k_k, k_a, r_k, ln_w, ln_b)
    s_fin = s_fin.reshape(bsz, n_pairs, hd, 2, hd).transpose(0, 1, 3, 2, 4).reshape(bsz, n_heads, hd, hd)
    return o_rw, s_fin


def _gelu_exact(x):
    return 0.5 * x * (1.0 + lax.erf(x * (2.0 ** -0.5)))


def _post_kernel(h_ref, ohg_ref, orw_ref, p_ref, cvprev_ref,
                 wohg_ref, worw_ref, nffn_ref, wg_ref, wv_ref, cw_ref, cb_ref, wo_ref,
                 nple_ref, pgate_ref, pproj_ref, fnorm_ref,
                 hout_ref, cvnew_ref, carry_ref, ugs_ref, *, tm, n_col_blocks, apply_final_norm):
    t_idx = pl.program_id(1)
    cols = FF_COLS

    @pl.when(t_idx == 0)
    def _():
        for j in range(n_col_blocks):
            carry_ref[j, SUBLANES - (CONV_W - 1):SUBLANES, :] = cvprev_ref[0, :, j * cols:(j + 1) * cols]

    h1 = (h_ref[0] + _dot(ohg_ref[0].astype(BF16), wohg_ref[...])
          + _dot(orw_ref[0].astype(BF16), worw_ref[...]))
    xn = _rmsnorm(h1, nffn_ref[...]).astype(BF16)
    acc = jnp.zeros_like(h1)
    for j in range(n_col_blocks):
        ug = _dot(xn, wg_ref[j])
        uv = _dot(xn, wv_ref[j])
        ugs_ref[0:SUBLANES, :] = carry_ref[j]
        ugs_ref[SUBLANES:SUBLANES + tm, :] = ug
        carry_ref[j] = ug[tm - SUBLANES:tm, :]
        cvnew_ref[0, :, j * cols:(j + 1) * cols] = ug[tm - (CONV_W - 1):tm, :]
        cw = cw_ref[j]
        c = (cb_ref[j] + cw[0:1, :] * ugs_ref[SUBLANES - 2:SUBLANES - 2 + tm, :]
             + cw[1:2, :] * ugs_ref[SUBLANES - 1:SUBLANES - 1 + tm, :] + cw[2:3, :] * ug)
        hid = (_gelu_exact(c) * uv).astype(BF16)
        acc = acc + _dot(hid, wo_ref[j])
    h2 = h1 + acc
    gate = jax.nn.sigmoid(_dot(_rmsnorm(h2, nple_ref[...]).astype(BF16), pgate_ref[...]))
    h3 = h2 + gate * _dot(p_ref[0].astype(BF16), pproj_ref[...])
    if apply_final_norm:
        h3 = _rmsnorm(h3, fnorm_ref[...])
    hout_ref[0] = h3


def _post(h, o_hg, o_rw, p, cv_prev, wo_hg, wo_rw, norm_ffn, wg, wv, cw, cb, wo, norm_ple,
          ple_gate, ple_proj, final_norm, apply_final_norm):
    bsz, t_len, d = h.shape
    tm = min(512, t_len)
    assert t_len % tm == 0 and tm >= SUBLANES
    n_col_blocks = wg.shape[0]
    d_ff = n_col_blocks * FF_COLS
    kern = functools.partial(_post_kernel, tm=tm, n_col_blocks=n_col_blocks,
                             apply_final_norm=apply_final_norm)
    tok = lambda w: pl.BlockSpec((1, tm, w), lambda b, t: (b, t, 0))
    cv_spec = pl.BlockSpec((1, CONV_W - 1, d_ff), lambda b, t: (b, 0, 0))
    consts = [wo_hg, wo_rw, norm_ffn, wg, wv, cw, cb, wo, norm_ple, ple_gate, ple_proj, final_norm]
    return pl.pallas_call(
        kern,
        grid=(bsz, t_len // tm),
        in_specs=[tok(d), tok(o_hg.shape[2]), tok(o_rw.shape[2]), tok(p.shape[2]), cv_spec]
                 + [_const_spec(a.shape) for a in consts],
        out_specs=[tok(d), cv_spec],
        out_shape=[jax.ShapeDtypeStruct(h.shape, F32), jax.ShapeDtypeStruct(cv_prev.shape, F32)],
        scratch_shapes=[pltpu.VMEM((n_col_blocks, SUBLANES, FF_COLS), F32),
                        pltpu.VMEM((SUBLANES + tm, FF_COLS), F32)],
        compiler_params=pltpu.CompilerParams(dimension_semantics=("parallel", "arbitrary"),
                                             vmem_limit_bytes=VMEM_LIMIT),
        name="post",
    )(h, o_hg, o_rw, p, cv_prev, *consts)


def _layer_params(i, hg_proj, lb_all, w):
    d_ff = w["ffn_conv_b"].shape[1]
    n_cb = d_ff // FF_COLS
    rw_width = w["rw_w0"].shape[1]
    hg_width = w["hg_norm"].shape[1]
    row = lambda a: a.reshape(1, -1)
    lb = jnp.maximum(lb_all[i], 0.0)
    zeros_lora = jnp.zeros((RW_LORA_W, rw_width), F32)
    ffn_in = w["ffn_in"][i].astype(BF16)
    d_model = ffn_in.shape[0]
    blocked = lambda a: a.reshape(d_model, n_cb, FF_COLS).transpose(1, 0, 2)
    return dict(
        norm_mix=row(w["norm_mix"][i]),
        w_in_hg=w["w_in"][i][:, :hg_proj].astype(BF16),
        w_in_rw=w["w_in"][i][:, hg_proj:].astype(BF16),
        log_lb=row(jnp.log(lb)), one_m_lb=row(1.0 - lb), hg_norm=row(w["hg_norm"][i]),
        mu=row(w["rw_mu"][i]), w0=row(w["rw_w0"][i]),
        w2p=jnp.concatenate([w["rw_w2"][i], zeros_lora], axis=0),
        a0=row(w["rw_a0"][i]),
        a2p=jnp.concatenate([zeros_lora, w["rw_a2"][i]], axis=0),
        g2=w["rw_g2"][i].astype(BF16),
        k_k=row(w["rw_k_k"][i]), k_a=row(w["rw_k_a"][i]), r_k=row(w["rw_r_k"][i]),
        ln_w=row(w["rw_ln_w"][i]), ln_b=row(w["rw_ln_b"][i]),
        wo_hg=w["w_out"][i][:hg_width].astype(BF16), wo_rw=w["w_out"][i][hg_width:].astype(BF16),
        norm_ffn=row(w["norm_ffn"][i]),
        wg=blocked(ffn_in[:, :d_ff]), wv=blocked(ffn_in[:, d_ff:]),
        cw=w["ffn_conv_w"][i].reshape(CONV_W, n_cb, FF_COLS).transpose(1, 0, 2),
        cb=w["ffn_conv_b"][i].reshape(n_cb, 1, FF_COLS),
        wo=w["ffn_out"][i].astype(BF16).reshape(n_cb, FF_COLS, d_model),
        norm_ple=row(w["norm_ple"][i]),
        ple_gate=w["ple_gate"][i].astype(BF16), ple_proj=w["ple_proj"][i].astype(BF16),
        final_norm=row(w["final_norm"]),
    )


def _trunk(x, p, s_hg, s_rw, s_sh, s_cv, layers):
    bsz, t_len, d = x.shape
    depth = len(layers)
    h = x
    new_hg, new_rw, new_sh, new_cv = [], [], [], []
    for i, lp in enumerate(layers):
        z_hg, z_rw = _in_proj(h.reshape(bsz * t_len, d), lp["norm_mix"], lp["w_in_hg"], lp["w_in_rw"])
        z_hg = z_hg.reshape(bsz, t_len, -1)
        z_rw = z_rw.reshape(bsz, t_len, -1)
        o_hg, n_hg = _hgrn2(z_hg, lp["log_lb"], lp["one_m_lb"], lp["hg_norm"], s_hg[i])
        o_rw, n_rw = _rwkv7(z_rw, s_sh[i][:, None, :], s_rw[i], lp["mu"], lp["w0"], lp["w2p"],
                            lp["a0"], lp["a2p"], lp["g2"], lp["k_k"], lp["k_a"], lp["r_k"],
                            lp["ln_w"], lp["ln_b"])
        h, n_cv = _post(h, o_hg, o_rw, p[i], s_cv[i], lp["wo_hg"], lp["wo_rw"], lp["norm_ffn"],
                        lp["wg"], lp["wv"], lp["cw"], lp["cb"], lp["wo"], lp["norm_ple"],
                        lp["ple_gate"], lp["ple_proj"], lp["final_norm"], i == depth - 1)
        new_hg.append(n_hg)
        new_rw.append(n_rw)
        new_sh.append(z_rw[:, -1])
        new_cv.append(n_cv)
    return h, jnp.stack(new_hg), jnp.stack(new_rw), jnp.stack(new_sh), jnp.stack(new_cv)


def kernel(x_prompt, x_sample, state_hgrn, state_rwkv, state_shift, state_conv, p_prompt, p_sample,
           norm_mix, w_in, lb_raw, hg_norm, rw_mu, rw_w0, rw_w2, rw_a0, rw_a2, rw_g2, rw_k_k, rw_k_a,
           rw_r_k, rw_ln_w, rw_ln_b, w_out, norm_ffn, ffn_in, ffn_conv_w, ffn_conv_b, ffn_out,
           norm_ple, ple_gate, ple_proj, final_norm):
    w = dict(norm_mix=norm_mix, w_in=w_in, hg_norm=hg_norm, rw_mu=rw_mu, rw_w0=rw_w0, rw_w2=rw_w2,
             rw_a0=rw_a0, rw_a2=rw_a2, rw_g2=rw_g2, rw_k_k=rw_k_k, rw_k_a=rw_k_a,
             rw_r_k=rw_r_k.reshape(rw_r_k.shape[0], -1), rw_ln_w=rw_ln_w, rw_ln_b=rw_ln_b, w_out=w_out,
             norm_ffn=norm_ffn, ffn_in=ffn_in, ffn_conv_w=ffn_conv_w, ffn_conv_b=ffn_conv_b,
             ffn_out=ffn_out, norm_ple=norm_ple, ple_gate=ple_gate, ple_proj=ple_proj,
             final_norm=final_norm)
    depth = w_in.shape[0]
    hg_proj = 4 * hg_norm.shape[1]
    lb_all = jnp.cumsum(jax.nn.softmax(lb_raw.astype(F32), axis=0), axis=0)
    lb_all = lb_all - lb_all[:1]
    layers = [_layer_params(i, hg_proj, lb_all, w) for i in range(depth)]

    nb = x_prompt.shape[0]
    zeros_like_state = lambda s: jnp.zeros((depth, nb) + s.shape[2:], s.dtype)
    y_p, hg_p, rw_p, sh_p, cv_p = _trunk(x_prompt, p_prompt, zeros_like_state(state_hgrn),
                                         zeros_like_state(state_rwkv), zeros_like_state(state_shift),
                                         zeros_like_state(state_conv), layers)
    y_s, hg_s, rw_s, sh_s, cv_s = _trunk(x_sample, p_sample, state_hgrn, state_rwkv, state_shift,
                                         state_conv, layers)
    return (y_p, y_s, hg_p, rw_p, sh_p, cv_p, hg_s, rw_s, sh_s, cv_s)
```

```python
import functools

import jax
import jax.numpy as jnp
from jax import lax
from jax.experimental import pallas as pl
from jax.experimental.pallas import tpu as pltpu

F32 = jnp.float32
BF16 = jnp.bfloat16

NORM_EPS = 1e-6
RW_GN_EPS = 64e-5
HG_HEAD_DIM = 128
RW_HEAD_DIM = 64
RW_LORA_W = 64
CONV_W = 3
HG_BLOCK = 16
RW_CHUNK = 64
RW_SUB = 16
RW_TILE = 128
FF_COLS = 256
LANES = 128
SUBLANES = 8
VMEM_LIMIT = 56 * 1024 * 1024

NN = ((1,), (0,))
NT = ((1,), (1,))
TN = ((0,), (0,))


def _dot(a, b, dims=NN):
    return lax.dot_general(a, b, (dims, ((), ())), preferred_element_type=F32)


def _dot1(a, b, dims=NN):
    return _dot(a.astype(BF16), b.astype(BF16), dims)


def _split2(x):
    hi = x.astype(BF16)
    lo = (x - hi.astype(F32)).astype(BF16)
    return hi, lo


def _split3(x):
    hi = x.astype(BF16)
    r1 = x - hi.astype(F32)
    mid = r1.astype(BF16)
    lo = (r1 - mid.astype(F32)).astype(BF16)
    return hi, mid, lo


def _dot3(a, b, dims=NN):
    ah, al = _split2(a)
    bh, bl = _split2(b)
    return _dot(ah, bh, dims) + (_dot(ah, bl, dims) + _dot(al, bh, dims))


def _dot_xr(x, b_exact, dims=NN):
    x0, x1, x2 = _split3(x)
    return _dot(x0, b_exact, dims) + (_dot(x1, b_exact, dims) + _dot(x2, b_exact, dims))


def _dot_xl(a_exact, x, dims=NN):
    x0, x1, x2 = _split3(x)
    return _dot(a_exact, x0, dims) + (_dot(a_exact, x1, dims) + _dot(a_exact, x2, dims))


def _dot_xr2(x, b_exact, dims=NN):
    hi, lo = _split2(x)
    return _dot(hi, b_exact, dims) + _dot(lo, b_exact, dims)


def _sigmoid(x):
    return 0.5 * jnp.tanh(0.5 * x) + 0.5


def _softplus(x):
    return jnp.maximum(x, 0.0) + jnp.log(1.0 + jnp.exp(-jnp.abs(x)))


def _rmsnorm(x, g):
    return x * lax.rsqrt(jnp.mean(x * x, axis=-1, keepdims=True) + NORM_EPS) * g


def _iota2(shape, axis):
    return lax.broadcasted_iota(jnp.int32, shape, axis)


def _const_spec(shape):
    zeros = (0,) * len(shape)
    return pl.BlockSpec(shape, lambda *_: zeros, pipeline_mode=pl.Buffered(1))


def _inproj_kernel(x_ref, g_ref, whg_ref, wrw_ref, zhg_ref, zrw_ref):
    xn = _rmsnorm(x_ref[...], g_ref[...]).astype(BF16)
    zhg_ref[...] = _dot(xn, whg_ref[...])
    zrw_ref[...] = _dot(xn, wrw_ref[...])


def _in_proj(x2, g, w_hg, w_rw):
    n, d = x2.shape
    tm = min(512, n)
    assert n % tm == 0
    return pl.pallas_call(
        _inproj_kernel,
        grid=(n // tm,),
        in_specs=[pl.BlockSpec((tm, d), lambda i: (i, 0)),
                  _const_spec((1, d)),
                  _const_spec(w_hg.shape),
                  _const_spec(w_rw.shape)],
        out_specs=[pl.BlockSpec((tm, w_hg.shape[1]), lambda i: (i, 0)),
                   pl.BlockSpec((tm, w_rw.shape[1]), lambda i: (i, 0))],
        out_shape=[jax.ShapeDtypeStruct((n, w_hg.shape[1]), F32),
                   jax.ShapeDtypeStruct((n, w_rw.shape[1]), F32)],
        compiler_params=pltpu.CompilerParams(dimension_semantics=("parallel",),
                                             vmem_limit_bytes=VMEM_LIMIT),
        name="in_proj",
    )(x2, g, w_hg, w_rw)


def _hgrn_kernel(z_ref, loglb_ref, omlb_ref, gn_ref, sel_ref, s0_ref, o_ref, sfin_ref,
                 st_ref, q_ref, k_ref, bl_ref, p_ref, a_ref, *, n_heads, tile):
    dk = HG_HEAD_DIM
    width = n_heads * dk
    blk = HG_BLOCK
    n_blocks = tile // blk
    t_idx = pl.program_id(1)

    @pl.when(t_idx == 0)
    def _():
        for h in range(n_heads):
            st_ref[h] = s0_ref[0, h].T

    zq = z_ref[0, :, 0:width]
    zf = z_ref[0, :, width:2 * width]
    q_ref[...] = zq * _sigmoid(zq)
    soft = jnp.log(1.0 + jnp.exp(-jnp.abs(zf)))
    log_sig_pos = jnp.minimum(zf, 0.0) - soft
    log_sig_neg = jnp.minimum(-zf, 0.0) - soft
    lo_term = loglb_ref[...] + log_sig_neg
    top = jnp.maximum(log_sig_pos, lo_term)
    logf = top + jnp.log(1.0 + jnp.exp(-jnp.abs(log_sig_pos - lo_term)))
    k_ref[...] = omlb_ref[...] * jnp.exp(log_sig_neg)
    row = _iota2((tile, tile), 0)
    col = _iota2((tile, tile), 1)
    tri = ((row // blk == col // blk) & (col <= row)).astype(BF16)
    bl_ref[...] = _dot_xl(tri, logf)

    t_row = _iota2((blk, dk), 0)

    def build_step(i, carry):
        r0 = pl.multiple_of(i * blk, blk)
        for h in range(n_heads):
            lanes = slice(h * dk, (h + 1) * dk)
            qb = q_ref[pl.ds(r0, blk), lanes]
            kb = k_ref[pl.ds(r0, blk), lanes]
            bb = bl_ref[pl.ds(r0, blk), lanes]
            p0 = pl.multiple_of((i * n_heads + h) * blk, blk)
            for s in range(blk):
                causal = t_row >= s
                ps = jnp.where(causal, qb * kb[s:s + 1, :] * jnp.exp(bb - bb[s:s + 1, :]), 0.0)
                p_ref[pl.ds(p0, blk), s * dk:(s + 1) * dk] = ps.astype(BF16)
        return carry

    lax.fori_loop(0, n_blocks, build_step, 0)
    a_ref[...] = _dot(p_ref[...], sel_ref[...])

    block_rows = [slice(i * blk, (i + 1) * blk) for i in range(n_blocks)]
    e_end = [bl_ref[(i + 1) * blk - 1:(i + 1) * blk, :] for i in range(n_blocks)]
    start = [jnp.zeros_like(e_end[0])]
    for i in range(n_blocks):
        start.append(start[i] + e_end[i])
    qe = [q_ref[r, :] * jnp.exp(bl_ref[r, :]) for r in block_rows]
    ke = [k_ref[r, :] * jnp.exp(e - bl_ref[r, :]) for r, e in zip(block_rows, e_end)]
    vv = [z_ref[0, r, 2 * width:3 * width] for r in block_rows]
    q_tile = jnp.concatenate([x * jnp.exp(s) for x, s in zip(qe, start)], axis=0)
    k_tile = jnp.concatenate([x * jnp.exp(start[n_blocks] - s) for x, s in zip(ke, start[1:])], axis=0)
    v_tile = jnp.concatenate(vv, axis=0)
    decay_tile = jnp.exp(start[n_blocks])

    levels = []
    span = n_blocks // 2
    while span >= 1:
        groups = []
        for lo in range(0, n_blocks, 2 * span):
            mid, hi = lo + span, lo + 2 * span
            q_r = jnp.concatenate([qe[i] if i == mid else qe[i] * jnp.exp(start[i] - start[mid])
                                   for i in range(mid, hi)], axis=0)
            k_l = jnp.concatenate([ke[j] if j == mid - 1 else ke[j] * jnp.exp(start[mid] - start[j + 1])
                                   for j in range(lo, mid)], axis=0)
            v_l = jnp.concatenate(vv[lo:mid], axis=0)
            groups.append((mid, hi, q_r, k_l, v_l))
        levels.append(groups)
        span //= 2

    zg = z_ref[0, :, 3 * width:4 * width]
    gate = gn_ref[...] * (zg * _sigmoid(zg))
    heads = range(n_heads)
    lanes = [slice(h * dk, (h + 1) * dk) for h in heads]
    st = [st_ref[h] for h in heads]
    new_st = [st[h] * decay_tile[:, lanes[h]] + _dot1(v_tile[:, lanes[h]], k_tile[:, lanes[h]], TN)
              for h in heads]
    o_state = [_dot1(q_tile[:, lanes[h]], st[h], NT) for h in heads]
    o_blocks = [[o_state[h][r] for r in block_rows] for h in heads]
    for groups in levels:
        for mid, hi, q_r, k_l, v_l in groups:
            scores = [_dot1(q_r[:, lanes[h]], k_l[:, lanes[h]], NT) for h in heads]
            upd = [_dot1(scores[h], v_l[:, lanes[h]]) for h in heads]
            for h in heads:
                for n, i in enumerate(range(mid, hi)):
                    o_blocks[h][i] = o_blocks[h][i] + upd[h][n * blk:(n + 1) * blk]
    for i in range(n_blocks):
        for h in heads:
            p0 = (i * n_heads + h) * blk
            o_blocks[h][i] = o_blocks[h][i] + _dot1(a_ref[p0:p0 + blk, 0:blk], vv[i][:, lanes[h]])
    for h in heads:
        o = jnp.concatenate(o_blocks[h], axis=0)
        o = o * lax.rsqrt(jnp.mean(o * o, axis=-1, keepdims=True) + NORM_EPS)
        o_ref[0, :, lanes[h]] = o * gate[:, lanes[h]]
    for h in heads:
        st_ref[h] = new_st[h]

    @pl.when(t_idx == pl.num_programs(1) - 1)
    def _():
        for h in range(n_heads):
            sfin_ref[0, h] = st_ref[h].T


def _hgrn2(z, log_lb, one_m_lb, norm_g, s0):
    bsz, t_len, proj = z.shape
    width = proj // 4
    n_heads = width // HG_HEAD_DIM
    tile = min(128, t_len)
    assert t_len % tile == 0 and tile % HG_BLOCK == 0
    kern = functools.partial(_hgrn_kernel, n_heads=n_heads, tile=tile)
    state_spec = pl.BlockSpec((1, n_heads, HG_HEAD_DIM, HG_HEAD_DIM), lambda b, t: (b, 0, 0, 0))
    group = jnp.arange(HG_BLOCK * HG_HEAD_DIM, dtype=jnp.int32)[:, None] // HG_HEAD_DIM
    selector = (group == jnp.arange(LANES, dtype=jnp.int32)[None, :]).astype(BF16)
    return pl.pallas_call(
        kern,
        grid=(bsz, t_len // tile),
        in_specs=[pl.BlockSpec((1, tile, proj), lambda b, t: (b, t, 0)),
                  _const_spec((1, width)), _const_spec((1, width)), _const_spec((1, width)),
                  _const_spec(selector.shape), state_spec],
        out_specs=[pl.BlockSpec((1, tile, width), lambda b, t: (b, t, 0)), state_spec],
        out_shape=[jax.ShapeDtypeStruct((bsz, t_len, width), F32),
                   jax.ShapeDtypeStruct(s0.shape, F32)],
        scratch_shapes=[pltpu.VMEM((n_heads, HG_HEAD_DIM, HG_HEAD_DIM), F32),
                        pltpu.VMEM((tile, width), F32), pltpu.VMEM((tile, width), F32),
                        pltpu.VMEM((tile, width), F32),
                        pltpu.VMEM((tile * n_heads, HG_BLOCK * HG_HEAD_DIM), BF16),
                        pltpu.VMEM((tile * n_heads, LANES), F32)],
        compiler_params=pltpu.CompilerParams(dimension_semantics=("parallel", "arbitrary"),
                                             vmem_limit_bytes=VMEM_LIMIT),
        name="hgrn2",
    )(z, log_lb, one_m_lb, norm_g, selector, s0)


def _map(fn, *lists):
    return [fn(*xs) for xs in zip(*lists)]


def _block_diag(y):
    lo = _iota2(y.shape, 1) < RW_HEAD_DIM
    y = y.astype(BF16)
    zero = jnp.zeros_like(y)
    return jnp.concatenate([jnp.where(lo, y, zero), jnp.where(lo, zero, y)], axis=0)


def _pair_dot(x, y):
    return _dot(x.astype(BF16), _block_diag(y))


def _pair_dot_nt(x, y):
    return _dot(x.astype(BF16), _block_diag(y), NT)


def _pair_dot2(x, y, z):
    return _dot(x.astype(BF16), jnp.concatenate([_block_diag(y), _block_diag(z)], axis=1))


def _pair_dot2_nt(x, y, z):
    return _dot(x.astype(BF16), jnp.concatenate([_block_diag(y), _block_diag(z)], axis=0), NT)


def _unit_lower_inverses(n_list, eye, same_sub):
    d = [jnp.where(same_sub, n, 0.0) for n in n_list]
    l_off = _map(lambda n, dd: n - dd, n_list, d)
    t_d = [eye - dd for dd in d]
    d_pow = d
    for _ in range(RW_SUB.bit_length() - 2):
        d_pow = _map(_pair_dot, d_pow, d_pow)
        t_d = _map(lambda t, dp: t + _pair_dot(t, dp), t_d, d_pow)
    p = _map(_pair_dot, t_d, l_off)
    p2 = _map(_pair_dot, p, p)
    t_p = _map(lambda pp, pp2: (eye - pp) + _pair_dot(eye - pp, pp2), p, p2)
    return _map(_pair_dot, t_p, t_d)


def _rwkv_kernel(z_ref, shift_ref, s0_ref, mu_ref, w0_ref, w2_ref, a0_ref, a2_ref, g2_ref,
                 kk_ref, ka_ref, rk_ref, lnw_ref, lnb_ref, o_ref, sfin_ref,
                 s_ref, carry_ref, zs_ref, y_ref, *, n_heads, chunk, tile):
    hd = RW_HEAD_DIM
    width = n_heads * hd
    n_chunks = tile // chunk
    t_idx = pl.program_id(1)

    @pl.when(t_idx == 0)
    def _():
        s_ref[...] = s0_ref[0]
        carry_ref[SUBLANES - 1:SUBLANES, :] = shift_ref[0]

    z = z_ref[0]
    zs_ref[0:SUBLANES, :] = carry_ref[...]
    zs_ref[SUBLANES:SUBLANES + tile, :] = z
    carry_ref[...] = z[tile - SUBLANES:tile, :]
    z_prev = zs_ref[SUBLANES - 1:SUBLANES - 1 + tile, :]
    zm = z + (z_prev - z) * mu_ref[...]

    r = zm[:, 0:width]
    k = zm[:, width:2 * width]
    v = zm[:, 2 * width:3 * width]
    lora_in = zm[:, 3 * width:3 * width + 2 * RW_LORA_W]
    gd = zm[:, 3 * width + 2 * RW_LORA_W:]

    w_pre = w0_ref[...] + _dot3(jnp.tanh(lora_in), w2_ref[...])
    w_raw = -_softplus(-w_pre) - 0.5
    logw = -jnp.exp(w_raw)
    a = _sigmoid(a0_ref[...] + _dot3(lora_in, a2_ref[...]))
    g = _dot(_sigmoid(gd).astype(BF16), g2_ref[...])

    hrow = _iota2((width, width), 0) // hd
    hcol = _iota2((width, width), 1) // hd
    head_ones = (hrow == hcol).astype(BF16)

    kk = k * kk_ref[...]
    kk = kk * jnp.minimum(lax.rsqrt(_dot_xr2(kk * kk, head_ones)), 1e12)
    k2 = k * (1.0 + (a - 1.0) * ka_ref[...])
    beta = kk * a

    trow = _iota2((tile, tile), 0)
    tcol = _iota2((tile, tile), 1)
    tri = ((trow // chunk == tcol // chunk) & (tcol <= trow)).astype(BF16)
    c = _dot_xl(tri, logw)
    e_pos = jnp.exp(c)
    e_neg = jnp.exp(-c)
    k_t = kk * jnp.exp(c - logw)
    r_t = r * e_pos
    b_h = beta * e_neg
    k_h = k2 * e_neg

    pw = 2 * hd
    pairs = range(n_heads // 2)
    lanes = [slice(p * pw, (p + 1) * pw) for p in pairs]
    row = _iota2((chunk, pw), 0)
    lane = _iota2((chunk, pw), 1)
    lo_half = lane < hd
    col = jnp.where(lo_half, lane, lane - hd)
    incl2 = jnp.concatenate([col < row, col <= row], axis=0)
    incl4 = jnp.concatenate([incl2, incl2], axis=1)
    eye = (row == col).astype(F32)
    same_sub = (row // RW_SUB) == (col // RW_SUB)
    same_head = (_iota2((pw, pw), 0) // hd) == (_iota2((pw, pw), 1) // hd)

    chunk_rows = [slice(ci * chunk, (ci + 1) * chunk) for ci in range(n_chunks)]
    c_last = [c[rows][chunk - 1:chunk, :] for rows in chunk_rows]
    e_end = [jnp.exp(cl - c[rows]) for cl, rows in zip(c_last, chunk_rows)]
    b_l_c = [beta[rows] * e for rows, e in zip(chunk_rows, e_end)]
    k_l_c = [k2[rows] * e for rows, e in zip(chunk_rows, e_end)]
    g_last_c = [jnp.exp(cl) for cl in c_last]
    units = [(ci, p) for ci in range(n_chunks) for p in pairs]
    at = lambda x: [x[chunk_rows[ci], lanes[p]] for ci, p in units]
    kt, rt, vv = at(k_t), at(r_t), at(v)
    b_l = [b_l_c[ci][:, lanes[p]] for ci, p in units]
    k_l = [k_l_c[ci][:, lanes[p]] for ci, p in units]
    g_last = [g_last_c[ci][:, lanes[p]] for ci, p in units]
    lhs = _map(lambda x, y: jnp.concatenate([x, y], axis=0), kt, rt)
    g_bk = [jnp.where(incl4, _pair_dot2_nt(l, xb, xk), 0.0)
            for l, xb, xk in zip(lhs, at(b_h), at(k_h))]
    a_ab = [x[0:chunk, 0:pw] for x in g_bk]
    a_rb = [x[chunk:, 0:pw] for x in g_bk]
    t_inv = _unit_lower_inverses(a_ab, eye, same_sub)
    gkv = [_pair_dot(x[:, pw:], v_) for x, v_ in zip(g_bk, vv)]
    x_loc = [x[0:chunk] for x in gkv]
    wu = _map(_pair_dot2, t_inv, kt, x_loc)
    w_t = [x[:, 0:pw] for x in wu]
    u_loc = [x[:, pw:] for x in wu]
    a_wu = _map(_pair_dot2, a_rb, w_t, u_loc)
    q_c = _map(lambda r_, x: r_ - x[:, 0:pw], rt, a_wu)
    y_loc = _map(lambda x, y: x[chunk:] - y[:, pw:], gkv, a_wu)
    m_off = _map(lambda w_, bl_: jnp.where(same_head, -_dot1(w_, bl_, TN), 0.0), w_t, b_l)
    c_full = _map(lambda v_, u_, kl_, bl_: _dot1(jnp.concatenate([v_, -u_], axis=0),
                                                 jnp.concatenate([kl_, bl_], axis=0), TN),
                  vv, u_loc, k_l, b_l)
    c_add = [jnp.where(lo_half, x[0:hd], x[hd:]) for x in c_full]
    state = [s_ref[p] for p in pairs]
    for idx, (ci, p) in enumerate(units):
        y_ref[chunk_rows[ci], lanes[p]] = _pair_dot_nt(q_c[idx], state[p]) + y_loc[idx]
        state[p] = state[p] * g_last[idx] + _dot1(state[p], m_off[idx]) + c_add[idx]
    for p in pairs:
        s_ref[p] = state[p]

    y = y_ref[...]
    mean = _dot_xr2(y, head_ones) * (1.0 / hd)
    yc = y - mean
    var = _dot_xr2(yc * yc, head_ones) * (1.0 / hd)
    yn = yc * lax.rsqrt(var + RW_GN_EPS) * lnw_ref[...] + lnb_ref[...]
    bonus = _dot_xr2(r * k2 * rk_ref[...], head_ones) * v
    o_ref[0] = (yn + bonus) * g
    sfin_ref[0] = s_ref[...]


def _rwkv7(z, shift_prev, s0, mu, w0, w2p, a0, a2p, g2, k_k, k_a, r_k, ln_w, ln_b):
    bsz, t_len, proj = z.shape
    n_heads = s0.shape[1]
    width = n_heads * RW_HEAD_DIM
    chunk = RW_CHUNK
    tile = min(RW_TILE, t_len)
    assert t_len % tile == 0 and tile % chunk == 0
    kern = functools.partial(_rwkv_kernel, n_heads=n_heads, chunk=chunk, tile=tile)
    n_pairs, hd = n_heads // 2, RW_HEAD_DIM
    assert n_heads % 2 == 0
    s0 = s0.reshape(bsz, n_pairs, 2, hd, hd).transpose(0, 1, 3, 2, 4).reshape(bsz, n_pairs, hd, 2 * hd)
    state_spec = pl.BlockSpec((1, n_pairs, hd, 2 * hd), lambda b, t: (b, 0, 0, 0))
    vec = lambda a: _const_spec(a.shape)
    o_rw, s_fin = pl.pallas_call(
        kern,
        grid=(bsz, t_len // tile),
        in_specs=[pl.BlockSpec((1, tile, proj), lambda b, t: (b, t, 0)),
                  pl.BlockSpec((1, 1, proj), lambda b, t: (b, 0, 0)),
                  state_spec,
                  vec(mu), vec(w0), vec(w2p), vec(a0), vec(a2p), vec(g2),
                  vec(k_k), vec(k_a), vec(r_k), vec(ln_w), vec(ln_b)],
        out_specs=[pl.BlockSpec((1, tile, width), lambda b, t: (b, t, 0)), state_spec],
        out_shape=[jax.ShapeDtypeStruct((bsz, t_len, width), F32),
                   jax.ShapeDtypeStruct(s0.shape, F32)],
        scratch_shapes=[pltpu.VMEM((n_pairs, hd, 2 * hd), F32),
                        pltpu.VMEM((SUBLANES, proj), F32),
                        pltpu.VMEM((SUBLANES + tile, proj), F32),
                        pltpu.VMEM((tile, width), F32)],
        compiler_params=pltpu.CompilerParams(dimension_semantics=("parallel", "arbitrary"),
                                             vmem_limit_bytes=VMEM_LIMIT),
        name="rwkv7",
    )(z, shift_prev, s0, mu, w0, w2p, a0, a2p, g2, k_k, k_a, r_k, ln_w, ln_b)
    s_fin = s_fin.reshape(bsz, n_pairs, hd, 2, hd).transpose(0, 1, 3, 2, 4).reshape(bsz, n_heads, hd, hd)
    return o_rw, s_fin


def _gelu_exact(x):
    return 0.5 * x * (1.0 + lax.erf(x * (2.0 ** -0.5)))


def _post_kernel(h_ref, ohg_ref, orw_ref, p_ref, cvprev_ref,
                 wohg_ref, worw_ref, nffn_ref, wg_ref, wv_ref, cw_ref, cb_ref, wo_ref,
                 nple_ref, pgate_ref, pproj_ref, fnorm_ref,
                 hout_ref, cvnew_ref, carry_ref, ugs_ref, *, tm, n_col_blocks, apply_final_norm):
    t_idx = pl.program_id(1)
    cols = FF_COLS

    @pl.when(t_idx == 0)
    def _():
        for j in range(n_col_blocks):
            carry_ref[j, SUBLANES - (CONV_W - 1):SUBLANES, :] = cvprev_ref[0, :, j * cols:(j + 1) * cols]

    h1 = (h_ref[0] + _dot(ohg_ref[0].astype(BF16), wohg_ref[...])
          + _dot(orw_ref[0].astype(BF16), worw_ref[...]))
    xn = _rmsnorm(h1, nffn_ref[...]).astype(BF16)
    acc = jnp.zeros_like(h1)
    for j in range(n_col_blocks):
        ug = _dot(xn, wg_ref[j])
        uv = _dot(xn, wv_ref[j])
        ugs_ref[0:SUBLANES, :] = carry_ref[j]
        ugs_ref[SUBLANES:SUBLANES + tm, :] = ug
        carry_ref[j] = ug[tm - SUBLANES:tm, :]
        cvnew_ref[0, :, j * cols:(j + 1) * cols] = ug[tm - (CONV_W - 1):tm, :]
        cw = cw_ref[j]
        c = (cb_ref[j] + cw[0:1, :] * ugs_ref[SUBLANES - 2:SUBLANES - 2 + tm, :]
             + cw[1:2, :] * ugs_ref[SUBLANES - 1:SUBLANES - 1 + tm, :] + cw[2:3, :] * ug)
        hid = (_gelu_exact(c) * uv).astype(BF16)
        acc = acc + _dot(hid, wo_ref[j])
    h2 = h1 + acc
    gate = _sigmoid(_dot(_rmsnorm(h2, nple_ref[...]).astype(BF16), pgate_ref[...]))
    h3 = h2 + gate * _dot(p_ref[0].astype(BF16), pproj_ref[...])
    if apply_final_norm:
        h3 = _rmsnorm(h3, fnorm_ref[...])
    hout_ref[0] = h3


def _post(h, o_hg, o_rw, p, cv_prev, wo_hg, wo_rw, norm_ffn, wg, wv, cw, cb, wo, norm_ple,
          ple_gate, ple_proj, final_norm, apply_final_norm):
    bsz, t_len, d = h.shape
    tm = min(512, t_len)
    assert t_len % tm == 0 and tm >= SUBLANES
    n_col_blocks = wg.shape[0]
    d_ff = n_col_blocks * FF_COLS
    kern = functools.partial(_post_kernel, tm=tm, n_col_blocks=n_col_blocks,
                             apply_final_norm=apply_final_norm)
    tok = lambda w: pl.BlockSpec((1, tm, w), lambda b, t: (b, t, 0))
    cv_spec = pl.BlockSpec((1, CONV_W - 1, d_ff), lambda b, t: (b, 0, 0))
    consts = [wo_hg, wo_rw, norm_ffn, wg, wv, cw, cb, wo, norm_ple, ple_gate, ple_proj, final_norm]
    return pl.pallas_call(
        kern,
        grid=(bsz, t_len // tm),
        in_specs=[tok(d), tok(o_hg.shape[2]), tok(o_rw.shape[2]), tok(p.shape[2]), cv_spec]
                 + [_const_spec(a.shape) for a in consts],
        out_specs=[tok(d), cv_spec],
        out_shape=[jax.ShapeDtypeStruct(h.shape, F32), jax.ShapeDtypeStruct(cv_prev.shape, F32)],
        scratch_shapes=[pltpu.VMEM((n_col_blocks, SUBLANES, FF_COLS), F32),
                        pltpu.VMEM((SUBLANES + tm, FF_COLS), F32)],
        compiler_params=pltpu.CompilerParams(dimension_semantics=("parallel", "arbitrary"),
                                             vmem_limit_bytes=VMEM_LIMIT),
        name="post",
    )(h, o_hg, o_rw, p, cv_prev, *consts)


def _layer_params(i, hg_proj, lb_all, w):
    d_ff = w["ffn_conv_b"].shape[1]
    n_cb = d_ff // FF_COLS
    rw_width = w["rw_w0"].shape[1]
    hg_width = w["hg_norm"].shape[1]
    row = lambda a: a.reshape(1, -1)
    lb = jnp.maximum(lb_all[i], 0.0)
    zeros_lora = jnp.zeros((RW_LORA_W, rw_width), F32)
    ffn_in = w["ffn_in"][i].astype(BF16)
    d_model = ffn_in.shape[0]
    blocked = lambda a: a.reshape(d_model, n_cb, FF_COLS).transpose(1, 0, 2)
    return dict(
        norm_mix=row(w["norm_mix"][i]),
        w_in_hg=w["w_in"][i][:, :hg_proj].astype(BF16),
        w_in_rw=w["w_in"][i][:, hg_proj:].astype(BF16),
        log_lb=row(jnp.log(lb)), one_m_lb=row(1.0 - lb), hg_norm=row(w["hg_norm"][i]),
        mu=row(w["rw_mu"][i]), w0=row(w["rw_w0"][i]),
        w2p=jnp.concatenate([w["rw_w2"][i], zeros_lora], axis=0),
        a0=row(w["rw_a0"][i]),
        a2p=jnp.concatenate([zeros_lora, w["rw_a2"][i]], axis=0),
        g2=w["rw_g2"][i].astype(BF16),
        k_k=row(w["rw_k_k"][i]), k_a=row(w["rw_k_a"][i]), r_k=row(w["rw_r_k"][i]),
        ln_w=row(w["rw_ln_w"][i]), ln_b=row(w["rw_ln_b"][i]),
        wo_hg=w["w_out"][i][:hg_width].astype(BF16), wo_rw=w["w_out"][i][hg_width:].astype(BF16),
        norm_ffn=row(w["norm_ffn"][i]),
        wg=blocked(ffn_in[:, :d_ff]), wv=blocked(ffn_in[:, d_ff:]),
        cw=w["ffn_conv_w"][i].reshape(CONV_W, n_cb, FF_COLS).transpose(1, 0, 2),
        cb=w["ffn_conv_b"][i].reshape(n_cb, 1, FF_COLS),
        wo=w["ffn_out"][i].astype(BF16).reshape(n_cb, FF_COLS, d_model),
        norm_ple=row(w["norm_ple"][i]),
        ple_gate=w["ple_gate"][i].astype(BF16), ple_proj=w["ple_proj"][i].astype(BF16),
        final_norm=row(w["final_norm"]),
    )


def _trunk(x, p, s_hg, s_rw, s_sh, s_cv, layers):
    bsz, t_len, d = x.shape
    depth = len(layers)
    h = x
    new_hg, new_rw, new_sh, new_cv = [], [], [], []
    for i, lp in enumerate(layers):
        z_hg, z_rw = _in_proj(h.reshape(bsz * t_len, d), lp["norm_mix"], lp["w_in_hg"], lp["w_in_rw"])
        z_hg = z_hg.reshape(bsz, t_len, -1)
        z_rw = z_rw.reshape(bsz, t_len, -1)
        o_hg, n_hg = _hgrn2(z_hg, lp["log_lb"], lp["one_m_lb"], lp["hg_norm"], s_hg[i])
        o_rw, n_rw = _rwkv7(z_rw, s_sh[i][:, None, :], s_rw[i], lp["mu"], lp["w0"], lp["w2p"],
                            lp["a0"], lp["a2p"], lp["g2"], lp["k_k"], lp["k_a"], lp["r_k"],
                            lp["ln_w"], lp["ln_b"])
        h, n_cv = _post(h, o_hg, o_rw, p[i], s_cv[i], lp["wo_hg"], lp["wo_rw"], lp["norm_ffn"],
                        lp["wg"], lp["wv"], lp["cw"], lp["cb"], lp["wo"], lp["norm_ple"],
                        lp["ple_gate"], lp["ple_proj"], lp["final_norm"], i == depth - 1)
        new_hg.append(n_hg)
        new_rw.append(n_rw)
        new_sh.append(z_rw[:, -1])
        new_cv.append(n_cv)
    return h, jnp.stack(new_hg), jnp.stack(new_rw), jnp.stack(new_sh), jnp.stack(new_cv)


def kernel(x_prompt, x_sample, state_hgrn, state_rwkv, state_shift, state_conv, p_prompt, p_sample,
           norm_mix, w_in, lb_raw, hg_norm, rw_mu, rw_w0, rw_w2, rw_a0, rw_a2, rw_g2, rw_k_k, rw_k_a,
           rw_r_k, rw_ln_w, rw_ln_b, w_out, norm_ffn, ffn_in, ffn_conv_w, ffn_conv_b, ffn_out,
           norm_ple, ple_gate, ple_proj, final_norm):
    w = dict(norm_mix=norm_mix, w_in=w_in, hg_norm=hg_norm, rw_mu=rw_mu, rw_w0=rw_w0, rw_w2=rw_w2,
             rw_a0=rw_a0, rw_a2=rw_a2, rw_g2=rw_g2, rw_k_k=rw_k_k, rw_k_a=rw_k_a,
             rw_r_k=rw_r_k.reshape(rw_r_k.shape[0], -1), rw_ln_w=rw_ln_w, rw_ln_b=rw_ln_b, w_out=w_out,
             norm_ffn=norm_ffn, ffn_in=ffn_in, ffn_conv_w=ffn_conv_w, ffn_conv_b=ffn_conv_b,
             ffn_out=ffn_out, norm_ple=norm_ple, ple_gate=ple_gate, ple_proj=ple_proj,
             final_norm=final_norm)
    depth = w_in.shape[0]
    hg_proj = 4 * hg_norm.shape[1]
    lb_all = jnp.cumsum(jax.nn.softmax(lb_raw.astype(F32), axis=0), axis=0)
    lb_all = lb_all - lb_all[:1]
    layers = [_layer_params(i, hg_proj, lb_all, w) for i in range(depth)]

    nb = x_prompt.shape[0]
    zeros_like_state = lambda s: jnp.zeros((depth, nb) + s.shape[2:], s.dtype)
    y_p, hg_p, rw_p, sh_p, cv_p = _trunk(x_prompt, p_prompt, zeros_like_state(state_hgrn),
                                         zeros_like_state(state_rwkv), zeros_like_state(state_shift),
                                         zeros_like_state(state_conv), layers)
    y_s, hg_s, rw_s, sh_s, cv_s = _trunk(x_sample, p_sample, state_hgrn, state_rwkv, state_shift,
                                         state_conv, layers)
    return (y_p, y_s, hg_p, rw_p, sh_p, cv_p, hg_s, rw_s, sh_s, cv_s)
```

```python
import functools

import jax
import jax.numpy as jnp
from jax import lax
from jax.experimental import pallas as pl
from jax.experimental.pallas import tpu as pltpu

F32 = jnp.float32
BF16 = jnp.bfloat16

NORM_EPS = 1e-6
RW_GN_EPS = 64e-5
HG_HEAD_DIM = 128
RW_HEAD_DIM = 64
RW_LORA_W = 64
CONV_W = 3
HG_BLOCK = 16
RW_CHUNK = 64
RW_SUB = 16
MIX_TILE = 128
FF_COLS = 256
LANES = 128
SUBLANES = 8
VMEM_LIMIT = 56 * 1024 * 1024

NN = ((1,), (0,))
NT = ((1,), (1,))
TN = ((0,), (0,))


def _dot(a, b, dims=NN):
    return lax.dot_general(a, b, (dims, ((), ())), preferred_element_type=F32)


def _dot1(a, b, dims=NN):
    return _dot(a.astype(BF16), b.astype(BF16), dims)


def _split2(x):
    hi = x.astype(BF16)
    lo = (x - hi.astype(F32)).astype(BF16)
    return hi, lo


def _split3(x):
    hi = x.astype(BF16)
    r1 = x - hi.astype(F32)
    mid = r1.astype(BF16)
    lo = (r1 - mid.astype(F32)).astype(BF16)
    return hi, mid, lo


def _dot3(a, b, dims=NN):
    ah, al = _split2(a)
    bh, bl = _split2(b)
    return _dot(ah, bh, dims) + (_dot(ah, bl, dims) + _dot(al, bh, dims))


def _dot_xl(a_exact, x, dims=NN):
    x0, x1, x2 = _split3(x)
    return _dot(a_exact, x0, dims) + (_dot(a_exact, x1, dims) + _dot(a_exact, x2, dims))


def _dot_xr2(x, b_exact, dims=NN):
    hi, lo = _split2(x)
    return _dot(hi, b_exact, dims) + _dot(lo, b_exact, dims)


def _sigmoid(x):
    return 0.5 * jnp.tanh(0.5 * x) + 0.5


def _softplus(x):
    return jnp.maximum(x, 0.0) + jnp.log(1.0 + jnp.exp(-jnp.abs(x)))


def _rmsnorm(x, g):
    return x * lax.rsqrt(jnp.mean(x * x, axis=-1, keepdims=True) + NORM_EPS) * g


def _iota2(shape, axis):
    return lax.broadcasted_iota(jnp.int32, shape, axis)


def _const_spec(shape):
    zeros = (0,) * len(shape)
    return pl.BlockSpec(shape, lambda *_: zeros, pipeline_mode=pl.Buffered(1))


def _map(fn, *lists):
    return [fn(*xs) for xs in zip(*lists)]


def _inproj_kernel(x_ref, g_ref, whg_ref, wrw_ref, zhg_ref, zrw_ref):
    xn = _rmsnorm(x_ref[...], g_ref[...]).astype(BF16)
    zhg_ref[...] = _dot(xn, whg_ref[...])
    zrw_ref[...] = _dot(xn, wrw_ref[...])


def _in_proj(x2, g, w_hg, w_rw):
    n, d = x2.shape
    tm = min(512, n)
    assert n % tm == 0
    return pl.pallas_call(
        _inproj_kernel,
        grid=(n // tm,),
        in_specs=[pl.BlockSpec((tm, d), lambda i: (i, 0)),
                  _const_spec((1, d)),
                  _const_spec(w_hg.shape),
                  _const_spec(w_rw.shape)],
        out_specs=[pl.BlockSpec((tm, w_hg.shape[1]), lambda i: (i, 0)),
                   pl.BlockSpec((tm, w_rw.shape[1]), lambda i: (i, 0))],
        out_shape=[jax.ShapeDtypeStruct((n, w_hg.shape[1]), F32),
                   jax.ShapeDtypeStruct((n, w_rw.shape[1]), F32)],
        compiler_params=pltpu.CompilerParams(dimension_semantics=("parallel",),
                                             vmem_limit_bytes=VMEM_LIMIT),
        name="in_proj",
    )(x2, g, w_hg, w_rw)


HG_Q, HG_K, HG_BL, HG_V, HG_GATE = range(5)


def _hgrn_front(z_ref, loglb_ref, omlb_ref, gn_ref, pre_ref, p_ref, *, n_heads, tile):
    dk = HG_HEAD_DIM
    width = n_heads * dk
    blk = HG_BLOCK
    zq = z_ref[0, :, 0:width]
    zf = z_ref[0, :, width:2 * width]
    zg = z_ref[0, :, 3 * width:4 * width]
    pre_ref[HG_Q] = zq * _sigmoid(zq)
    pre_ref[HG_V] = z_ref[0, :, 2 * width:3 * width]
    pre_ref[HG_GATE] = gn_ref[...] * (zg * _sigmoid(zg))
    soft = jnp.log(1.0 + jnp.exp(-jnp.abs(zf)))
    log_sig_pos = jnp.minimum(zf, 0.0) - soft
    log_sig_neg = jnp.minimum(-zf, 0.0) - soft
    lo_term = loglb_ref[...] + log_sig_neg
    top = jnp.maximum(log_sig_pos, lo_term)
    logf = top + jnp.log(1.0 + jnp.exp(-jnp.abs(log_sig_pos - lo_term)))
    pre_ref[HG_K] = omlb_ref[...] * jnp.exp(log_sig_neg)
    row = _iota2((tile, tile), 0)
    col = _iota2((tile, tile), 1)
    tri = ((row // blk == col // blk) & (col <= row)).astype(BF16)
    pre_ref[HG_BL] = _dot_xl(tri, logf)
    yield

    t_row = _iota2((blk, dk), 0)
    for i in range(tile // blk):
        rows = slice(i * blk, (i + 1) * blk)
        for h in range(n_heads):
            lanes = slice(h * dk, (h + 1) * dk)
            qb = pre_ref[HG_Q, rows, lanes]
            kb = pre_ref[HG_K, rows, lanes]
            bb = pre_ref[HG_BL, rows, lanes]
            p0 = (i * n_heads + h) * blk
            for s in range(blk):
                causal = t_row >= s
                ps = jnp.where(causal, qb * kb[s:s + 1, :] * jnp.exp(bb - bb[s:s + 1, :]), 0.0)
                p_ref[p0:p0 + blk, s * dk:(s + 1) * dk] = ps.astype(BF16)
            yield


def _hgrn_back(sel_ref, pre_ref, p_ref, a_ref, st_ref, o_ref, *, n_heads, tile):
    dk = HG_HEAD_DIM
    blk = HG_BLOCK
    n_blocks = tile // blk
    a_ref[...] = _dot(p_ref[...], sel_ref[...])
    yield

    block_rows = [slice(i * blk, (i + 1) * blk) for i in range(n_blocks)]
    e_end = [pre_ref[HG_BL, (i + 1) * blk - 1:(i + 1) * blk, :] for i in range(n_blocks)]
    start = [jnp.zeros_like(e_end[0])]
    for i in range(n_blocks):
        start.append(start[i] + e_end[i])
    qe = [pre_ref[HG_Q, r, :] * jnp.exp(pre_ref[HG_BL, r, :]) for r in block_rows]
    ke = [pre_ref[HG_K, r, :] * jnp.exp(e - pre_ref[HG_BL, r, :])
          for r, e in zip(block_rows, e_end)]
    vv = [pre_ref[HG_V, r, :] for r in block_rows]
    q_tile = jnp.concatenate([x * jnp.exp(s) for x, s in zip(qe, start)], axis=0)
    k_tile = jnp.concatenate([x * jnp.exp(start[n_blocks] - s) for x, s in zip(ke, start[1:])], axis=0)
    v_tile = pre_ref[HG_V]
    decay_tile = jnp.exp(start[n_blocks])
    yield

    heads = range(n_heads)
    lanes = [slice(h * dk, (h + 1) * dk) for h in heads]
    st = [st_ref[h] for h in heads]
    new_st = [st[h] * decay_tile[:, lanes[h]] + _dot1(v_tile[:, lanes[h]], k_tile[:, lanes[h]], TN)
              for h in heads]
    yield
    o_state = [_dot1(q_tile[:, lanes[h]], st[h], NT) for h in heads]
    o_blocks = [[o_state[h][r] for r in block_rows] for h in heads]
    yield
    span = n_blocks // 2
    while span >= 1:
        for lo in range(0, n_blocks, 2 * span):
            mid, hi = lo + span, lo + 2 * span
            q_r = jnp.concatenate([qe[i] if i == mid else qe[i] * jnp.exp(start[i] - start[mid])
                                   for i in range(mid, hi)], axis=0)
            k_l = jnp.concatenate([ke[j] if j == mid - 1 else ke[j] * jnp.exp(start[mid] - start[j + 1])
                                   for j in range(lo, mid)], axis=0)
            v_l = jnp.concatenate(vv[lo:mid], axis=0)
            scores = [_dot1(q_r[:, lanes[h]], k_l[:, lanes[h]], NT) for h in heads]
            upd = [_dot1(scores[h], v_l[:, lanes[h]]) for h in heads]
            for h in heads:
                for n, i in enumerate(range(mid, hi)):
                    o_blocks[h][i] = o_blocks[h][i] + upd[h][n * blk:(n + 1) * blk]
            yield
        span //= 2
    for i in range(n_blocks):
        for h in heads:
            p0 = (i * n_heads + h) * blk
            o_blocks[h][i] = o_blocks[h][i] + _dot1(a_ref[p0:p0 + blk, 0:blk], vv[i][:, lanes[h]])
        yield
    for h in heads:
        o = jnp.concatenate(o_blocks[h], axis=0)
        o = o * lax.rsqrt(jnp.mean(o * o, axis=-1, keepdims=True) + NORM_EPS)
        o_ref[0, :, lanes[h]] = o * pre_ref[HG_GATE, :, lanes[h]]
    for h in heads:
        st_ref[h] = new_st[h]


RW_KT, RW_RT, RW_V, RW_BH, RW_KH, RW_BL, RW_KL, RW_C, RW_RK, RW_G = range(10)


def _block_diag(y):
    lo = _iota2(y.shape, 1) < RW_HEAD_DIM
    y = y.astype(BF16)
    zero = jnp.zeros_like(y)
    return jnp.concatenate([jnp.where(lo, y, zero), jnp.where(lo, zero, y)], axis=0)


def _pair_dot(x, y):
    return _dot(x.astype(BF16), _block_diag(y))


def _pair_dot_nt(x, y):
    return _dot(x.astype(BF16), _block_diag(y), NT)


def _pair_dot2(x, y, z):
    return _dot(x.astype(BF16), jnp.concatenate([_block_diag(y), _block_diag(z)], axis=1))


def _pair_dot2_nt(x, y, z):
    return _dot(x.astype(BF16), jnp.concatenate([_block_diag(y), _block_diag(z)], axis=0), NT)


def _unit_lower_inverses(n_list, eye, same_sub):
    d = [jnp.where(same_sub, n, 0.0) for n in n_list]
    l_off = _map(lambda n, dd: n - dd, n_list, d)
    t_d = [eye - dd for dd in d]
    d_pow = d
    for _ in range(RW_SUB.bit_length() - 2):
        d_pow = _map(_pair_dot, d_pow, d_pow)
        yield
        t_d = _map(lambda t, dp: t + _pair_dot(t, dp), t_d, d_pow)
        yield
    p = _map(_pair_dot, t_d, l_off)
    yield
    p2 = _map(_pair_dot, p, p)
    yield
    t_p = _map(lambda pp, pp2: (eye - pp) + _pair_dot(eye - pp, pp2), p, p2)
    yield
    return _map(_pair_dot, t_p, t_d)


def _rwkv_front(z_ref, mu_ref, w0_ref, w2_ref, a0_ref, a2_ref, g2_ref, kk_ref, ka_ref, rk_ref,
                hones_ref, pre_ref, carry_ref, zs_ref, *, n_heads, chunk, tile):
    hd = RW_HEAD_DIM
    width = n_heads * hd
    z = z_ref[0]
    zs_ref[0:SUBLANES, :] = carry_ref[...]
    zs_ref[SUBLANES:SUBLANES + tile, :] = z
    carry_ref[...] = z[tile - SUBLANES:tile, :]
    z_prev = zs_ref[SUBLANES - 1:SUBLANES - 1 + tile, :]
    zm = z + (z_prev - z) * mu_ref[...]

    r = zm[:, 0:width]
    k = zm[:, width:2 * width]
    v = zm[:, 2 * width:3 * width]
    lora_in = zm[:, 3 * width:3 * width + 2 * RW_LORA_W]
    gd = zm[:, 3 * width + 2 * RW_LORA_W:]
    pre_ref[RW_V] = v
    pre_ref[RW_G] = _dot(_sigmoid(gd).astype(BF16), g2_ref[...])
    yield

    w_pre = w0_ref[...] + _dot3(jnp.tanh(lora_in), w2_ref[...])
    w_raw = -_softplus(-w_pre) - 0.5
    logw = -jnp.exp(w_raw)
    a = _sigmoid(a0_ref[...] + _dot3(lora_in, a2_ref[...]))
    kk = k * kk_ref[...]
    kk = kk * jnp.minimum(lax.rsqrt(_dot_xr2(kk * kk, hones_ref[...])), 1e12)
    k2 = k * (1.0 + (a - 1.0) * ka_ref[...])
    beta = kk * a
    pre_ref[RW_RK] = r * k2 * rk_ref[...]
    yield

    trow = _iota2((tile, tile), 0)
    tcol = _iota2((tile, tile), 1)
    tri = ((trow // chunk == tcol // chunk) & (tcol <= trow)).astype(BF16)
    c = _dot_xl(tri, logw)
    pre_ref[RW_C] = c
    e_neg = jnp.exp(-c)
    pre_ref[RW_KT] = kk * jnp.exp(c - logw)
    pre_ref[RW_RT] = r * jnp.exp(c)
    pre_ref[RW_BH] = beta * e_neg
    pre_ref[RW_KH] = k2 * e_neg
    yield
    for ci in range(tile // chunk):
        rows = slice(ci * chunk, (ci + 1) * chunk)
        e_end = jnp.exp(c[rows][chunk - 1:chunk, :] - c[rows])
        pre_ref[RW_BL, rows, :] = beta[rows] * e_end
        pre_ref[RW_KL, rows, :] = k2[rows] * e_end
    yield


def _rwkv_back(pre_ref, hones_ref, lnw_ref, lnb_ref, s_ref, y_ref, o_ref, *, n_heads, chunk, tile):
    hd = RW_HEAD_DIM
    n_chunks = tile // chunk
    pw = 2 * hd
    pairs = range(n_heads // 2)
    lanes = [slice(p * pw, (p + 1) * pw) for p in pairs]
    row = _iota2((chunk, pw), 0)
    lane = _iota2((chunk, pw), 1)
    lo_half = lane < hd
    col = jnp.where(lo_half, lane, lane - hd)
    incl2 = jnp.concatenate([col < row, col <= row], axis=0)
    incl4 = jnp.concatenate([incl2, incl2], axis=1)
    eye = (row == col).astype(F32)
    same_sub = (row // RW_SUB) == (col // RW_SUB)
    same_head = (_iota2((pw, pw), 0) // hd) == (_iota2((pw, pw), 1) // hd)

    chunk_rows = [slice(ci * chunk, (ci + 1) * chunk) for ci in range(n_chunks)]
    units = [(ci, p) for ci in range(n_chunks) for p in pairs]
    at = lambda name: [pre_ref[name, chunk_rows[ci], lanes[p]] for ci, p in units]
    kt, rt, vv, b_l, k_l = at(RW_KT), at(RW_RT), at(RW_V), at(RW_BL), at(RW_KL)
    g_last = [jnp.exp(pre_ref[RW_C, (ci + 1) * chunk - 1:(ci + 1) * chunk, lanes[p]]) for ci, p in units]
    lhs = _map(lambda x, y: jnp.concatenate([x, y], axis=0), kt, rt)
    g_bk = [jnp.where(incl4, _pair_dot2_nt(l, xb, xk), 0.0)
            for l, xb, xk in zip(lhs, at(RW_BH), at(RW_KH))]
    a_ab = [x[0:chunk, 0:pw] for x in g_bk]
    a_rb = [x[chunk:, 0:pw] for x in g_bk]
    yield
    gkv = [_pair_dot(x[:, pw:], v_) for x, v_ in zip(g_bk, vv)]
    x_loc = [x[0:chunk] for x in gkv]
    yield
    t_inv = yield from _unit_lower_inverses(a_ab, eye, same_sub)
    yield
    wu = _map(_pair_dot2, t_inv, kt, x_loc)
    w_t = [x[:, 0:pw] for x in wu]
    u_loc = [x[:, pw:] for x in wu]
    yield
    a_wu = _map(_pair_dot2, a_rb, w_t, u_loc)
    q_c = _map(lambda r_, x: r_ - x[:, 0:pw], rt, a_wu)
    y_loc = _map(lambda x, y: x[chunk:] - y[:, pw:], gkv, a_wu)
    yield
    m_off = _map(lambda w_, bl_: jnp.where(same_head, -_dot1(w_, bl_, TN), 0.0), w_t, b_l)
    yield
    c_full = _map(lambda v_, u_, kl_, bl_: _dot1(jnp.concatenate([v_, -u_], axis=0),
                                                 jnp.concatenate([kl_, bl_], axis=0), TN),
                  vv, u_loc, k_l, b_l)
    c_add = [jnp.where(lo_half, x[0:hd], x[hd:]) for x in c_full]
    yield
    state = [s_ref[p] for p in pairs]
    for idx, (ci, p) in enumerate(units):
        y_ref[chunk_rows[ci], lanes[p]] = _pair_dot_nt(q_c[idx], state[p]) + y_loc[idx]
        state[p] = state[p] * g_last[idx] + _dot1(state[p], m_off[idx]) + c_add[idx]
    for p in pairs:
        s_ref[p] = state[p]
    yield

    y = y_ref[...]
    head_ones = hones_ref[...]
    mean = _dot_xr2(y, head_ones) * (1.0 / hd)
    yc = y - mean
    var = _dot_xr2(yc * yc, head_ones) * (1.0 / hd)
    yn = yc * lax.rsqrt(var + RW_GN_EPS) * lnw_ref[...] + lnb_ref[...]
    bonus = _dot_xr2(pre_ref[RW_RK], head_ones) * pre_ref[RW_V]
    o_ref[0] = (yn + bonus) * pre_ref[RW_G]


_DONE = object()
MIX_STEPS = (1, 1, 1, 2)


def _mixer_kernel(zhg_ref, zrw_ref, loglb_ref, omlb_ref, gn_ref, sel_ref, s0hg_ref,
                  shift_ref, s0rw_ref, mu_ref, w0_ref, w2_ref, a0_ref, a2_ref, g2_ref,
                  kk_ref, ka_ref, rk_ref, lnw_ref, lnb_ref, hones_ref,
                  ohg_ref, sfin_hg_ref, orw_ref, sfin_rw_ref,
                  st_ref, s_ref, carry_ref, zs_ref, y_ref, a_ref,
                  hg_pre_a, hg_pre_b, p_a, p_b, rw_pre_a, rw_pre_b,
                  *, hg_heads, rw_heads, tile):
    j = pl.program_id(1)
    hg = dict(n_heads=hg_heads, tile=tile)
    rw = dict(n_heads=rw_heads, chunk=RW_CHUNK, tile=tile)

    @pl.when(j == 0)
    def _():
        for h in range(hg_heads):
            st_ref[h] = s0hg_ref[0, h].T
        s_ref[...] = s0rw_ref[0]
        carry_ref[...] = jnp.broadcast_to(shift_ref[0], carry_ref.shape)
        hg_pre_b[...] = jnp.zeros(hg_pre_b.shape, hg_pre_b.dtype)
        p_b[...] = jnp.zeros(p_b.shape, p_b.dtype)
        rw_pre_b[...] = jnp.zeros(rw_pre_b.shape, rw_pre_b.dtype)

    def step(hg_w, p_w, rw_w, hg_r, p_r, rw_r):
        bodies = [
            _rwkv_back(rw_r, hones_ref, lnw_ref, lnb_ref, s_ref, y_ref, orw_ref, **rw),
            _hgrn_back(sel_ref, hg_r, p_r, a_ref, st_ref, ohg_ref, **hg),
            _rwkv_front(zrw_ref, mu_ref, w0_ref, w2_ref, a0_ref, a2_ref, g2_ref, kk_ref, ka_ref,
                        rk_ref, hones_ref, rw_w, carry_ref, zs_ref, **rw),
            _hgrn_front(zhg_ref, loglb_ref, omlb_ref, gn_ref, hg_w, p_w, **hg),
        ]
        live = [True] * len(bodies)
        while any(live):
            for n, body in enumerate(bodies):
                for _ in range(MIX_STEPS[n]):
                    if live[n]:
                        live[n] = next(body, _DONE) is not _DONE

    @pl.when(j % 2 == 0)
    def _():
        step(hg_pre_a, p_a, rw_pre_a, hg_pre_b, p_b, rw_pre_b)

    @pl.when(j % 2 == 1)
    def _():
        step(hg_pre_b, p_b, rw_pre_b, hg_pre_a, p_a, rw_pre_a)

    @pl.when(j == pl.num_programs(1) - 1)
    def _():
        for h in range(hg_heads):
            sfin_hg_ref[0, h] = st_ref[h].T
        sfin_rw_ref[0] = s_ref[...]


def _mixers(z_hg, z_rw, log_lb, one_m_lb, hg_norm, s0_hg, shift_prev, s0_rw,
            mu, w0, w2p, a0, a2p, g2, k_k, k_a, r_k, ln_w, ln_b):
    bsz, t_len, hg_proj = z_hg.shape
    rw_proj = z_rw.shape[2]
    hg_width = hg_proj // 4
    hg_heads = hg_width // HG_HEAD_DIM
    rw_heads = s0_rw.shape[1]
    hd = RW_HEAD_DIM
    rw_width = rw_heads * hd
    tile = min(MIX_TILE, t_len)
    assert t_len % tile == 0 and tile % RW_CHUNK == 0 and tile % HG_BLOCK == 0 and rw_heads % 2 == 0
    n_tiles = t_len // tile
    n_pairs = rw_heads // 2
    s0_rw = s0_rw.reshape(bsz, n_pairs, 2, hd, hd).transpose(0, 1, 3, 2, 4).reshape(bsz, n_pairs, hd, 2 * hd)
    group = jnp.arange(HG_BLOCK * HG_HEAD_DIM, dtype=jnp.int32)[:, None] // HG_HEAD_DIM
    selector = (group == jnp.arange(LANES, dtype=jnp.int32)[None, :]).astype(BF16)
    head_of = jnp.arange(rw_width, dtype=jnp.int32) // hd
    head_ones = (head_of[:, None] == head_of[None, :]).astype(BF16)

    tok_in = lambda w: pl.BlockSpec((1, tile, w), lambda b, j: (b, jnp.minimum(j, n_tiles - 1), 0))
    tok_out = lambda w: pl.BlockSpec((1, tile, w), lambda b, j: (b, jnp.maximum(j - 1, 0), 0))
    per_row = lambda a: pl.BlockSpec((1,) + a.shape[1:], lambda b, j: (b,) + (0,) * (a.ndim - 1))
    consts_hg = [log_lb, one_m_lb, hg_norm, selector]
    consts_rw = [mu, w0, w2p, a0, a2p, g2, k_k, k_a, r_k, ln_w, ln_b, head_ones]
    kern = functools.partial(_mixer_kernel, hg_heads=hg_heads, rw_heads=rw_heads, tile=tile)
    o_hg, n_hg, o_rw, n_rw = pl.pallas_call(
        kern,
        grid=(bsz, n_tiles + 1),
        in_specs=[tok_in(hg_proj), tok_in(rw_proj)]
                 + [_const_spec(a.shape) for a in consts_hg] + [per_row(s0_hg)]
                 + [per_row(shift_prev), per_row(s0_rw)] + [_const_spec(a.shape) for a in consts_rw],
        out_specs=[tok_out(hg_width), per_row(s0_hg), tok_out(rw_width), per_row(s0_rw)],
        out_shape=[jax.ShapeDtypeStruct((bsz, t_len, hg_width), F32),
                   jax.ShapeDtypeStruct(s0_hg.shape, F32),
                   jax.ShapeDtypeStruct((bsz, t_len, rw_width), F32),
                   jax.ShapeDtypeStruct(s0_rw.shape, F32)],
        scratch_shapes=[pltpu.VMEM((hg_heads, HG_HEAD_DIM, HG_HEAD_DIM), F32),
                        pltpu.VMEM((n_pairs, hd, 2 * hd), F32),
                        pltpu.VMEM((SUBLANES, rw_proj), F32),
                        pltpu.VMEM((SUBLANES + tile, rw_proj), F32),
                        pltpu.VMEM((tile, rw_width), F32),
                        pltpu.VMEM((tile * hg_heads, LANES), F32)]
                       + [pltpu.VMEM((5, tile, hg_width), F32)] * 2
                       + [pltpu.VMEM((tile * hg_heads, HG_BLOCK * HG_HEAD_DIM), BF16)] * 2
                       + [pltpu.VMEM((10, tile, rw_width), F32)] * 2,
        compiler_params=pltpu.CompilerParams(dimension_semantics=("parallel", "arbitrary"),
                                             vmem_limit_bytes=VMEM_LIMIT),
        name="mixers",
    )(z_hg, z_rw, *consts_hg, s0_hg, shift_prev, s0_rw, *consts_rw)
    n_rw = n_rw.reshape(bsz, n_pairs, hd, 2, hd).transpose(0, 1, 3, 2, 4).reshape(bsz, rw_heads, hd, hd)
    return o_hg, n_hg, o_rw, n_rw


def _gelu_exact(x):
    return 0.5 * x * (1.0 + lax.erf(x * (2.0 ** -0.5)))


def _post_kernel(h_ref, ohg_ref, orw_ref, p_ref, cvprev_ref,
                 wohg_ref, worw_ref, nffn_ref, wg_ref, wv_ref, cw_ref, cb_ref, wo_ref,
                 nple_ref, pgate_ref, pproj_ref, fnorm_ref,
                 hout_ref, cvnew_ref, carry_ref, ugs_ref, *, tm, n_col_blocks, apply_final_norm):
    t_idx = pl.program_id(1)
    cols = FF_COLS

    @pl.when(t_idx == 0)
    def _():
        carry_ref[...] = jnp.zeros(carry_ref.shape, carry_ref.dtype)
        for j in range(n_col_blocks):
            carry_ref[j, SUBLANES - (CONV_W - 1):SUBLANES, :] = cvprev_ref[0, :, j * cols:(j + 1) * cols]

    h1 = (h_ref[0] + _dot(ohg_ref[0].astype(BF16), wohg_ref[...])
          + _dot(orw_ref[0].astype(BF16), worw_ref[...]))
    xn = _rmsnorm(h1, nffn_ref[...]).astype(BF16)
    acc = jnp.zeros_like(h1)
    for j in range(n_col_blocks):
        ug = _dot(xn, wg_ref[j])
        uv = _dot(xn, wv_ref[j])
        ugs_ref[0:SUBLANES, :] = carry_ref[j]
        ugs_ref[SUBLANES:SUBLANES + tm, :] = ug
        carry_ref[j] = ug[tm - SUBLANES:tm, :]
        cvnew_ref[0, :, j * cols:(j + 1) * cols] = ug[tm - (CONV_W - 1):tm, :]
        cw = cw_ref[j]
        c = (cb_ref[j] + cw[0:1, :] * ugs_ref[SUBLANES - 2:SUBLANES - 2 + tm, :]
             + cw[1:2, :] * ugs_ref[SUBLANES - 1:SUBLANES - 1 + tm, :] + cw[2:3, :] * ug)
        hid = (_gelu_exact(c) * uv).astype(BF16)
        acc = acc + _dot(hid, wo_ref[j])
    h2 = h1 + acc
    gate = _sigmoid(_dot(_rmsnorm(h2, nple_ref[...]).astype(BF16), pgate_ref[...]))
    h3 = h2 + gate * _dot(p_ref[0].astype(BF16), pproj_ref[...])
    if apply_final_norm:
        h3 = _rmsnorm(h3, fnorm_ref[...])
    hout_ref[0] = h3


def _post(h, o_hg, o_rw, p, cv_prev, wo_hg, wo_rw, norm_ffn, wg, wv, cw, cb, wo, norm_ple,
          ple_gate, ple_proj, final_norm, apply_final_norm):
    bsz, t_len, d = h.shape
    tm = min(512, t_len)
    assert t_len % tm == 0 and tm >= SUBLANES
    n_col_blocks = wg.shape[0]
    d_ff = n_col_blocks * FF_COLS
    kern = functools.partial(_post_kernel, tm=tm, n_col_blocks=n_col_blocks,
                             apply_final_norm=apply_final_norm)
    tok = lambda w: pl.BlockSpec((1, tm, w), lambda b, t: (b, t, 0))
    cv_spec = pl.BlockSpec((1, CONV_W - 1, d_ff), lambda b, t: (b, 0, 0))
    consts = [wo_hg, wo_rw, norm_ffn, wg, wv, cw, cb, wo, norm_ple, ple_gate, ple_proj, final_norm]
    return pl.pallas_call(
        kern,
        grid=(bsz, t_len // tm),
        in_specs=[tok(d), tok(o_hg.shape[2]), tok(o_rw.shape[2]), tok(p.shape[2]), cv_spec]
                 + [_const_spec(a.shape) for a in consts],
        out_specs=[tok(d), cv_spec],
        out_shape=[jax.ShapeDtypeStruct(h.shape, F32), jax.ShapeDtypeStruct(cv_prev.shape, F32)],
        scratch_shapes=[pltpu.VMEM((n_col_blocks, SUBLANES, FF_COLS), F32),
                        pltpu.VMEM((SUBLANES + tm, FF_COLS), F32)],
        compiler_params=pltpu.CompilerParams(dimension_semantics=("parallel", "arbitrary"),
                                             vmem_limit_bytes=VMEM_LIMIT),
        name="post",
    )(h, o_hg, o_rw, p, cv_prev, *consts)


def _layer_params(i, hg_proj, lb_all, w):
    d_ff = w["ffn_conv_b"].shape[1]
    n_cb = d_ff // FF_COLS
    rw_width = w["rw_w0"].shape[1]
    hg_width = w["hg_norm"].shape[1]
    row = lambda a: a.reshape(1, -1)
    lb = jnp.maximum(lb_all[i], 0.0)
    zeros_lora = jnp.zeros((RW_LORA_W, rw_width), F32)
    ffn_in = w["ffn_in"][i].astype(BF16)
    d_model = ffn_in.shape[0]
    blocked = lambda a: a.reshape(d_model, n_cb, FF_COLS).transpose(1, 0, 2)
    return dict(
        norm_mix=row(w["norm_mix"][i]),
        w_in_hg=w["w_in"][i][:, :hg_proj].astype(BF16),
        w_in_rw=w["w_in"][i][:, hg_proj:].astype(BF16),
        log_lb=row(jnp.log(lb)), one_m_lb=row(1.0 - lb), hg_norm=row(w["hg_norm"][i]),
        mu=row(w["rw_mu"][i]), w0=row(w["rw_w0"][i]),
        w2p=jnp.concatenate([w["rw_w2"][i], zeros_lora], axis=0),
        a0=row(w["rw_a0"][i]),
        a2p=jnp.concatenate([zeros_lora, w["rw_a2"][i]], axis=0),
        g2=w["rw_g2"][i].astype(BF16),
        k_k=row(w["rw_k_k"][i]), k_a=row(w["rw_k_a"][i]), r_k=row(w["rw_r_k"][i]),
        ln_w=row(w["rw_ln_w"][i]), ln_b=row(w["rw_ln_b"][i]),
        wo_hg=w["w_out"][i][:hg_width].astype(BF16), wo_rw=w["w_out"][i][hg_width:].astype(BF16),
        norm_ffn=row(w["norm_ffn"][i]),
        wg=blocked(ffn_in[:, :d_ff]), wv=blocked(ffn_in[:, d_ff:]),
        cw=w["ffn_conv_w"][i].reshape(CONV_W, n_cb, FF_COLS).transpose(1, 0, 2),
        cb=w["ffn_conv_b"][i].reshape(n_cb, 1, FF_COLS),
        wo=w["ffn_out"][i].astype(BF16).reshape(n_cb, FF_COLS, d_model),
        norm_ple=row(w["norm_ple"][i]),
        ple_gate=w["ple_gate"][i].astype(BF16), ple_proj=w["ple_proj"][i].astype(BF16),
        final_norm=row(w["final_norm"]),
    )


def _trunk(x, p, s_hg, s_rw, s_sh, s_cv, layers):
    bsz, t_len, d = x.shape
    depth = len(layers)
    h = x
    new_hg, new_rw, new_sh, new_cv = [], [], [], []
    for i, lp in enumerate(layers):
        z_hg, z_rw = _in_proj(h.reshape(bsz * t_len, d), lp["norm_mix"], lp["w_in_hg"], lp["w_in_rw"])
        z_hg = z_hg.reshape(bsz, t_len, -1)
        z_rw = z_rw.reshape(bsz, t_len, -1)
        o_hg, n_hg, o_rw, n_rw = _mixers(
            z_hg, z_rw, lp["log_lb"], lp["one_m_lb"], lp["hg_norm"], s_hg[i], s_sh[i][:, None, :],
            s_rw[i], lp["mu"], lp["w0"], lp["w2p"], lp["a0"], lp["a2p"], lp["g2"], lp["k_k"],
            lp["k_a"], lp["r_k"], lp["ln_w"], lp["ln_b"])
        h, n_cv = _post(h, o_hg, o_rw, p[i], s_cv[i], lp["wo_hg"], lp["wo_rw"], lp["norm_ffn"],
                        lp["wg"], lp["wv"], lp["cw"], lp["cb"], lp["wo"], lp["norm_ple"],
                        lp["ple_gate"], lp["ple_proj"], lp["final_norm"], i == depth - 1)
        new_hg.append(n_hg)
        new_rw.append(n_rw)
        new_sh.append(z_rw[:, -1])
        new_cv.append(n_cv)
    return h, jnp.stack(new_hg), jnp.stack(new_rw), jnp.stack(new_sh), jnp.stack(new_cv)


def kernel(x_prompt, x_sample, state_hgrn, state_rwkv, state_shift, state_conv, p_prompt, p_sample,
           norm_mix, w_in, lb_raw, hg_norm, rw_mu, rw_w0, rw_w2, rw_a0, rw_a2, rw_g2, rw_k_k, rw_k_a,
           rw_r_k, rw_ln_w, rw_ln_b, w_out, norm_ffn, ffn_in, ffn_conv_w, ffn_conv_b, ffn_out,
           norm_ple, ple_gate, ple_proj, final_norm):
    w = dict(norm_mix=norm_mix, w_in=w_in, hg_norm=hg_norm, rw_mu=rw_mu, rw_w0=rw_w0, rw_w2=rw_w2,
             rw_a0=rw_a0, rw_a2=rw_a2, rw_g2=rw_g2, rw_k_k=rw_k_k, rw_k_a=rw_k_a,
             rw_r_k=rw_r_k.reshape(rw_r_k.shape[0], -1), rw_ln_w=rw_ln_w, rw_ln_b=rw_ln_b, w_out=w_out,
             norm_ffn=norm_ffn, ffn_in=ffn_in, ffn_conv_w=ffn_conv_w, ffn_conv_b=ffn_conv_b,
             ffn_out=ffn_out, norm_ple=norm_ple, ple_gate=ple_gate, ple_proj=ple_proj,
             final_norm=final_norm)
    depth = w_in.shape[0]
    hg_proj = 4 * hg_norm.shape[1]
    lb_all = jnp.cumsum(jax.nn.softmax(lb_raw.astype(F32), axis=0), axis=0)
    lb_all = lb_all - lb_all[:1]
    layers = [_layer_params(i, hg_proj, lb_all, w) for i in range(depth)]

    nb = x_prompt.shape[0]
    zeros_like_state = lambda s: jnp.zeros((depth, nb) + s.shape[2:], s.dtype)
    y_p, hg_p, rw_p, sh_p, cv_p = _trunk(x_prompt, p_prompt, zeros_like_state(state_hgrn),
                                         zeros_like_state(state_rwkv), zeros_like_state(state_shift),
                                         zeros_like_state(state_conv), layers)
    y_s, hg_s, rw_s, sh_s, cv_s = _trunk(x_sample, p_sample, state_hgrn, state_rwkv, state_shift,
                                         state_conv, layers)
    return (y_p, y_s, hg_p, rw_p, sh_p, cv_p, hg_s, rw_s, sh_s, cv_s)
```

```python
import functools

import jax
import jax.numpy as jnp
from jax import lax
from jax.experimental import pallas as pl
from jax.experimental.pallas import tpu as pltpu

F32 = jnp.float32
BF16 = jnp.bfloat16

NORM_EPS = 1e-6
RW_GN_EPS = 64e-5
HG_HEAD_DIM = 128
RW_HEAD_DIM = 64
RW_LORA_W = 64
CONV_W = 3
HG_BLOCK = 16
RW_CHUNK = 64
RW_SUB = 16
MIX_TILE = 128
FF_COLS = 256
POST_TILE = 512
LANES = 128
SUBLANES = 8
VMEM_LIMIT = 56 * 1024 * 1024

NN = ((1,), (0,))
NT = ((1,), (1,))
TN = ((0,), (0,))


def _dot(a, b, dims=NN):
    return lax.dot_general(a, b, (dims, ((), ())), preferred_element_type=F32)


def _dot1(a, b, dims=NN):
    return _dot(a.astype(BF16), b.astype(BF16), dims)


def _split2(x):
    hi = x.astype(BF16)
    lo = (x - hi.astype(F32)).astype(BF16)
    return hi, lo


def _dot_xl2(a_exact, x, dims=NN):
    hi, lo = _split2(x)
    return _dot(a_exact, hi, dims) + _dot(a_exact, lo, dims)


def _dot_xr2(x, b, dims=NN):
    hi, lo = _split2(x)
    b = b.astype(BF16)
    return _dot(hi, b, dims) + _dot(lo, b, dims)


def _head_sums(x, pair_ones):
    n, width = x.shape
    tiles = [x[:, p * LANES:(p + 1) * LANES] for p in range(width // LANES)]
    sums = _dot(jnp.concatenate(tiles, axis=0).astype(BF16), pair_ones)
    return jnp.concatenate([sums[p * n:(p + 1) * n] for p in range(width // LANES)], axis=1)


def _sigmoid(x):
    return 0.5 * jnp.tanh(0.5 * x) + 0.5


def _softplus(x):
    return jnp.maximum(x, 0.0) + jnp.log(1.0 + jnp.exp(-jnp.abs(x)))


def _rmsnorm(x, g):
    return x * lax.rsqrt(jnp.mean(x * x, axis=-1, keepdims=True) + NORM_EPS) * g


def _iota2(shape, axis):
    return lax.broadcasted_iota(jnp.int32, shape, axis)


def _const_spec(shape):
    zeros = (0,) * len(shape)
    return pl.BlockSpec(shape, lambda *_: zeros, pipeline_mode=pl.Buffered(1))


def _map(fn, *lists):
    return [fn(*xs) for xs in zip(*lists)]


def _inproj_kernel(x_ref, g_ref, whg_ref, wrw_ref, zhg_ref, zrw_ref):
    xn = _rmsnorm(x_ref[...], g_ref[...]).astype(BF16)
    zhg_ref[...] = _dot(xn, whg_ref[...])
    zrw_ref[...] = _dot(xn, wrw_ref[...])


def _in_proj(x2, g, w_hg, w_rw):
    n, d = x2.shape
    tm = min(512, n)
    assert n % tm == 0
    return pl.pallas_call(
        _inproj_kernel,
        grid=(n // tm,),
        in_specs=[pl.BlockSpec((tm, d), lambda i: (i, 0)),
                  _const_spec((1, d)),
                  _const_spec(w_hg.shape),
                  _const_spec(w_rw.shape)],
        out_specs=[pl.BlockSpec((tm, w_hg.shape[1]), lambda i: (i, 0)),
                   pl.BlockSpec((tm, w_rw.shape[1]), lambda i: (i, 0))],
        out_shape=[jax.ShapeDtypeStruct((n, w_hg.shape[1]), F32),
                   jax.ShapeDtypeStruct((n, w_rw.shape[1]), F32)],
        compiler_params=pltpu.CompilerParams(dimension_semantics=("parallel",),
                                             vmem_limit_bytes=VMEM_LIMIT),
        name="in_proj",
    )(x2, g, w_hg, w_rw)


HG_Q, HG_K, HG_BL, HG_V, HG_GATE = range(5)


def _hgrn_front(z_ref, loglb_ref, omlb_ref, gn_ref, pre_ref, p_ref, *, n_heads, tile):
    dk = HG_HEAD_DIM
    width = n_heads * dk
    blk = HG_BLOCK
    zq = z_ref[0, :, 0:width]
    zf = z_ref[0, :, width:2 * width]
    zg = z_ref[0, :, 3 * width:4 * width]
    pre_ref[HG_Q] = zq * _sigmoid(zq)
    pre_ref[HG_V] = z_ref[0, :, 2 * width:3 * width]
    pre_ref[HG_GATE] = gn_ref[...] * (zg * _sigmoid(zg))
    soft = jnp.log(1.0 + jnp.exp(-jnp.abs(zf)))
    log_sig_pos = jnp.minimum(zf, 0.0) - soft
    log_sig_neg = jnp.minimum(-zf, 0.0) - soft
    lo_term = loglb_ref[...] + log_sig_neg
    top = jnp.maximum(log_sig_pos, lo_term)
    logf = top + jnp.log(1.0 + jnp.exp(-jnp.abs(log_sig_pos - lo_term)))
    pre_ref[HG_K] = omlb_ref[...] * jnp.exp(log_sig_neg)
    row = _iota2((tile, tile), 0)
    col = _iota2((tile, tile), 1)
    tri = ((row // blk == col // blk) & (col <= row)).astype(BF16)
    pre_ref[HG_BL] = _dot_xl2(tri, logf)
    yield

    t_row = _iota2((blk, dk), 0)
    for i in range(tile // blk):
        rows = slice(i * blk, (i + 1) * blk)
        for h in range(n_heads):
            lanes = slice(h * dk, (h + 1) * dk)
            qb = pre_ref[HG_Q, rows, lanes]
            kb = pre_ref[HG_K, rows, lanes]
            bb = pre_ref[HG_BL, rows, lanes]
            p0 = (i * n_heads + h) * blk
            for s in range(blk):
                causal = t_row >= s
                ps = jnp.where(causal, qb * kb[s:s + 1, :] * jnp.exp(bb - bb[s:s + 1, :]), 0.0)
                p_ref[p0:p0 + blk, s * dk:(s + 1) * dk] = ps.astype(BF16)
            yield


def _hgrn_back(sel_ref, pre_ref, p_ref, a_ref, st_ref, o_ref, *, n_heads, tile):
    dk = HG_HEAD_DIM
    blk = HG_BLOCK
    n_blocks = tile // blk
    a_ref[...] = _dot(p_ref[...], sel_ref[...])
    yield

    block_rows = [slice(i * blk, (i + 1) * blk) for i in range(n_blocks)]
    e_end = [pre_ref[HG_BL, (i + 1) * blk - 1:(i + 1) * blk, :] for i in range(n_blocks)]
    start = [jnp.zeros_like(e_end[0])]
    for i in range(n_blocks):
        start.append(start[i] + e_end[i])
    qe = [pre_ref[HG_Q, r, :] * jnp.exp(pre_ref[HG_BL, r, :]) for r in block_rows]
    ke = [pre_ref[HG_K, r, :] * jnp.exp(e - pre_ref[HG_BL, r, :])
          for r, e in zip(block_rows, e_end)]
    vv = [pre_ref[HG_V, r, :] for r in block_rows]
    q_tile = jnp.concatenate([x * jnp.exp(s) for x, s in zip(qe, start)], axis=0)
    k_tile = jnp.concatenate([x * jnp.exp(start[n_blocks] - s) for x, s in zip(ke, start[1:])], axis=0)
    v_tile = pre_ref[HG_V]
    decay_tile = jnp.exp(start[n_blocks])
    yield

    heads = range(n_heads)
    lanes = [slice(h * dk, (h + 1) * dk) for h in heads]
    st = [st_ref[h] for h in heads]
    new_st = [st[h] * decay_tile[:, lanes[h]] + _dot1(v_tile[:, lanes[h]], k_tile[:, lanes[h]], TN)
              for h in heads]
    yield
    o_state = [_dot1(q_tile[:, lanes[h]], st[h], NT) for h in heads]
    o_blocks = [[o_state[h][r] for r in block_rows] for h in heads]
    yield
    span = n_blocks // 2
    while span >= 1:
        for lo in range(0, n_blocks, 2 * span):
            mid, hi = lo + span, lo + 2 * span
            q_r = jnp.concatenate([qe[i] if i == mid else qe[i] * jnp.exp(start[i] - start[mid])
                                   for i in range(mid, hi)], axis=0)
            k_l = jnp.concatenate([ke[j] if j == mid - 1 else ke[j] * jnp.exp(start[mid] - start[j + 1])
                                   for j in range(lo, mid)], axis=0)
            v_l = jnp.concatenate(vv[lo:mid], axis=0)
            scores = [_dot1(q_r[:, lanes[h]], k_l[:, lanes[h]], NT) for h in heads]
            upd = [_dot1(scores[h], v_l[:, lanes[h]]) for h in heads]
            for h in heads:
                for n, i in enumerate(range(mid, hi)):
                    o_blocks[h][i] = o_blocks[h][i] + upd[h][n * blk:(n + 1) * blk]
            yield
        span //= 2
    for i in range(n_blocks):
        for h in heads:
            p0 = (i * n_heads + h) * blk
            o_blocks[h][i] = o_blocks[h][i] + _dot1(a_ref[p0:p0 + blk, 0:blk], vv[i][:, lanes[h]])
        yield
    for h in heads:
        o = jnp.concatenate(o_blocks[h], axis=0)
        o = o * lax.rsqrt(jnp.mean(o * o, axis=-1, keepdims=True) + NORM_EPS)
        o_ref[0, :, lanes[h]] = (o * pre_ref[HG_GATE, :, lanes[h]]).astype(o_ref.dtype)
    for h in heads:
        st_ref[h] = new_st[h]


RW_KT, RW_RT, RW_V, RW_BH, RW_KH, RW_BL, RW_KL, RW_C, RW_RK, RW_G = range(10)


def _block_diag(y):
    lo = _iota2(y.shape, 1) < RW_HEAD_DIM
    y = y.astype(BF16)
    zero = jnp.zeros_like(y)
    return jnp.concatenate([jnp.where(lo, y, zero), jnp.where(lo, zero, y)], axis=0)


def _pair_dot(x, y):
    return _dot(x.astype(BF16), _block_diag(y))


def _pair_dot_nt(x, y):
    return _dot(x.astype(BF16), _block_diag(y), NT)


def _pair_dot2(x, y, z):
    return _dot(x.astype(BF16), jnp.concatenate([_block_diag(y), _block_diag(z)], axis=1))


def _pair_dot2_nt(x, y, z):
    return _dot(x.astype(BF16), jnp.concatenate([_block_diag(y), _block_diag(z)], axis=0), NT)


def _unit_lower_inverses(n_list, eye, same_sub):
    d = [jnp.where(same_sub, n, 0.0) for n in n_list]
    l_off = _map(lambda n, dd: n - dd, n_list, d)
    t_d = [eye - dd for dd in d]
    d_pow = d
    for _ in range(RW_SUB.bit_length() - 2):
        d_pow = _map(_pair_dot, d_pow, d_pow)
        yield
        t_d = _map(lambda t, dp: t + _pair_dot(t, dp), t_d, d_pow)
        yield
    p = _map(_pair_dot, t_d, l_off)
    yield
    p2 = _map(_pair_dot, p, p)
    yield
    t_p = _map(lambda pp, pp2: (eye - pp) + _pair_dot(eye - pp, pp2), p, p2)
    yield
    return _map(_pair_dot, t_p, t_d)


def _rwkv_front(z_ref, mu_ref, w0_ref, w2_ref, a0_ref, a2_ref, g2_ref, kk_ref, ka_ref, rk_ref,
                hones_ref, pre_ref, carry_ref, zs_ref, *, n_heads, chunk, tile):
    hd = RW_HEAD_DIM
    width = n_heads * hd
    z = z_ref[0]
    zs_ref[0:SUBLANES, :] = carry_ref[...]
    zs_ref[SUBLANES:SUBLANES + tile, :] = z
    carry_ref[...] = z[tile - SUBLANES:tile, :]
    z_prev = zs_ref[SUBLANES - 1:SUBLANES - 1 + tile, :]
    zm = z + (z_prev - z) * mu_ref[...]

    r = zm[:, 0:width]
    k = zm[:, width:2 * width]
    v = zm[:, 2 * width:3 * width]
    lora_in = zm[:, 3 * width:3 * width + 2 * RW_LORA_W]
    gd = zm[:, 3 * width + 2 * RW_LORA_W:]
    pre_ref[RW_V] = v
    pre_ref[RW_G] = _dot(_sigmoid(gd).astype(BF16), g2_ref[...])
    yield

    w_pre = w0_ref[...] + _dot_xr2(jnp.tanh(lora_in), w2_ref[...])
    w_raw = -_softplus(-w_pre) - 0.5
    logw = -jnp.exp(w_raw)
    a = _sigmoid(a0_ref[...] + _dot1(lora_in, a2_ref[...]))
    kk = k * kk_ref[...]
    kk = kk * jnp.minimum(lax.rsqrt(_head_sums(kk * kk, hones_ref[...])), 1e12)
    k2 = k * (1.0 + (a - 1.0) * ka_ref[...])
    beta = kk * a
    pre_ref[RW_RK] = r * k2 * rk_ref[...]
    yield

    trow = _iota2((tile, tile), 0)
    tcol = _iota2((tile, tile), 1)
    tri = ((trow // chunk == tcol // chunk) & (tcol <= trow)).astype(BF16)
    c = _dot_xl2(tri, logw)
    pre_ref[RW_C] = c
    e_neg = jnp.exp(-c)
    pre_ref[RW_KT] = kk * jnp.exp(c - logw)
    pre_ref[RW_RT] = r * jnp.exp(c)
    pre_ref[RW_BH] = beta * e_neg
    pre_ref[RW_KH] = k2 * e_neg
    yield
    for ci in range(tile // chunk):
        rows = slice(ci * chunk, (ci + 1) * chunk)
        e_end = jnp.exp(c[rows][chunk - 1:chunk, :] - c[rows])
        pre_ref[RW_BL, rows, :] = beta[rows] * e_end
        pre_ref[RW_KL, rows, :] = k2[rows] * e_end
    yield


def _rwkv_back(pre_ref, hones_ref, lnw_ref, lnb_ref, s_ref, y_ref, o_ref, *, n_heads, chunk, tile):
    hd = RW_HEAD_DIM
    n_chunks = tile // chunk
    pw = 2 * hd
    pairs = range(n_heads // 2)
    lanes = [slice(p * pw, (p + 1) * pw) for p in pairs]
    row = _iota2((chunk, pw), 0)
    lane = _iota2((chunk, pw), 1)
    lo_half = lane < hd
    col = jnp.where(lo_half, lane, lane - hd)
    incl2 = jnp.concatenate([col < row, col <= row], axis=0)
    incl4 = jnp.concatenate([incl2, incl2], axis=1)
    eye = (row == col).astype(F32)
    same_sub = (row // RW_SUB) == (col // RW_SUB)
    same_head = (_iota2((pw, pw), 0) // hd) == (_iota2((pw, pw), 1) // hd)

    chunk_rows = [slice(ci * chunk, (ci + 1) * chunk) for ci in range(n_chunks)]
    units = [(ci, p) for ci in range(n_chunks) for p in pairs]
    at = lambda name: [pre_ref[name, chunk_rows[ci], lanes[p]] for ci, p in units]
    kt, rt, vv, b_l, k_l = at(RW_KT), at(RW_RT), at(RW_V), at(RW_BL), at(RW_KL)
    g_last = [jnp.exp(pre_ref[RW_C, (ci + 1) * chunk - 1:(ci + 1) * chunk, lanes[p]]) for ci, p in units]
    lhs = _map(lambda x, y: jnp.concatenate([x, y], axis=0), kt, rt)
    g_bk = [jnp.where(incl4, _pair_dot2_nt(l, xb, xk), 0.0)
            for l, xb, xk in zip(lhs, at(RW_BH), at(RW_KH))]
    a_ab = [x[0:chunk, 0:pw] for x in g_bk]
    a_rb = [x[chunk:, 0:pw] for x in g_bk]
    yield
    gkv = [_pair_dot(x[:, pw:], v_) for x, v_ in zip(g_bk, vv)]
    x_loc = [x[0:chunk] for x in gkv]
    yield
    t_inv = yield from _unit_lower_inverses(a_ab, eye, same_sub)
    yield
    wu = _map(_pair_dot2, t_inv, kt, x_loc)
    w_t = [x[:, 0:pw] for x in wu]
    u_loc = [x[:, pw:] for x in wu]
    yield
    a_wu = _map(_pair_dot2, a_rb, w_t, u_loc)
    q_c = _map(lambda r_, x: r_ - x[:, 0:pw], rt, a_wu)
    y_loc = _map(lambda x, y: x[chunk:] - y[:, pw:], gkv, a_wu)
    yield
    m_off = _map(lambda w_, bl_: jnp.where(same_head, -_dot1(w_, bl_, TN), 0.0), w_t, b_l)
    yield
    c_full = _map(lambda v_, u_, kl_, bl_: _dot1(jnp.concatenate([v_, -u_], axis=0),
                                                 jnp.concatenate([kl_, bl_], axis=0), TN),
                  vv, u_loc, k_l, b_l)
    c_add = [jnp.where(lo_half, x[0:hd], x[hd:]) for x in c_full]
    yield
    state = [s_ref[p] for p in pairs]
    for idx, (ci, p) in enumerate(units):
        y_ref[chunk_rows[ci], lanes[p]] = _pair_dot_nt(q_c[idx], state[p]) + y_loc[idx]
        state[p] = state[p] * g_last[idx] + _dot1(state[p], m_off[idx]) + c_add[idx]
    for p in pairs:
        s_ref[p] = state[p]
    yield

    y = y_ref[...]
    head_ones = hones_ref[...]
    mean = _head_sums(y, head_ones) * (1.0 / hd)
    yc = y - mean
    var = _head_sums(yc * yc, head_ones) * (1.0 / hd)
    yn = yc * lax.rsqrt(var + RW_GN_EPS) * lnw_ref[...] + lnb_ref[...]
    bonus = _head_sums(pre_ref[RW_RK], head_ones) * pre_ref[RW_V]
    o_ref[0] = ((yn + bonus) * pre_ref[RW_G]).astype(o_ref.dtype)


_DONE = object()
MIX_STEPS = (3, 2, 1, 4)


def _mixer_kernel(zhg_ref, zrw_ref, loglb_ref, omlb_ref, gn_ref, sel_ref, s0hg_ref,
                  shift_ref, s0rw_ref, mu_ref, w0_ref, w2_ref, a0_ref, a2_ref, g2_ref,
                  kk_ref, ka_ref, rk_ref, lnw_ref, lnb_ref, hones_ref,
                  ohg_ref, sfin_hg_ref, orw_ref, sfin_rw_ref,
                  st_ref, s_ref, carry_ref, zs_ref, y_ref, a_ref,
                  hg_pre_a, hg_pre_b, p_a, p_b, rw_pre_a, rw_pre_b,
                  *, hg_heads, rw_heads, tile, pipelined):
    j = pl.program_id(1)
    hg = dict(n_heads=hg_heads, tile=tile)
    rw = dict(n_heads=rw_heads, chunk=RW_CHUNK, tile=tile)

    @pl.when(j == 0)
    def _():
        for h in range(hg_heads):
            st_ref[h] = s0hg_ref[0, h].T
        s_ref[...] = s0rw_ref[0]
        carry_ref[...] = jnp.broadcast_to(shift_ref[0], carry_ref.shape)
        if pipelined:
            hg_pre_b[...] = jnp.zeros(hg_pre_b.shape, hg_pre_b.dtype)
            p_b[...] = jnp.zeros(p_b.shape, p_b.dtype)
            rw_pre_b[...] = jnp.zeros(rw_pre_b.shape, rw_pre_b.dtype)

    def fronts(hg_w, p_w, rw_w):
        return [_rwkv_front(zrw_ref, mu_ref, w0_ref, w2_ref, a0_ref, a2_ref, g2_ref, kk_ref, ka_ref,
                            rk_ref, hones_ref, rw_w, carry_ref, zs_ref, **rw),
                _hgrn_front(zhg_ref, loglb_ref, omlb_ref, gn_ref, hg_w, p_w, **hg)]

    def backs(hg_r, p_r, rw_r):
        return [_rwkv_back(rw_r, hones_ref, lnw_ref, lnb_ref, s_ref, y_ref, orw_ref, **rw),
                _hgrn_back(sel_ref, hg_r, p_r, a_ref, st_ref, ohg_ref, **hg)]

    def run(bodies, steps):
        live = [True] * len(bodies)
        while any(live):
            for n, body in enumerate(bodies):
                for _ in range(steps[n]):
                    if live[n]:
                        live[n] = next(body, _DONE) is not _DONE

    if not pipelined:
        run(fronts(hg_pre_a, p_a, rw_pre_a), MIX_STEPS[2:])
        run(backs(hg_pre_a, p_a, rw_pre_a), MIX_STEPS[:2])
    else:
        @pl.when(j % 2 == 0)
        def _():
            run(backs(hg_pre_b, p_b, rw_pre_b) + fronts(hg_pre_a, p_a, rw_pre_a), MIX_STEPS)

        @pl.when(j % 2 == 1)
        def _():
            run(backs(hg_pre_a, p_a, rw_pre_a) + fronts(hg_pre_b, p_b, rw_pre_b), MIX_STEPS)

    @pl.when(j == pl.num_programs(1) - 1)
    def _():
        for h in range(hg_heads):
            sfin_hg_ref[0, h] = st_ref[h].T
        sfin_rw_ref[0] = s_ref[...]


def _mixers(z_hg, z_rw, log_lb, one_m_lb, hg_norm, s0_hg, shift_prev, s0_rw,
            mu, w0, w2p, a0, a2p, g2, k_k, k_a, r_k, ln_w, ln_b):
    bsz, t_len, hg_proj = z_hg.shape
    rw_proj = z_rw.shape[2]
    hg_width = hg_proj // 4
    hg_heads = hg_width // HG_HEAD_DIM
    rw_heads = s0_rw.shape[1]
    hd = RW_HEAD_DIM
    rw_width = rw_heads * hd
    tile = min(MIX_TILE, t_len)
    assert t_len % tile == 0 and tile % RW_CHUNK == 0 and tile % HG_BLOCK == 0 and rw_heads % 2 == 0
    n_tiles = t_len // tile
    n_pairs = rw_heads // 2
    s0_rw = s0_rw.reshape(bsz, n_pairs, 2, hd, hd).transpose(0, 1, 3, 2, 4).reshape(bsz, n_pairs, hd, 2 * hd)
    group = jnp.arange(HG_BLOCK * HG_HEAD_DIM, dtype=jnp.int32)[:, None] // HG_HEAD_DIM
    selector = (group == jnp.arange(LANES, dtype=jnp.int32)[None, :]).astype(BF16)
    head_of = jnp.arange(LANES, dtype=jnp.int32) // hd
    head_ones = (head_of[:, None] == head_of[None, :]).astype(BF16)

    pipelined = n_tiles > 1
    if pipelined:
        in_tile = lambda j: jnp.minimum(j, n_tiles - 1)
        out_tile = lambda j: jnp.maximum(j - 1, 0)
    else:
        in_tile = out_tile = lambda j: j
    tok_in = lambda w: pl.BlockSpec((1, tile, w), lambda b, j: (b, in_tile(j), 0))
    tok_out = lambda w: pl.BlockSpec((1, tile, w), lambda b, j: (b, out_tile(j), 0))
    per_row = lambda a: pl.BlockSpec((1,) + a.shape[1:], lambda b, j: (b,) + (0,) * (a.ndim - 1))
    consts_hg = [log_lb, one_m_lb, hg_norm, selector]
    consts_rw = [mu, w0, w2p, a0, a2p, g2, k_k, k_a, r_k, ln_w, ln_b, head_ones]
    kern = functools.partial(_mixer_kernel, hg_heads=hg_heads, rw_heads=rw_heads, tile=tile,
                             pipelined=pipelined)
    o_hg, n_hg, o_rw, n_rw = pl.pallas_call(
        kern,
        grid=(bsz, n_tiles + 1 if pipelined else n_tiles),
        in_specs=[tok_in(hg_proj), tok_in(rw_proj)]
                 + [_const_spec(a.shape) for a in consts_hg] + [per_row(s0_hg)]
                 + [per_row(shift_prev), per_row(s0_rw)] + [_const_spec(a.shape) for a in consts_rw],
        out_specs=[tok_out(hg_width), per_row(s0_hg), tok_out(rw_width), per_row(s0_rw)],
        out_shape=[jax.ShapeDtypeStruct((bsz, t_len, hg_width), BF16),
                   jax.ShapeDtypeStruct(s0_hg.shape, F32),
                   jax.ShapeDtypeStruct((bsz, t_len, rw_width), BF16),
                   jax.ShapeDtypeStruct(s0_rw.shape, F32)],
        scratch_shapes=[pltpu.VMEM((hg_heads, HG_HEAD_DIM, HG_HEAD_DIM), F32),
                        pltpu.VMEM((n_pairs, hd, 2 * hd), F32),
                        pltpu.VMEM((SUBLANES, rw_proj), F32),
                        pltpu.VMEM((SUBLANES + tile, rw_proj), F32),
                        pltpu.VMEM((tile, rw_width), F32),
                        pltpu.VMEM((tile * hg_heads, LANES), F32)]
                       + [pltpu.VMEM((5, tile, hg_width), F32)] * 2
                       + [pltpu.VMEM((tile * hg_heads, HG_BLOCK * HG_HEAD_DIM), BF16)] * 2
                       + [pltpu.VMEM((10, tile, rw_width), F32)] * 2,
        compiler_params=pltpu.CompilerParams(dimension_semantics=("parallel", "arbitrary"),
                                             vmem_limit_bytes=VMEM_LIMIT),
        name="mixers",
    )(z_hg, z_rw, *consts_hg, s0_hg, shift_prev, s0_rw, *consts_rw)
    n_rw = n_rw.reshape(bsz, n_pairs, hd, 2, hd).transpose(0, 1, 3, 2, 4).reshape(bsz, rw_heads, hd, hd)
    return o_hg, n_hg, o_rw, n_rw


def _gelu_exact(x):
    return 0.5 * x * (1.0 + lax.erf(x * (2.0 ** -0.5)))


def _post_kernel(h_ref, ohg_ref, orw_ref, p_ref, cvprev_ref,
                 wohg_ref, worw_ref, nffn_ref, wg_ref, wv_ref, cw_ref, cb_ref, wo_ref,
                 nple_ref, pgate_ref, pproj_ref, fnorm_ref,
                 hout_ref, cvnew_ref, carry_ref, ugs_ref, *, tm, n_col_blocks, apply_final_norm):
    t_idx = pl.program_id(1)
    cols = FF_COLS

    @pl.when(t_idx == 0)
    def _():
        carry_ref[...] = jnp.zeros(carry_ref.shape, carry_ref.dtype)
        for j in range(n_col_blocks):
            carry_ref[j, SUBLANES - (CONV_W - 1):SUBLANES, :] = cvprev_ref[0, :, j * cols:(j + 1) * cols]

    h1 = (h_ref[0] + _dot(ohg_ref[0].astype(BF16), wohg_ref[...])
          + _dot(orw_ref[0].astype(BF16), worw_ref[...]))
    xn = _rmsnorm(h1, nffn_ref[...]).astype(BF16)
    acc = jnp.zeros_like(h1)
    up_next = (_dot(xn, wg_ref[0]), _dot(xn, wv_ref[0]))
    for j in range(n_col_blocks):
        ug, uv = up_next
        if j + 1 < n_col_blocks:
            up_next = (_dot(xn, wg_ref[j + 1]), _dot(xn, wv_ref[j + 1]))
        ugs_ref[0:SUBLANES, :] = carry_ref[j]
        ugs_ref[SUBLANES:SUBLANES + tm, :] = ug
        carry_ref[j] = ug[tm - SUBLANES:tm, :]
        cvnew_ref[0, :, j * cols:(j + 1) * cols] = ug[tm - (CONV_W - 1):tm, :]
        cw = cw_ref[j]
        c = (cb_ref[j] + cw[0:1, :] * ugs_ref[SUBLANES - 2:SUBLANES - 2 + tm, :]
             + cw[1:2, :] * ugs_ref[SUBLANES - 1:SUBLANES - 1 + tm, :] + cw[2:3, :] * ug)
        hid = (_gelu_exact(c) * uv).astype(BF16)
        acc = acc + _dot(hid, wo_ref[j])
    h2 = h1 + acc
    gate = _sigmoid(_dot(_rmsnorm(h2, nple_ref[...]).astype(BF16), pgate_ref[...]))
    h3 = h2 + gate * _dot(p_ref[0].astype(BF16), pproj_ref[...])
    if apply_final_norm:
        h3 = _rmsnorm(h3, fnorm_ref[...])
    hout_ref[0] = h3


def _post(h, o_hg, o_rw, p, cv_prev, wo_hg, wo_rw, norm_ffn, wg, wv, cw, cb, wo, norm_ple,
          ple_gate, ple_proj, final_norm, apply_final_norm):
    bsz, t_len, d = h.shape
    tm = min(POST_TILE, t_len)
    assert t_len % tm == 0 and tm >= SUBLANES
    n_col_blocks = wg.shape[0]
    d_ff = n_col_blocks * FF_COLS
    kern = functools.partial(_post_kernel, tm=tm, n_col_blocks=n_col_blocks,
                             apply_final_norm=apply_final_norm)
    tok = lambda w: pl.BlockSpec((1, tm, w), lambda b, t: (b, t, 0))
    cv_spec = pl.BlockSpec((1, CONV_W - 1, d_ff), lambda b, t: (b, 0, 0))
    consts = [wo_hg, wo_rw, norm_ffn, wg, wv, cw, cb, wo, norm_ple, ple_gate, ple_proj, final_norm]
    return pl.pallas_call(
        kern,
        grid=(bsz, t_len // tm),
        in_specs=[tok(d), tok(o_hg.shape[2]), tok(o_rw.shape[2]), tok(p.shape[2]), cv_spec]
                 + [_const_spec(a.shape) for a in consts],
        out_specs=[tok(d), cv_spec],
        out_shape=[jax.ShapeDtypeStruct(h.shape, F32), jax.ShapeDtypeStruct(cv_prev.shape, F32)],
        scratch_shapes=[pltpu.VMEM((n_col_blocks, SUBLANES, FF_COLS), F32),
                        pltpu.VMEM((SUBLANES + tm, FF_COLS), F32)],
        compiler_params=pltpu.CompilerParams(dimension_semantics=("parallel", "arbitrary"),
                                             vmem_limit_bytes=VMEM_LIMIT),
        name="post",
    )(h, o_hg, o_rw, p, cv_prev, *consts)


def _layer_params(i, hg_proj, lb_all, w):
    d_ff = w["ffn_conv_b"].shape[1]
    n_cb = d_ff // FF_COLS
    rw_width = w["rw_w0"].shape[1]
    hg_width = w["hg_norm"].shape[1]
    row = lambda a: a.reshape(1, -1)
    lb = jnp.maximum(lb_all[i], 0.0)
    zeros_lora = jnp.zeros((RW_LORA_W, rw_width), F32)
    ffn_in = w["ffn_in"][i].astype(BF16)
    d_model = ffn_in.shape[0]
    blocked = lambda a: a.reshape(d_model, n_cb, FF_COLS).transpose(1, 0, 2)
    return dict(
        norm_mix=row(w["norm_mix"][i]),
        w_in_hg=w["w_in"][i][:, :hg_proj].astype(BF16),
        w_in_rw=w["w_in"][i][:, hg_proj:].astype(BF16),
        log_lb=row(jnp.log(lb)), one_m_lb=row(1.0 - lb), hg_norm=row(w["hg_norm"][i]),
        mu=row(w["rw_mu"][i]), w0=row(w["rw_w0"][i]),
        w2p=jnp.concatenate([w["rw_w2"][i], zeros_lora], axis=0).astype(BF16),
        a0=row(w["rw_a0"][i]),
        a2p=jnp.concatenate([zeros_lora, w["rw_a2"][i]], axis=0).astype(BF16),
        g2=w["rw_g2"][i].astype(BF16),
        k_k=row(w["rw_k_k"][i]), k_a=row(w["rw_k_a"][i]), r_k=row(w["rw_r_k"][i]),
        ln_w=row(w["rw_ln_w"][i]), ln_b=row(w["rw_ln_b"][i]),
        wo_hg=w["w_out"][i][:hg_width].astype(BF16), wo_rw=w["w_out"][i][hg_width:].astype(BF16),
        norm_ffn=row(w["norm_ffn"][i]),
        wg=blocked(ffn_in[:, :d_ff]), wv=blocked(ffn_in[:, d_ff:]),
        cw=w["ffn_conv_w"][i].reshape(CONV_W, n_cb, FF_COLS).transpose(1, 0, 2),
        cb=w["ffn_conv_b"][i].reshape(n_cb, 1, FF_COLS),
        wo=w["ffn_out"][i].astype(BF16).reshape(n_cb, FF_COLS, d_model),
        norm_ple=row(w["norm_ple"][i]),
        ple_gate=w["ple_gate"][i].astype(BF16), ple_proj=w["ple_proj"][i].astype(BF16),
        final_norm=row(w["final_norm"]),
    )


def _trunk(x, p, s_hg, s_rw, s_sh, s_cv, layers):
    bsz, t_len, d = x.shape
    depth = len(layers)
    h = x
    new_hg, new_rw, new_sh, new_cv = [], [], [], []
    for i, lp in enumerate(layers):
        z_hg, z_rw = _in_proj(h.reshape(bsz * t_len, d), lp["norm_mix"], lp["w_in_hg"], lp["w_in_rw"])
        z_hg = z_hg.reshape(bsz, t_len, -1)
        z_rw = z_rw.reshape(bsz, t_len, -1)
        o_hg, n_hg, o_rw, n_rw = _mixers(
            z_hg, z_rw, lp["log_lb"], lp["one_m_lb"], lp["hg_norm"], s_hg[i], s_sh[i][:, None, :],
            s_rw[i], lp["mu"], lp["w0"], lp["w2p"], lp["a0"], lp["a2p"], lp["g2"], lp["k_k"],
            lp["k_a"], lp["r_k"], lp["ln_w"], lp["ln_b"])
        h, n_cv = _post(h, o_hg, o_rw, p[i], s_cv[i], lp["wo_hg"], lp["wo_rw"], lp["norm_ffn"],
                        lp["wg"], lp["wv"], lp["cw"], lp["cb"], lp["wo"], lp["norm_ple"],
                        lp["ple_gate"], lp["ple_proj"], lp["final_norm"], i == depth - 1)
        new_hg.append(n_hg)
        new_rw.append(n_rw)
        new_sh.append(z_rw[:, -1])
        new_cv.append(n_cv)
    return h, jnp.stack(new_hg), jnp.stack(new_rw), jnp.stack(new_sh), jnp.stack(new_cv)


def kernel(x_prompt, x_sample, state_hgrn, state_rwkv, state_shift, state_conv, p_prompt, p_sample,
           norm_mix, w_in, lb_raw, hg_norm, rw_mu, rw_w0, rw_w2, rw_a0, rw_a2, rw_g2, rw_k_k, rw_k_a,
           rw_r_k, rw_ln_w, rw_ln_b, w_out, norm_ffn, ffn_in, ffn_conv_w, ffn_conv_b, ffn_out,
           norm_ple, ple_gate, ple_proj, final_norm):
    w = dict(norm_mix=norm_mix, w_in=w_in, hg_norm=hg_norm, rw_mu=rw_mu, rw_w0=rw_w0, rw_w2=rw_w2,
             rw_a0=rw_a0, rw_a2=rw_a2, rw_g2=rw_g2, rw_k_k=rw_k_k, rw_k_a=rw_k_a,
             rw_r_k=rw_r_k.reshape(rw_r_k.shape[0], -1), rw_ln_w=rw_ln_w, rw_ln_b=rw_ln_b, w_out=w_out,
             norm_ffn=norm_ffn, ffn_in=ffn_in, ffn_conv_w=ffn_conv_w, ffn_conv_b=ffn_conv_b,
             ffn_out=ffn_out, norm_ple=norm_ple, ple_gate=ple_gate, ple_proj=ple_proj,
             final_norm=final_norm)
    depth = w_in.shape[0]
    hg_proj = 4 * hg_norm.shape[1]
    lb_all = jnp.cumsum(jax.nn.softmax(lb_raw.astype(F32), axis=0), axis=0)
    lb_all = lb_all - lb_all[:1]
    layers = [_layer_params(i, hg_proj, lb_all, w) for i in range(depth)]

    nb = x_prompt.shape[0]
    zeros_like_state = lambda s: jnp.zeros((depth, nb) + s.shape[2:], s.dtype)
    y_p, hg_p, rw_p, sh_p, cv_p = _trunk(x_prompt, p_prompt, zeros_like_state(state_hgrn),
                                         zeros_like_state(state_rwkv), zeros_like_state(state_shift),
                                         zeros_like_state(state_conv), layers)
    y_s, hg_s, rw_s, sh_s, cv_s = _trunk(x_sample, p_sample, state_hgrn, state_rwkv, state_shift,
                                         state_conv, layers)
    return (y_p, y_s, hg_p, rw_p, sh_p, cv_p, hg_s, rw_s, sh_s, cv_s)
```

```python
import functools

import jax
import jax.numpy as jnp
from jax import lax
from jax.experimental import pallas as pl
from jax.experimental.pallas import tpu as pltpu

F32 = jnp.float32
BF16 = jnp.bfloat16

NORM_EPS = 1e-6
RW_GN_EPS = 64e-5
HG_HEAD_DIM = 128
RW_HEAD_DIM = 64
RW_LORA_W = 64
CONV_W = 3
HG_BLOCK = 16
RW_CHUNK = 64
RW_SUB = 16
MIX_TILE = 256
FF_COLS = 256
POST_TILE = 512
LANES = 128
SUBLANES = 8
VMEM_LIMIT = 56 * 1024 * 1024

NN = ((1,), (0,))
NT = ((1,), (1,))
TN = ((0,), (0,))


def _dot(a, b, dims=NN):
    return lax.dot_general(a, b, (dims, ((), ())), preferred_element_type=F32)


def _dot1(a, b, dims=NN):
    return _dot(a.astype(BF16), b.astype(BF16), dims)


def _split2(x):
    hi = x.astype(BF16)
    lo = (x - hi.astype(F32)).astype(BF16)
    return hi, lo


def _dot_xl2(a_exact, x, dims=NN):
    hi, lo = _split2(x)
    return _dot(a_exact, hi, dims) + _dot(a_exact, lo, dims)


def _dot_xr2(x, b, dims=NN):
    hi, lo = _split2(x)
    b = b.astype(BF16)
    return _dot(hi, b, dims) + _dot(lo, b, dims)


def _head_sums(x, pair_ones):
    n, width = x.shape
    tiles = [x[:, p * LANES:(p + 1) * LANES] for p in range(width // LANES)]
    sums = _dot(jnp.concatenate(tiles, axis=0).astype(BF16), pair_ones)
    return jnp.concatenate([sums[p * n:(p + 1) * n] for p in range(width // LANES)], axis=1)


def _sigmoid(x):
    return 0.5 * jnp.tanh(0.5 * x) + 0.5


def _softplus(x):
    return jnp.maximum(x, 0.0) + jnp.log(1.0 + jnp.exp(-jnp.abs(x)))


def _rmsnorm(x, g):
    return x * lax.rsqrt(jnp.mean(x * x, axis=-1, keepdims=True) + NORM_EPS) * g


def _iota2(shape, axis):
    return lax.broadcasted_iota(jnp.int32, shape, axis)


def _const_spec(shape):
    zeros = (0,) * len(shape)
    return pl.BlockSpec(shape, lambda *_: zeros, pipeline_mode=pl.Buffered(1))


def _map(fn, *lists):
    return [fn(*xs) for xs in zip(*lists)]


def _inproj_kernel(x_ref, g_ref, w_ref, zhg_ref, zrw_ref):
    hg_proj = zhg_ref.shape[1]
    xn = _rmsnorm(x_ref[...], g_ref[...]).astype(BF16)
    zhg_ref[...] = _dot(xn, w_ref[:, 0:hg_proj])
    zrw_ref[...] = _dot(xn, w_ref[:, hg_proj:])


def _in_proj(x2, g, w_in, hg_proj):
    n, d = x2.shape
    tm = min(512, n)
    assert n % tm == 0 and hg_proj % LANES == 0
    rw_proj = w_in.shape[1] - hg_proj
    return pl.pallas_call(
        _inproj_kernel,
        grid=(n // tm,),
        in_specs=[pl.BlockSpec((tm, d), lambda i: (i, 0)),
                  _const_spec((1, d)),
                  _const_spec(w_in.shape)],
        out_specs=[pl.BlockSpec((tm, hg_proj), lambda i: (i, 0)),
                   pl.BlockSpec((tm, rw_proj), lambda i: (i, 0))],
        out_shape=[jax.ShapeDtypeStruct((n, hg_proj), F32),
                   jax.ShapeDtypeStruct((n, rw_proj), F32)],
        compiler_params=pltpu.CompilerParams(dimension_semantics=("parallel",),
                                             vmem_limit_bytes=VMEM_LIMIT),
        name="in_proj",
    )(x2, g, w_in)


HG_Q, HG_K, HG_BL, HG_V, HG_GATE = range(5)


def _hgrn_front(z_ref, loglb_ref, omlb_ref, gn_ref, pre_ref, p_ref, *, n_heads, tile):
    dk = HG_HEAD_DIM
    width = n_heads * dk
    blk = HG_BLOCK
    zq = z_ref[0, :, 0:width]
    zf = z_ref[0, :, width:2 * width]
    zg = z_ref[0, :, 3 * width:4 * width]
    pre_ref[HG_Q] = zq * _sigmoid(zq)
    pre_ref[HG_V] = z_ref[0, :, 2 * width:3 * width]
    pre_ref[HG_GATE] = gn_ref[...] * (zg * _sigmoid(zg))
    soft = jnp.log(1.0 + jnp.exp(-jnp.abs(zf)))
    log_sig_pos = jnp.minimum(zf, 0.0) - soft
    log_sig_neg = jnp.minimum(-zf, 0.0) - soft
    lo_term = loglb_ref[...] + log_sig_neg
    top = jnp.maximum(log_sig_pos, lo_term)
    logf = top + jnp.log(1.0 + jnp.exp(-jnp.abs(log_sig_pos - lo_term)))
    pre_ref[HG_K] = omlb_ref[...] * jnp.exp(log_sig_neg)
    row = _iota2((tile, tile), 0)
    col = _iota2((tile, tile), 1)
    tri = ((row // blk == col // blk) & (col <= row)).astype(BF16)
    pre_ref[HG_BL] = _dot_xl2(tri, logf)
    yield

    t_row = _iota2((blk, dk), 0)
    for i in range(tile // blk):
        rows = slice(i * blk, (i + 1) * blk)
        for h in range(n_heads):
            lanes = slice(h * dk, (h + 1) * dk)
            qb = pre_ref[HG_Q, rows, lanes]
            kb = pre_ref[HG_K, rows, lanes]
            bb = pre_ref[HG_BL, rows, lanes]
            p0 = (i * n_heads + h) * blk
            for s in range(blk):
                causal = t_row >= s
                ps = jnp.where(causal, qb * kb[s:s + 1, :] * jnp.exp(bb - bb[s:s + 1, :]), 0.0)
                p_ref[p0:p0 + blk, s * dk:(s + 1) * dk] = ps.astype(BF16)
            yield


def _hgrn_back(sel_ref, pre_ref, p_ref, a_ref, st_ref, o_ref, *, n_heads, tile):
    dk = HG_HEAD_DIM
    blk = HG_BLOCK
    n_blocks = tile // blk
    a_ref[...] = _dot(p_ref[...], sel_ref[...])
    yield

    block_rows = [slice(i * blk, (i + 1) * blk) for i in range(n_blocks)]
    e_end = [pre_ref[HG_BL, (i + 1) * blk - 1:(i + 1) * blk, :] for i in range(n_blocks)]
    start = [jnp.zeros_like(e_end[0])]
    for i in range(n_blocks):
        start.append(start[i] + e_end[i])
    qe = [pre_ref[HG_Q, r, :] * jnp.exp(pre_ref[HG_BL, r, :]) for r in block_rows]
    ke = [pre_ref[HG_K, r, :] * jnp.exp(e - pre_ref[HG_BL, r, :])
          for r, e in zip(block_rows, e_end)]
    vv = [pre_ref[HG_V, r, :] for r in block_rows]
    q_tile = jnp.concatenate([x * jnp.exp(s) for x, s in zip(qe, start)], axis=0)
    k_tile = jnp.concatenate([x * jnp.exp(start[n_blocks] - s) for x, s in zip(ke, start[1:])], axis=0)
    v_tile = pre_ref[HG_V]
    decay_tile = jnp.exp(start[n_blocks])
    yield

    heads = range(n_heads)
    lanes = [slice(h * dk, (h + 1) * dk) for h in heads]
    st = [st_ref[h] for h in heads]
    new_st = [st[h] * decay_tile[:, lanes[h]] + _dot1(v_tile[:, lanes[h]], k_tile[:, lanes[h]], TN)
              for h in heads]
    yield
    o_state = [_dot1(q_tile[:, lanes[h]], st[h], NT) for h in heads]
    o_blocks = [[o_state[h][r] for r in block_rows] for h in heads]
    yield
    span = n_blocks // 2
    while span >= 1:
        for lo in range(0, n_blocks, 2 * span):
            mid, hi = lo + span, lo + 2 * span
            q_r = jnp.concatenate([qe[i] if i == mid else qe[i] * jnp.exp(start[i] - start[mid])
                                   for i in range(mid, hi)], axis=0)
            k_l = jnp.concatenate([ke[j] if j == mid - 1 else ke[j] * jnp.exp(start[mid] - start[j + 1])
                                   for j in range(lo, mid)], axis=0)
            v_l = jnp.concatenate(vv[lo:mid], axis=0)
            scores = [_dot1(q_r[:, lanes[h]], k_l[:, lanes[h]], NT) for h in heads]
            upd = [_dot1(scores[h], v_l[:, lanes[h]]) for h in heads]
            for h in heads:
                for n, i in enumerate(range(mid, hi)):
                    o_blocks[h][i] = o_blocks[h][i] + upd[h][n * blk:(n + 1) * blk]
            yield
        span //= 2
    for i in range(n_blocks):
        for h in heads:
            p0 = (i * n_heads + h) * blk
            o_blocks[h][i] = o_blocks[h][i] + _dot1(a_ref[p0:p0 + blk, 0:blk], vv[i][:, lanes[h]])
        yield
    for h in heads:
        o = jnp.concatenate(o_blocks[h], axis=0)
        o = o * lax.rsqrt(jnp.mean(o * o, axis=-1, keepdims=True) + NORM_EPS)
        o_ref[0, :, lanes[h]] = (o * pre_ref[HG_GATE, :, lanes[h]]).astype(o_ref.dtype)
    for h in heads:
        st_ref[h] = new_st[h]


RW_KT, RW_RT, RW_V, RW_BH, RW_KH, RW_BL, RW_KL, RW_C, RW_RK, RW_G = range(10)


def _block_diag(y):
    lo = _iota2(y.shape, 1) < RW_HEAD_DIM
    y = y.astype(BF16)
    zero = jnp.zeros_like(y)
    return jnp.concatenate([jnp.where(lo, y, zero), jnp.where(lo, zero, y)], axis=0)


def _pair_dot(x, y):
    return _dot(x.astype(BF16), _block_diag(y))


def _pair_dot_nt(x, y):
    return _dot(x.astype(BF16), _block_diag(y), NT)


def _pair_dot2(x, y, z):
    return _dot(x.astype(BF16), jnp.concatenate([_block_diag(y), _block_diag(z)], axis=1))


def _pair_dot2_nt(x, y, z):
    return _dot(x.astype(BF16), jnp.concatenate([_block_diag(y), _block_diag(z)], axis=0), NT)


def _unit_lower_inverses(n_list, eye, same_sub):
    d = [jnp.where(same_sub, n, 0.0) for n in n_list]
    l_off = _map(lambda n, dd: n - dd, n_list, d)
    t_d = [eye - dd for dd in d]
    n = n_list[0].shape[0]
    d_pow = _map(_pair_dot, d, d)
    yield
    for _ in range(RW_SUB.bit_length() - 3):
        both = _map(lambda dp, t: _pair_dot(jnp.concatenate([dp, t], axis=0), dp), d_pow, t_d)
        t_d = _map(lambda t, b: t + b[n:], t_d, both)
        d_pow = [b[0:n] for b in both]
        yield
    t_d = _map(lambda t, dp: t + _pair_dot(t, dp), t_d, d_pow)
    yield
    p = _map(_pair_dot, t_d, l_off)
    yield
    p2 = _map(_pair_dot, p, p)
    yield
    t_p = _map(lambda pp, pp2: (eye - pp) + _pair_dot(eye - pp, pp2), p, p2)
    yield
    return _map(_pair_dot, t_p, t_d)


def _rwkv_front(z_ref, mu_ref, w0_ref, w2_ref, a0_ref, a2_ref, g2_ref, kk_ref, ka_ref, rk_ref,
                hones_ref, pre_ref, carry_ref, zs_ref, *, n_heads, chunk, tile):
    hd = RW_HEAD_DIM
    width = n_heads * hd
    z = z_ref[0]
    zs_ref[0:SUBLANES, :] = carry_ref[...]
    zs_ref[SUBLANES:SUBLANES + tile, :] = z
    carry_ref[...] = z[tile - SUBLANES:tile, :]
    z_prev = zs_ref[SUBLANES - 1:SUBLANES - 1 + tile, :]
    zm = z + (z_prev - z) * mu_ref[...]

    r = zm[:, 0:width]
    k = zm[:, width:2 * width]
    v = zm[:, 2 * width:3 * width]
    lora_in = zm[:, 3 * width:3 * width + 2 * RW_LORA_W]
    gd = zm[:, 3 * width + 2 * RW_LORA_W:]
    pre_ref[RW_V] = v
    pre_ref[RW_G] = _dot(_sigmoid(gd).astype(BF16), g2_ref[...])
    yield

    w_pre = w0_ref[...] + _dot_xr2(jnp.tanh(lora_in), w2_ref[...])
    w_raw = -_softplus(-w_pre) - 0.5
    logw = -jnp.exp(w_raw)
    a = _sigmoid(a0_ref[...] + _dot1(lora_in, a2_ref[...]))
    kk = k * kk_ref[...]
    kk = kk * jnp.minimum(lax.rsqrt(_head_sums(kk * kk, hones_ref[...])), 1e12)
    k2 = k * (1.0 + (a - 1.0) * ka_ref[...])
    beta = kk * a
    pre_ref[RW_RK] = r * k2 * rk_ref[...]
    yield

    trow = _iota2((tile, tile), 0)
    tcol = _iota2((tile, tile), 1)
    tri = ((trow // chunk == tcol // chunk) & (tcol <= trow)).astype(BF16)
    c = _dot_xl2(tri, logw)
    pre_ref[RW_C] = c
    e_neg = jnp.exp(-c)
    pre_ref[RW_KT] = kk * jnp.exp(c - logw)
    pre_ref[RW_RT] = r * jnp.exp(c)
    pre_ref[RW_BH] = beta * e_neg
    pre_ref[RW_KH] = k2 * e_neg
    yield
    for ci in range(tile // chunk):
        rows = slice(ci * chunk, (ci + 1) * chunk)
        e_end = jnp.exp(c[rows][chunk - 1:chunk, :] - c[rows])
        pre_ref[RW_BL, rows, :] = beta[rows] * e_end
        pre_ref[RW_KL, rows, :] = k2[rows] * e_end
    yield


def _rwkv_back(pre_ref, hones_ref, lnw_ref, lnb_ref, s_ref, y_ref, o_ref, *, n_heads, chunk, tile):
    hd = RW_HEAD_DIM
    n_chunks = tile // chunk
    pw = 2 * hd
    pairs = range(n_heads // 2)
    lanes = [slice(p * pw, (p + 1) * pw) for p in pairs]
    row = _iota2((chunk, pw), 0)
    lane = _iota2((chunk, pw), 1)
    lo_half = lane < hd
    col = jnp.where(lo_half, lane, lane - hd)
    incl2 = jnp.concatenate([col < row, col <= row], axis=0)
    incl4 = jnp.concatenate([incl2, incl2], axis=1)
    eye = (row == col).astype(F32)
    same_sub = (row // RW_SUB) == (col // RW_SUB)
    same_head = (_iota2((pw, pw), 0) // hd) == (_iota2((pw, pw), 1) // hd)

    chunk_rows = [slice(ci * chunk, (ci + 1) * chunk) for ci in range(n_chunks)]
    units = [(ci, p) for ci in range(n_chunks) for p in pairs]
    at = lambda name: [pre_ref[name, chunk_rows[ci], lanes[p]] for ci, p in units]
    kt, rt, vv, b_l, k_l = at(RW_KT), at(RW_RT), at(RW_V), at(RW_BL), at(RW_KL)
    g_last = [jnp.exp(pre_ref[RW_C, (ci + 1) * chunk - 1:(ci + 1) * chunk, lanes[p]]) for ci, p in units]
    lhs = _map(lambda x, y: jnp.concatenate([x, y], axis=0), kt, rt)
    g_bk = [jnp.where(incl4, _pair_dot2_nt(l, xb, xk), 0.0)
            for l, xb, xk in zip(lhs, at(RW_BH), at(RW_KH))]
    a_ab = [x[0:chunk, 0:pw] for x in g_bk]
    a_rb = [x[chunk:, 0:pw] for x in g_bk]
    yield
    gkv = [_pair_dot(x[:, pw:], v_) for x, v_ in zip(g_bk, vv)]
    x_loc = [x[0:chunk] for x in gkv]
    yield
    t_inv = yield from _unit_lower_inverses(a_ab, eye, same_sub)
    yield
    wu = _map(_pair_dot2, t_inv, kt, x_loc)
    w_t = [x[:, 0:pw] for x in wu]
    u_loc = [x[:, pw:] for x in wu]
    yield
    a_wu = _map(_pair_dot2, a_rb, w_t, u_loc)
    q_c = _map(lambda r_, x: r_ - x[:, 0:pw], rt, a_wu)
    y_loc = _map(lambda x, y: x[chunk:] - y[:, pw:], gkv, a_wu)
    yield
    m_off = _map(lambda w_, bl_: jnp.where(same_head, -_dot1(w_, bl_, TN), 0.0), w_t, b_l)
    yield
    c_full = _map(lambda v_, u_, kl_, bl_: _dot1(jnp.concatenate([v_, -u_], axis=0),
                                                 jnp.concatenate([kl_, bl_], axis=0), TN),
                  vv, u_loc, k_l, b_l)
    c_add = [jnp.where(lo_half, x[0:hd], x[hd:]) for x in c_full]
    yield
    state = [s_ref[p] for p in pairs]
    for idx, (ci, p) in enumerate(units):
        y_ref[chunk_rows[ci], lanes[p]] = _pair_dot_nt(q_c[idx], state[p]) + y_loc[idx]
        state[p] = state[p] * g_last[idx] + _dot1(state[p], m_off[idx]) + c_add[idx]
    for p in pairs:
        s_ref[p] = state[p]
    yield

    y = y_ref[...]
    head_ones = hones_ref[...]
    mean = _head_sums(y, head_ones) * (1.0 / hd)
    yc = y - mean
    var = _head_sums(yc * yc, head_ones) * (1.0 / hd)
    yn = yc * lax.rsqrt(var + RW_GN_EPS) * lnw_ref[...] + lnb_ref[...]
    bonus = _head_sums(pre_ref[RW_RK], head_ones) * pre_ref[RW_V]
    o_ref[0] = ((yn + bonus) * pre_ref[RW_G]).astype(o_ref.dtype)


_DONE = object()
MIX_STEPS = (3, 2, 1, 4)


def _mixer_kernel(zhg_ref, zrw_ref, loglb_ref, omlb_ref, gn_ref, sel_ref, s0hg_ref,
                  shift_ref, s0rw_ref, mu_ref, w0_ref, w2_ref, a0_ref, a2_ref, g2_ref,
                  kk_ref, ka_ref, rk_ref, lnw_ref, lnb_ref, hones_ref,
                  ohg_ref, sfin_hg_ref, orw_ref, sfin_rw_ref,
                  st_ref, s_ref, carry_ref, zs_ref, y_ref, a_ref,
                  hg_pre_a, hg_pre_b, p_a, p_b, rw_pre_a, rw_pre_b,
                  *, hg_heads, rw_heads, tile, pipelined):
    j = pl.program_id(1)
    hg = dict(n_heads=hg_heads, tile=tile)
    rw = dict(n_heads=rw_heads, chunk=RW_CHUNK, tile=tile)

    @pl.when(j == 0)
    def _():
        for h in range(hg_heads):
            st_ref[h] = s0hg_ref[0, h].T
        s_ref[...] = s0rw_ref[0]
        carry_ref[...] = jnp.broadcast_to(shift_ref[0], carry_ref.shape)
        if pipelined:
            hg_pre_b[...] = jnp.zeros(hg_pre_b.shape, hg_pre_b.dtype)
            p_b[...] = jnp.zeros(p_b.shape, p_b.dtype)
            rw_pre_b[...] = jnp.zeros(rw_pre_b.shape, rw_pre_b.dtype)

    def fronts(hg_w, p_w, rw_w):
        return [_rwkv_front(zrw_ref, mu_ref, w0_ref, w2_ref, a0_ref, a2_ref, g2_ref, kk_ref, ka_ref,
                            rk_ref, hones_ref, rw_w, carry_ref, zs_ref, **rw),
                _hgrn_front(zhg_ref, loglb_ref, omlb_ref, gn_ref, hg_w, p_w, **hg)]

    def backs(hg_r, p_r, rw_r):
        return [_rwkv_back(rw_r, hones_ref, lnw_ref, lnb_ref, s_ref, y_ref, orw_ref, **rw),
                _hgrn_back(sel_ref, hg_r, p_r, a_ref, st_ref, ohg_ref, **hg)]

    def run(bodies, steps):
        live = [True] * len(bodies)
        while any(live):
            for n, body in enumerate(bodies):
                for _ in range(steps[n]):
                    if live[n]:
                        live[n] = next(body, _DONE) is not _DONE

    if not pipelined:
        run(fronts(hg_pre_a, p_a, rw_pre_a), MIX_STEPS[2:])
        run(backs(hg_pre_a, p_a, rw_pre_a), MIX_STEPS[:2])
    else:
        @pl.when(j % 2 == 0)
        def _():
            run(backs(hg_pre_b, p_b, rw_pre_b) + fronts(hg_pre_a, p_a, rw_pre_a), MIX_STEPS)

        @pl.when(j % 2 == 1)
        def _():
            run(backs(hg_pre_a, p_a, rw_pre_a) + fronts(hg_pre_b, p_b, rw_pre_b), MIX_STEPS)

    @pl.when(j == pl.num_programs(1) - 1)
    def _():
        for h in range(hg_heads):
            sfin_hg_ref[0, h] = st_ref[h].T
        sfin_rw_ref[0] = s_ref[...]


def _mixers(z_hg, z_rw, log_lb, one_m_lb, hg_norm, s0_hg, shift_prev, s0_rw,
            mu, w0, w2p, a0, a2p, g2, k_k, k_a, r_k, ln_w, ln_b):
    bsz, t_len, hg_proj = z_hg.shape
    rw_proj = z_rw.shape[2]
    hg_width = hg_proj // 4
    hg_heads = hg_width // HG_HEAD_DIM
    rw_heads = s0_rw.shape[1]
    hd = RW_HEAD_DIM
    rw_width = rw_heads * hd
    tile = min(MIX_TILE, t_len)
    assert t_len % tile == 0 and tile % RW_CHUNK == 0 and tile % HG_BLOCK == 0 and rw_heads % 2 == 0
    n_tiles = t_len // tile
    n_pairs = rw_heads // 2
    s0_rw = s0_rw.reshape(bsz, n_pairs, 2, hd, hd).transpose(0, 1, 3, 2, 4).reshape(bsz, n_pairs, hd, 2 * hd)
    group = jnp.arange(HG_BLOCK * HG_HEAD_DIM, dtype=jnp.int32)[:, None] // HG_HEAD_DIM
    selector = (group == jnp.arange(LANES, dtype=jnp.int32)[None, :]).astype(BF16)
    head_of = jnp.arange(LANES, dtype=jnp.int32) // hd
    head_ones = (head_of[:, None] == head_of[None, :]).astype(BF16)

    pipelined = n_tiles > 1
    if pipelined:
        in_tile = lambda j: jnp.minimum(j, n_tiles - 1)
        out_tile = lambda j: jnp.maximum(j - 1, 0)
    else:
        in_tile = out_tile = lambda j: j
    tok_in = lambda w: pl.BlockSpec((1, tile, w), lambda b, j: (b, in_tile(j), 0))
    tok_out = lambda w: pl.BlockSpec((1, tile, w), lambda b, j: (b, out_tile(j), 0))
    per_row = lambda a: pl.BlockSpec((1,) + a.shape[1:], lambda b, j: (b,) + (0,) * (a.ndim - 1))
    consts_hg = [log_lb, one_m_lb, hg_norm, selector]
    consts_rw = [mu, w0, w2p, a0, a2p, g2, k_k, k_a, r_k, ln_w, ln_b, head_ones]
    kern = functools.partial(_mixer_kernel, hg_heads=hg_heads, rw_heads=rw_heads, tile=tile,
                             pipelined=pipelined)
    o_hg, n_hg, o_rw, n_rw = pl.pallas_call(
        kern,
        grid=(bsz, n_tiles + 1 if pipelined else n_tiles),
        in_specs=[tok_in(hg_proj), tok_in(rw_proj)]
                 + [_const_spec(a.shape) for a in consts_hg] + [per_row(s0_hg)]
                 + [per_row(shift_prev), per_row(s0_rw)] + [_const_spec(a.shape) for a in consts_rw],
        out_specs=[tok_out(hg_width), per_row(s0_hg), tok_out(rw_width), per_row(s0_rw)],
        out_shape=[jax.ShapeDtypeStruct((bsz, t_len, hg_width), BF16),
                   jax.ShapeDtypeStruct(s0_hg.shape, F32),
                   jax.ShapeDtypeStruct((bsz, t_len, rw_width), BF16),
                   jax.ShapeDtypeStruct(s0_rw.shape, F32)],
        scratch_shapes=[pltpu.VMEM((hg_heads, HG_HEAD_DIM, HG_HEAD_DIM), F32),
                        pltpu.VMEM((n_pairs, hd, 2 * hd), F32),
                        pltpu.VMEM((SUBLANES, rw_proj), F32),
                        pltpu.VMEM((SUBLANES + tile, rw_proj), F32),
                        pltpu.VMEM((tile, rw_width), F32),
                        pltpu.VMEM((tile * hg_heads, LANES), F32)]
                       + [pltpu.VMEM((5, tile, hg_width), F32)] * 2
                       + [pltpu.VMEM((tile * hg_heads, HG_BLOCK * HG_HEAD_DIM), BF16)] * 2
                       + [pltpu.VMEM((10, tile, rw_width), F32)] * 2,
        compiler_params=pltpu.CompilerParams(dimension_semantics=("parallel", "arbitrary"),
                                             vmem_limit_bytes=VMEM_LIMIT),
        name="mixers",
    )(z_hg, z_rw, *consts_hg, s0_hg, shift_prev, s0_rw, *consts_rw)
    n_rw = n_rw.reshape(bsz, n_pairs, hd, 2, hd).transpose(0, 1, 3, 2, 4).reshape(bsz, rw_heads, hd, hd)
    return o_hg, n_hg, o_rw, n_rw


def _gelu_exact(x):
    return 0.5 * x * (1.0 + lax.erf(x * (2.0 ** -0.5)))


def _post_kernel(h_ref, ohg_ref, orw_ref, p_ref, cvprev_ref,
                 wout_ref, nffn_ref, win_ref, cw_ref, cb_ref, wo_ref,
                 nple_ref, pgate_ref, pproj_ref, fnorm_ref,
                 hout_ref, cvnew_ref, carry_ref, ugs_ref, *, tm, n_col_blocks, apply_final_norm):
    t_idx = pl.program_id(1)
    cols = FF_COLS
    d_ff = n_col_blocks * cols
    hg_width = ohg_ref.shape[2]
    col_block = lambda j: slice(j * cols, (j + 1) * cols)
    up = lambda j: (_dot(xn, win_ref[:, col_block(j)]),
                    _dot(xn, win_ref[:, d_ff + j * cols:d_ff + (j + 1) * cols]))

    @pl.when(t_idx == 0)
    def _():
        carry_ref[...] = jnp.zeros(carry_ref.shape, carry_ref.dtype)
        for j in range(n_col_blocks):
            carry_ref[j, SUBLANES - (CONV_W - 1):SUBLANES, :] = cvprev_ref[0, :, j * cols:(j + 1) * cols]

    h1 = (h_ref[0] + _dot(ohg_ref[0].astype(BF16), wout_ref[0:hg_width, :])
          + _dot(orw_ref[0].astype(BF16), wout_ref[hg_width:, :]))
    xn = _rmsnorm(h1, nffn_ref[...]).astype(BF16)
    acc = jnp.zeros_like(h1)
    up_next = up(0)
    for j in range(n_col_blocks):
        ug, uv = up_next
        if j + 1 < n_col_blocks:
            up_next = up(j + 1)
        ugs_ref[0:SUBLANES, :] = carry_ref[j]
        ugs_ref[SUBLANES:SUBLANES + tm, :] = ug
        carry_ref[j] = ug[tm - SUBLANES:tm, :]
        cvnew_ref[0, :, j * cols:(j + 1) * cols] = ug[tm - (CONV_W - 1):tm, :]
        cw = cw_ref[:, col_block(j)]
        c = (cb_ref[:, col_block(j)] + cw[0:1, :] * ugs_ref[SUBLANES - 2:SUBLANES - 2 + tm, :]
             + cw[1:2, :] * ugs_ref[SUBLANES - 1:SUBLANES - 1 + tm, :] + cw[2:3, :] * ug)
        hid = (_gelu_exact(c) * uv).astype(BF16)
        acc = acc + _dot(hid, wo_ref[col_block(j), :])
    h2 = h1 + acc
    gate = _sigmoid(_dot(_rmsnorm(h2, nple_ref[...]).astype(BF16), pgate_ref[...]))
    h3 = h2 + gate * _dot(p_ref[0].astype(BF16), pproj_ref[...])
    if apply_final_norm:
        h3 = _rmsnorm(h3, fnorm_ref[...])
    hout_ref[0] = h3


def _post(h, o_hg, o_rw, p, cv_prev, w_out, norm_ffn, ffn_in, cw, cb, wo, norm_ple,
          ple_gate, ple_proj, final_norm, apply_final_norm):
    bsz, t_len, d = h.shape
    tm = min(POST_TILE, t_len)
    assert t_len % tm == 0 and tm >= SUBLANES
    d_ff = cb.shape[1]
    n_col_blocks = d_ff // FF_COLS
    assert d_ff % FF_COLS == 0
    kern = functools.partial(_post_kernel, tm=tm, n_col_blocks=n_col_blocks,
                             apply_final_norm=apply_final_norm)
    tok = lambda w: pl.BlockSpec((1, tm, w), lambda b, t: (b, t, 0))
    cv_spec = pl.BlockSpec((1, CONV_W - 1, d_ff), lambda b, t: (b, 0, 0))
    consts = [w_out, norm_ffn, ffn_in, cw, cb, wo, norm_ple, ple_gate, ple_proj, final_norm]
    return pl.pallas_call(
        kern,
        grid=(bsz, t_len // tm),
        in_specs=[tok(d), tok(o_hg.shape[2]), tok(o_rw.shape[2]), tok(p.shape[2]), cv_spec]
                 + [_const_spec(a.shape) for a in consts],
        out_specs=[tok(d), cv_spec],
        out_shape=[jax.ShapeDtypeStruct(h.shape, F32), jax.ShapeDtypeStruct(cv_prev.shape, F32)],
        scratch_shapes=[pltpu.VMEM((n_col_blocks, SUBLANES, FF_COLS), F32),
                        pltpu.VMEM((SUBLANES + tm, FF_COLS), F32)],
        compiler_params=pltpu.CompilerParams(dimension_semantics=("parallel", "arbitrary"),
                                             vmem_limit_bytes=VMEM_LIMIT),
        name="post",
    )(h, o_hg, o_rw, p, cv_prev, *consts)


def _layer_params(i, hg_proj, lb_all, w):
    rw_width = w["rw_w0"].shape[1]
    row = lambda a: a.reshape(1, -1)
    lb = jnp.maximum(lb_all[i], 0.0)
    zeros_lora = jnp.zeros((RW_LORA_W, rw_width), F32)
    return dict(
        norm_mix=row(w["norm_mix"][i]),
        w_in=w["w_in"][i].astype(BF16),
        log_lb=row(jnp.log(lb)), one_m_lb=row(1.0 - lb), hg_norm=row(w["hg_norm"][i]),
        mu=row(w["rw_mu"][i]), w0=row(w["rw_w0"][i]),
        w2p=jnp.concatenate([w["rw_w2"][i], zeros_lora], axis=0).astype(BF16),
        a0=row(w["rw_a0"][i]),
        a2p=jnp.concatenate([zeros_lora, w["rw_a2"][i]], axis=0).astype(BF16),
        g2=w["rw_g2"][i].astype(BF16),
        k_k=row(w["rw_k_k"][i]), k_a=row(w["rw_k_a"][i]), r_k=row(w["rw_r_k"][i]),
        ln_w=row(w["rw_ln_w"][i]), ln_b=row(w["rw_ln_b"][i]),
        w_out=w["w_out"][i].astype(BF16),
        norm_ffn=row(w["norm_ffn"][i]),
        ffn_in=w["ffn_in"][i].astype(BF16),
        cw=w["ffn_conv_w"][i], cb=row(w["ffn_conv_b"][i]),
        wo=w["ffn_out"][i].astype(BF16),
        norm_ple=row(w["norm_ple"][i]),
        ple_gate=w["ple_gate"][i].astype(BF16), ple_proj=w["ple_proj"][i].astype(BF16),
        final_norm=row(w["final_norm"]),
    )


def _trunk(x, p, s_hg, s_rw, s_sh, s_cv, layers):
    bsz, t_len, d = x.shape
    depth = len(layers)
    h = x
    new_hg, new_rw, new_sh, new_cv = [], [], [], []
    for i, lp in enumerate(layers):
        z_hg, z_rw = _in_proj(h.reshape(bsz * t_len, d), lp["norm_mix"], lp["w_in"], 4 * lp["hg_norm"].shape[1])
        z_hg = z_hg.reshape(bsz, t_len, -1)
        z_rw = z_rw.reshape(bsz, t_len, -1)
        o_hg, n_hg, o_rw, n_rw = _mixers(
            z_hg, z_rw, lp["log_lb"], lp["one_m_lb"], lp["hg_norm"], s_hg[i], s_sh[i][:, None, :],
            s_rw[i], lp["mu"], lp["w0"], lp["w2p"], lp["a0"], lp["a2p"], lp["g2"], lp["k_k"],
            lp["k_a"], lp["r_k"], lp["ln_w"], lp["ln_b"])
        h, n_cv = _post(h, o_hg, o_rw, p[i], s_cv[i], lp["w_out"], lp["norm_ffn"],
                        lp["ffn_in"], lp["cw"], lp["cb"], lp["wo"], lp["norm_ple"],
                        lp["ple_gate"], lp["ple_proj"], lp["final_norm"], i == depth - 1)
        new_hg.append(n_hg)
        new_rw.append(n_rw)
        new_sh.append(z_rw[:, -1])
        new_cv.append(n_cv)
    return h, jnp.stack(new_hg), jnp.stack(new_rw), jnp.stack(new_sh), jnp.stack(new_cv)


def kernel(x_prompt, x_sample, state_hgrn, state_rwkv, state_shift, state_conv, p_prompt, p_sample,
           norm_mix, w_in, lb_raw, hg_norm, rw_mu, rw_w0, rw_w2, rw_a0, rw_a2, rw_g2, rw_k_k, rw_k_a,
           rw_r_k, rw_ln_w, rw_ln_b, w_out, norm_ffn, ffn_in, ffn_conv_w, ffn_conv_b, ffn_out,
           norm_ple, ple_gate, ple_proj, final_norm):
    w = dict(norm_mix=norm_mix, w_in=w_in, hg_norm=hg_norm, rw_mu=rw_mu, rw_w0=rw_w0, rw_w2=rw_w2,
             rw_a0=rw_a0, rw_a2=rw_a2, rw_g2=rw_g2, rw_k_k=rw_k_k, rw_k_a=rw_k_a,
             rw_r_k=rw_r_k.reshape(rw_r_k.shape[0], -1), rw_ln_w=rw_ln_w, rw_ln_b=rw_ln_b, w_out=w_out,
             norm_ffn=norm_ffn, ffn_in=ffn_in, ffn_conv_w=ffn_conv_w, ffn_conv_b=ffn_conv_b,
             ffn_out=ffn_out, norm_ple=norm_ple, ple_gate=ple_gate, ple_proj=ple_proj,
             final_norm=final_norm)
    depth = w_in.shape[0]
    hg_proj = 4 * hg_norm.shape[1]
    lb_all = jnp.cumsum(jax.nn.softmax(lb_raw.astype(F32), axis=0), axis=0)
    lb_all = lb_all - lb_all[:1]
    layers = [_layer_params(i, hg_proj, lb_all, w) for i in range(depth)]

    nb = x_prompt.shape[0]
    zeros_like_state = lambda s: jnp.zeros((depth, nb) + s.shape[2:], s.dtype)
    y_p, hg_p, rw_p, sh_p, cv_p = _trunk(x_prompt, p_prompt, zeros_like_state(state_hgrn),
                                         zeros_like_state(state_rwkv), zeros_like_state(state_shift),
                                         zeros_like_state(state_conv), layers)
    y_s, hg_s, rw_s, sh_s, cv_s = _trunk(x_sample, p_sample, state_hgrn, state_rwkv, state_shift,
                                         state_conv, layers)
    return (y_p, y_s, hg_p, rw_p, sh_p, cv_p, hg_s, rw_s, sh_s, cv_s)
```

```python
import functools

import jax
import jax.numpy as jnp
from jax import lax
from jax.experimental import pallas as pl
from jax.experimental.pallas import tpu as pltpu

F32 = jnp.float32
BF16 = jnp.bfloat16

NORM_EPS = 1e-6
RW_GN_EPS = 64e-5
HG_HEAD_DIM = 128
RW_HEAD_DIM = 64
RW_LORA_W = 64
CONV_W = 3
HG_BLOCK = 16
RW_CHUNK = 64
RW_SUB = 16
MIX_TILE = 256
FF_COLS = 256
POST_TILE = 512
LANES = 128
SUBLANES = 8
VMEM_LIMIT = 56 * 1024 * 1024

NN = ((1,), (0,))
NT = ((1,), (1,))
TN = ((0,), (0,))


def _dot(a, b, dims=NN):
    return lax.dot_general(a, b, (dims, ((), ())), preferred_element_type=F32)


def _dot1(a, b, dims=NN):
    return _dot(a.astype(BF16), b.astype(BF16), dims)


def _split2(x):
    hi = x.astype(BF16)
    lo = (x - hi.astype(F32)).astype(BF16)
    return hi, lo


def _dot_xl2(a_exact, x, dims=NN):
    hi, lo = _split2(x)
    return _dot(a_exact, hi, dims) + _dot(a_exact, lo, dims)


def _dot_xr2(x, b, dims=NN):
    hi, lo = _split2(x)
    b = b.astype(BF16)
    return _dot(hi, b, dims) + _dot(lo, b, dims)


def _head_sums(x, pair_ones):
    n, width = x.shape
    tiles = [x[:, p * LANES:(p + 1) * LANES] for p in range(width // LANES)]
    sums = _dot(jnp.concatenate(tiles, axis=0).astype(BF16), pair_ones)
    return jnp.concatenate([sums[p * n:(p + 1) * n] for p in range(width // LANES)], axis=1)


def _sigmoid(x):
    return 0.5 * jnp.tanh(0.5 * x) + 0.5


def _softplus(x):
    return jnp.maximum(x, 0.0) + jnp.log(1.0 + jnp.exp(-jnp.abs(x)))


def _rmsnorm(x, g):
    return x * lax.rsqrt(jnp.mean(x * x, axis=-1, keepdims=True) + NORM_EPS) * g


def _iota2(shape, axis):
    return lax.broadcasted_iota(jnp.int32, shape, axis)


def _const_spec(shape):
    zeros = (0,) * len(shape)
    return pl.BlockSpec(shape, lambda *_: zeros, pipeline_mode=pl.Buffered(1))


def _map(fn, *lists):
    return [fn(*xs) for xs in zip(*lists)]


def _inproj_kernel(x_ref, g_ref, w_ref, zhg_ref, zrw_ref):
    hg_proj = zhg_ref.shape[1]
    xn = _rmsnorm(x_ref[...], g_ref[...]).astype(BF16)
    zhg_ref[...] = _dot(xn, w_ref[:, 0:hg_proj])
    zrw_ref[...] = _dot(xn, w_ref[:, hg_proj:])


def _in_proj(x2, g, w_in, hg_proj):
    n, d = x2.shape
    tm = min(512, n)
    assert n % tm == 0 and hg_proj % LANES == 0
    rw_proj = w_in.shape[1] - hg_proj
    return pl.pallas_call(
        _inproj_kernel,
        grid=(n // tm,),
        in_specs=[pl.BlockSpec((tm, d), lambda i: (i, 0)),
                  _const_spec((1, d)),
                  _const_spec(w_in.shape)],
        out_specs=[pl.BlockSpec((tm, hg_proj), lambda i: (i, 0)),
                   pl.BlockSpec((tm, rw_proj), lambda i: (i, 0))],
        out_shape=[jax.ShapeDtypeStruct((n, hg_proj), F32),
                   jax.ShapeDtypeStruct((n, rw_proj), F32)],
        compiler_params=pltpu.CompilerParams(dimension_semantics=("parallel",),
                                             vmem_limit_bytes=VMEM_LIMIT),
        name="in_proj",
    )(x2, g, w_in)


HG_Q, HG_K, HG_BL, HG_V, HG_GATE = range(5)


def _hgrn_front(z_ref, loglb_ref, omlb_ref, gn_ref, pre_ref, p_ref, *, n_heads, tile):
    dk = HG_HEAD_DIM
    width = n_heads * dk
    blk = HG_BLOCK
    zq = z_ref[0, :, 0:width]
    zf = z_ref[0, :, width:2 * width]
    zg = z_ref[0, :, 3 * width:4 * width]
    pre_ref[HG_Q] = zq * _sigmoid(zq)
    pre_ref[HG_V] = z_ref[0, :, 2 * width:3 * width]
    pre_ref[HG_GATE] = gn_ref[...] * (zg * _sigmoid(zg))
    soft = jnp.log(1.0 + jnp.exp(-jnp.abs(zf)))
    log_sig_pos = jnp.minimum(zf, 0.0) - soft
    log_sig_neg = jnp.minimum(-zf, 0.0) - soft
    lo_term = loglb_ref[...] + log_sig_neg
    top = jnp.maximum(log_sig_pos, lo_term)
    logf = top + jnp.log(1.0 + jnp.exp(-jnp.abs(log_sig_pos - lo_term)))
    pre_ref[HG_K] = omlb_ref[...] * jnp.exp(log_sig_neg)
    row = _iota2((tile, tile), 0)
    col = _iota2((tile, tile), 1)
    tri = ((row // blk == col // blk) & (col <= row)).astype(BF16)
    pre_ref[HG_BL] = _dot_xl2(tri, logf)
    yield

    t_row = _iota2((blk, dk), 0)
    for i in range(tile // blk):
        rows = slice(i * blk, (i + 1) * blk)
        for h in range(n_heads):
            lanes = slice(h * dk, (h + 1) * dk)
            qb = pre_ref[HG_Q, rows, lanes]
            kb = pre_ref[HG_K, rows, lanes]
            bb = pre_ref[HG_BL, rows, lanes]
            p0 = (i * n_heads + h) * blk
            for s in range(blk):
                causal = t_row >= s
                ps = jnp.where(causal, qb * kb[s:s + 1, :] * jnp.exp(bb - bb[s:s + 1, :]), 0.0)
                p_ref[p0:p0 + blk, s * dk:(s + 1) * dk] = ps.astype(BF16)
            yield


def _hgrn_back(sel_ref, pre_ref, p_ref, a_ref, st_ref, o_ref, *, n_heads, tile):
    dk = HG_HEAD_DIM
    blk = HG_BLOCK
    n_blocks = tile // blk
    a_ref[...] = _dot(p_ref[...], sel_ref[...])
    yield

    block_rows = [slice(i * blk, (i + 1) * blk) for i in range(n_blocks)]
    e_end = [pre_ref[HG_BL, (i + 1) * blk - 1:(i + 1) * blk, :] for i in range(n_blocks)]
    start = [jnp.zeros_like(e_end[0])]
    for i in range(n_blocks):
        start.append(start[i] + e_end[i])
    qe = [pre_ref[HG_Q, r, :] * jnp.exp(pre_ref[HG_BL, r, :]) for r in block_rows]
    ke = [pre_ref[HG_K, r, :] * jnp.exp(e - pre_ref[HG_BL, r, :])
          for r, e in zip(block_rows, e_end)]
    vv = [pre_ref[HG_V, r, :] for r in block_rows]
    q_tile = jnp.concatenate([x * jnp.exp(s) for x, s in zip(qe, start)], axis=0)
    k_tile = jnp.concatenate([x * jnp.exp(start[n_blocks] - s) for x, s in zip(ke, start[1:])], axis=0)
    v_tile = pre_ref[HG_V]
    decay_tile = jnp.exp(start[n_blocks])
    yield

    heads = range(n_heads)
    lanes = [slice(h * dk, (h + 1) * dk) for h in heads]
    st = [st_ref[h] for h in heads]
    new_st = [st[h] * decay_tile[:, lanes[h]] + _dot1(v_tile[:, lanes[h]], k_tile[:, lanes[h]], TN)
              for h in heads]
    yield
    o_state = [_dot1(q_tile[:, lanes[h]], st[h], NT) for h in heads]
    o_blocks = [[o_state[h][r] for r in block_rows] for h in heads]
    yield
    span = n_blocks // 2
    while span >= 1:
        for lo in range(0, n_blocks, 2 * span):
            mid, hi = lo + span, lo + 2 * span
            q_r = jnp.concatenate([qe[i] if i == mid else qe[i] * jnp.exp(start[i] - start[mid])
                                   for i in range(mid, hi)], axis=0)
            k_l = jnp.concatenate([ke[j] if j == mid - 1 else ke[j] * jnp.exp(start[mid] - start[j + 1])
                                   for j in range(lo, mid)], axis=0)
            v_l = jnp.concatenate(vv[lo:mid], axis=0)
            scores = [_dot1(q_r[:, lanes[h]], k_l[:, lanes[h]], NT) for h in heads]
            upd = [_dot1(scores[h], v_l[:, lanes[h]]) for h in heads]
            for h in heads:
                for n, i in enumerate(range(mid, hi)):
                    o_blocks[h][i] = o_blocks[h][i] + upd[h][n * blk:(n + 1) * blk]
            yield
        span //= 2
    for i in range(n_blocks):
        for h in heads:
            p0 = (i * n_heads + h) * blk
            o_blocks[h][i] = o_blocks[h][i] + _dot1(a_ref[p0:p0 + blk, 0:blk], vv[i][:, lanes[h]])
        yield
    for h in heads:
        o = jnp.concatenate(o_blocks[h], axis=0)
        o = o * lax.rsqrt(jnp.mean(o * o, axis=-1, keepdims=True) + NORM_EPS)
        o_ref[0, :, lanes[h]] = (o * pre_ref[HG_GATE, :, lanes[h]]).astype(o_ref.dtype)
    for h in heads:
        st_ref[h] = new_st[h]


RW_KT, RW_RT, RW_V, RW_BH, RW_KH, RW_BL, RW_KL, RW_C, RW_RK, RW_G = range(10)


def _block_diag(y):
    lo = _iota2(y.shape, 1) < RW_HEAD_DIM
    y = y.astype(BF16)
    zero = jnp.zeros_like(y)
    return jnp.concatenate([jnp.where(lo, y, zero), jnp.where(lo, zero, y)], axis=0)


def _pair_dot(x, y):
    return _dot(x.astype(BF16), _block_diag(y))


def _pair_dot_nt(x, y):
    return _dot(x.astype(BF16), _block_diag(y), NT)


def _pair_dot2(x, y, z):
    return _dot(x.astype(BF16), jnp.concatenate([_block_diag(y), _block_diag(z)], axis=1))


def _pair_dot2_nt(x, y, z):
    return _dot(x.astype(BF16), jnp.concatenate([_block_diag(y), _block_diag(z)], axis=0), NT)


def _unit_lower_inverses(n_list, eye, same_sub):
    d = [jnp.where(same_sub, n, 0.0) for n in n_list]
    l_off = _map(lambda n, dd: n - dd, n_list, d)
    t_d = [eye - dd for dd in d]
    n = n_list[0].shape[0]
    d_pow = _map(_pair_dot, d, d)
    yield
    for _ in range(RW_SUB.bit_length() - 3):
        both = _map(lambda dp, t: _pair_dot(jnp.concatenate([dp, t], axis=0), dp), d_pow, t_d)
        t_d = _map(lambda t, b: t + b[n:], t_d, both)
        d_pow = [b[0:n] for b in both]
        yield
    t_d = _map(lambda t, dp: t + _pair_dot(t, dp), t_d, d_pow)
    yield
    p = _map(_pair_dot, t_d, l_off)
    yield
    p2 = _map(_pair_dot, p, p)
    yield
    t_p = _map(lambda pp, pp2: (eye - pp) + _pair_dot(eye - pp, pp2), p, p2)
    yield
    return _map(_pair_dot, t_p, t_d)


def _rwkv_front(z_ref, mu_ref, w0_ref, w2_ref, a0_ref, a2_ref, g2_ref, kk_ref, ka_ref, rk_ref,
                hones_ref, pre_ref, carry_ref, zs_ref, *, n_heads, chunk, tile):
    hd = RW_HEAD_DIM
    width = n_heads * hd
    z = z_ref[0]
    zs_ref[0:SUBLANES, :] = carry_ref[...]
    zs_ref[SUBLANES:SUBLANES + tile, :] = z
    carry_ref[...] = z[tile - SUBLANES:tile, :]
    z_prev = zs_ref[SUBLANES - 1:SUBLANES - 1 + tile, :]
    zm = z + (z_prev - z) * mu_ref[...]

    r = zm[:, 0:width]
    k = zm[:, width:2 * width]
    v = zm[:, 2 * width:3 * width]
    lora_in = zm[:, 3 * width:3 * width + 2 * RW_LORA_W]
    gd = zm[:, 3 * width + 2 * RW_LORA_W:]
    pre_ref[RW_V] = v
    pre_ref[RW_G] = _dot(_sigmoid(gd).astype(BF16), g2_ref[...])
    yield

    w_pre = w0_ref[...] + _dot_xr2(jnp.tanh(lora_in), w2_ref[...])
    w_raw = -_softplus(-w_pre) - 0.5
    logw = -jnp.exp(w_raw)
    a = _sigmoid(a0_ref[...] + _dot1(lora_in, a2_ref[...]))
    kk = k * kk_ref[...]
    kk = kk * jnp.minimum(lax.rsqrt(_head_sums(kk * kk, hones_ref[...])), 1e12)
    k2 = k * (1.0 + (a - 1.0) * ka_ref[...])
    beta = kk * a
    pre_ref[RW_RK] = r * k2 * rk_ref[...]
    yield

    trow = _iota2((tile, tile), 0)
    tcol = _iota2((tile, tile), 1)
    tri = ((trow // chunk == tcol // chunk) & (tcol <= trow)).astype(BF16)
    c = _dot_xl2(tri, logw)
    pre_ref[RW_C] = c
    e_neg = jnp.exp(-c)
    pre_ref[RW_KT] = kk * jnp.exp(c - logw)
    pre_ref[RW_RT] = r * jnp.exp(c)
    pre_ref[RW_BH] = beta * e_neg
    pre_ref[RW_KH] = k2 * e_neg
    yield
    for ci in range(tile // chunk):
        rows = slice(ci * chunk, (ci + 1) * chunk)
        e_end = jnp.exp(c[rows][chunk - 1:chunk, :] - c[rows])
        pre_ref[RW_BL, rows, :] = beta[rows] * e_end
        pre_ref[RW_KL, rows, :] = k2[rows] * e_end
    yield


def _rwkv_back(pre_ref, hones_ref, lnw_ref, lnb_ref, s_ref, y_ref, o_ref, *, n_heads, chunk, tile):
    hd = RW_HEAD_DIM
    n_chunks = tile // chunk
    pw = 2 * hd
    pairs = range(n_heads // 2)
    lanes = [slice(p * pw, (p + 1) * pw) for p in pairs]
    row = _iota2((chunk, pw), 0)
    lane = _iota2((chunk, pw), 1)
    lo_half = lane < hd
    col = jnp.where(lo_half, lane, lane - hd)
    incl2 = jnp.concatenate([col < row, col <= row], axis=0)
    incl4 = jnp.concatenate([incl2, incl2], axis=1)
    eye = (row == col).astype(F32)
    same_sub = (row // RW_SUB) == (col // RW_SUB)
    same_head = (_iota2((pw, pw), 0) // hd) == (_iota2((pw, pw), 1) // hd)

    chunk_rows = [slice(ci * chunk, (ci + 1) * chunk) for ci in range(n_chunks)]
    units = [(ci, p) for ci in range(n_chunks) for p in pairs]
    at = lambda name: [pre_ref[name, chunk_rows[ci], lanes[p]] for ci, p in units]
    kt, rt, vv, b_l, k_l = at(RW_KT), at(RW_RT), at(RW_V), at(RW_BL), at(RW_KL)
    g_last = [jnp.exp(pre_ref[RW_C, (ci + 1) * chunk - 1:(ci + 1) * chunk, lanes[p]]) for ci, p in units]
    lhs = _map(lambda x, y: jnp.concatenate([x, y], axis=0), kt, rt)
    g_bk = [jnp.where(incl4, _pair_dot2_nt(l, xb, xk), 0.0)
            for l, xb, xk in zip(lhs, at(RW_BH), at(RW_KH))]
    a_ab = [x[0:chunk, 0:pw] for x in g_bk]
    a_rb = [x[chunk:, 0:pw] for x in g_bk]
    yield
    gkv = [_pair_dot(x[:, pw:], v_) for x, v_ in zip(g_bk, vv)]
    x_loc = [x[0:chunk] for x in gkv]
    yield
    t_inv = yield from _unit_lower_inverses(a_ab, eye, same_sub)
    yield
    wu = _map(_pair_dot2, t_inv, kt, x_loc)
    w_t = [x[:, 0:pw] for x in wu]
    u_loc = [x[:, pw:] for x in wu]
    yield
    a_wu = _map(_pair_dot2, a_rb, w_t, u_loc)
    q_c = _map(lambda r_, x: r_ - x[:, 0:pw], rt, a_wu)
    y_loc = _map(lambda x, y: x[chunk:] - y[:, pw:], gkv, a_wu)
    yield
    m_off = _map(lambda w_, bl_: jnp.where(same_head, -_dot1(w_, bl_, TN), 0.0), w_t, b_l)
    yield
    c_full = _map(lambda v_, u_, kl_, bl_: _dot1(jnp.concatenate([v_, -u_], axis=0),
                                                 jnp.concatenate([kl_, bl_], axis=0), TN),
                  vv, u_loc, k_l, b_l)
    c_add = [jnp.where(lo_half, x[0:hd], x[hd:]) for x in c_full]
    yield
    state = [s_ref[p] for p in pairs]
    for idx, (ci, p) in enumerate(units):
        y_ref[chunk_rows[ci], lanes[p]] = _pair_dot_nt(q_c[idx], state[p]) + y_loc[idx]
        state[p] = state[p] * g_last[idx] + _dot1(state[p], m_off[idx]) + c_add[idx]
    for p in pairs:
        s_ref[p] = state[p]
    yield

    y = y_ref[...]
    head_ones = hones_ref[...]
    mean = _head_sums(y, head_ones) * (1.0 / hd)
    yc = y - mean
    var = _head_sums(yc * yc, head_ones) * (1.0 / hd)
    yn = yc * lax.rsqrt(var + RW_GN_EPS) * lnw_ref[...] + lnb_ref[...]
    bonus = _head_sums(pre_ref[RW_RK], head_ones) * pre_ref[RW_V]
    o_ref[0] = ((yn + bonus) * pre_ref[RW_G]).astype(o_ref.dtype)


_DONE = object()
MIX_STEPS = (3, 2, 1, 4)


def _mixer_kernel(zhg_ref, zrw_ref, loglb_ref, omlb_ref, gn_ref, sel_ref, s0hg_ref,
                  shift_ref, s0rw_ref, mu_ref, w0_ref, w2_ref, a0_ref, a2_ref, g2_ref,
                  kk_ref, ka_ref, rk_ref, lnw_ref, lnb_ref, hones_ref,
                  ohg_ref, sfin_hg_ref, orw_ref, sfin_rw_ref,
                  st_ref, s_ref, carry_ref, zs_ref, y_ref, a_ref,
                  hg_pre_a, hg_pre_b, p_a, p_b, rw_pre_a, rw_pre_b,
                  *, hg_heads, rw_heads, tile, pipelined):
    j = pl.program_id(1)
    hg = dict(n_heads=hg_heads, tile=tile)
    rw = dict(n_heads=rw_heads, chunk=RW_CHUNK, tile=tile)

    @pl.when(j == 0)
    def _():
        for h in range(hg_heads):
            st_ref[h] = s0hg_ref[0, h].T
        s_ref[...] = s0rw_ref[0]
        carry_ref[...] = jnp.broadcast_to(shift_ref[0], carry_ref.shape)
        if pipelined:
            hg_pre_b[...] = jnp.zeros(hg_pre_b.shape, hg_pre_b.dtype)
            p_b[...] = jnp.zeros(p_b.shape, p_b.dtype)
            rw_pre_b[...] = jnp.zeros(rw_pre_b.shape, rw_pre_b.dtype)

    def fronts(hg_w, p_w, rw_w):
        return [_rwkv_front(zrw_ref, mu_ref, w0_ref, w2_ref, a0_ref, a2_ref, g2_ref, kk_ref, ka_ref,
                            rk_ref, hones_ref, rw_w, carry_ref, zs_ref, **rw),
                _hgrn_front(zhg_ref, loglb_ref, omlb_ref, gn_ref, hg_w, p_w, **hg)]

    def backs(hg_r, p_r, rw_r):
        return [_rwkv_back(rw_r, hones_ref, lnw_ref, lnb_ref, s_ref, y_ref, orw_ref, **rw),
                _hgrn_back(sel_ref, hg_r, p_r, a_ref, st_ref, ohg_ref, **hg)]

    def run(bodies, steps):
        live = [True] * len(bodies)
        while any(live):
            for n, body in enumerate(bodies):
                for _ in range(steps[n]):
                    if live[n]:
                        live[n] = next(body, _DONE) is not _DONE

    if not pipelined:
        run(fronts(hg_pre_a, p_a, rw_pre_a), MIX_STEPS[2:])
        run(backs(hg_pre_a, p_a, rw_pre_a), MIX_STEPS[:2])
    else:
        @pl.when(j % 2 == 0)
        def _():
            run(backs(hg_pre_b, p_b, rw_pre_b) + fronts(hg_pre_a, p_a, rw_pre_a), MIX_STEPS)

        @pl.when(j % 2 == 1)
        def _():
            run(backs(hg_pre_a, p_a, rw_pre_a) + fronts(hg_pre_b, p_b, rw_pre_b), MIX_STEPS)

    @pl.when(j == pl.num_programs(1) - 1)
    def _():
        for h in range(hg_heads):
            sfin_hg_ref[0, h] = st_ref[h].T
        sfin_rw_ref[0] = s_ref[...]


def _mixers(z_hg, z_rw, log_lb, one_m_lb, hg_norm, s0_hg, shift_prev, s0_rw,
            mu, w0, w2p, a0, a2p, g2, k_k, k_a, r_k, ln_w, ln_b):
    bsz, t_len, hg_proj = z_hg.shape
    rw_proj = z_rw.shape[2]
    hg_width = hg_proj // 4
    hg_heads = hg_width // HG_HEAD_DIM
    rw_heads = s0_rw.shape[1]
    hd = RW_HEAD_DIM
    rw_width = rw_heads * hd
    tile = min(MIX_TILE, t_len)
    assert t_len % tile == 0 and tile % RW_CHUNK == 0 and tile % HG_BLOCK == 0 and rw_heads % 2 == 0
    n_tiles = t_len // tile
    n_pairs = rw_heads // 2
    s0_rw = s0_rw.reshape(bsz, n_pairs, 2, hd, hd).transpose(0, 1, 3, 2, 4).reshape(bsz, n_pairs, hd, 2 * hd)
    group = jnp.arange(HG_BLOCK * HG_HEAD_DIM, dtype=jnp.int32)[:, None] // HG_HEAD_DIM
    selector = (group == jnp.arange(LANES, dtype=jnp.int32)[None, :]).astype(BF16)
    head_of = jnp.arange(LANES, dtype=jnp.int32) // hd
    head_ones = (head_of[:, None] == head_of[None, :]).astype(BF16)

    pipelined = n_tiles > 1
    if pipelined:
        in_tile = lambda j: jnp.minimum(j, n_tiles - 1)
        out_tile = lambda j: jnp.maximum(j - 1, 0)
    else:
        in_tile = out_tile = lambda j: j
    tok_in = lambda w: pl.BlockSpec((1, tile, w), lambda b, j: (b, in_tile(j), 0))
    tok_out = lambda w: pl.BlockSpec((1, tile, w), lambda b, j: (b, out_tile(j), 0))
    per_row = lambda a: pl.BlockSpec((1,) + a.shape[1:], lambda b, j: (b,) + (0,) * (a.ndim - 1))
    consts_hg = [log_lb, one_m_lb, hg_norm, selector]
    consts_rw = [mu, w0, w2p, a0, a2p, g2, k_k, k_a, r_k, ln_w, ln_b, head_ones]
    kern = functools.partial(_mixer_kernel, hg_heads=hg_heads, rw_heads=rw_heads, tile=tile,
                             pipelined=pipelined)
    o_hg, n_hg, o_rw, n_rw = pl.pallas_call(
        kern,
        grid=(bsz, n_tiles + 1 if pipelined else n_tiles),
        in_specs=[tok_in(hg_proj), tok_in(rw_proj)]
                 + [_const_spec(a.shape) for a in consts_hg] + [per_row(s0_hg)]
                 + [per_row(shift_prev), per_row(s0_rw)] + [_const_spec(a.shape) for a in consts_rw],
        out_specs=[tok_out(hg_width), per_row(s0_hg), tok_out(rw_width), per_row(s0_rw)],
        out_shape=[jax.ShapeDtypeStruct((bsz, t_len, hg_width), BF16),
                   jax.ShapeDtypeStruct(s0_hg.shape, F32),
                   jax.ShapeDtypeStruct((bsz, t_len, rw_width), BF16),
                   jax.ShapeDtypeStruct(s0_rw.shape, F32)],
        scratch_shapes=[pltpu.VMEM((hg_heads, HG_HEAD_DIM, HG_HEAD_DIM), F32),
                        pltpu.VMEM((n_pairs, hd, 2 * hd), F32),
                        pltpu.VMEM((SUBLANES, rw_proj), F32),
                        pltpu.VMEM((SUBLANES + tile, rw_proj), F32),
                        pltpu.VMEM((tile, rw_width), F32),
                        pltpu.VMEM((tile * hg_heads, LANES), F32)]
                       + [pltpu.VMEM((5, tile, hg_width), F32)] * 2
                       + [pltpu.VMEM((tile * hg_heads, HG_BLOCK * HG_HEAD_DIM), BF16)] * 2
                       + [pltpu.VMEM((10, tile, rw_width), F32)] * 2,
        compiler_params=pltpu.CompilerParams(dimension_semantics=("parallel", "arbitrary"),
                                             vmem_limit_bytes=VMEM_LIMIT),
        name="mixers",
    )(z_hg, z_rw, *consts_hg, s0_hg, shift_prev, s0_rw, *consts_rw)
    n_rw = n_rw.reshape(bsz, n_pairs, hd, 2, hd).transpose(0, 1, 3, 2, 4).reshape(bsz, rw_heads, hd, hd)
    return o_hg, n_hg, o_rw, n_rw


def _gelu_exact(x):
    return 0.5 * x * (1.0 + lax.erf(x * (2.0 ** -0.5)))


def _post_kernel(h_ref, ohg_ref, orw_ref, p_ref, cvprev_ref,
                 wout_ref, nffn_ref, win_ref, cw_ref, cb_ref, wo_ref,
                 nple_ref, pgate_ref, pproj_ref, fnorm_ref,
                 hout_ref, cvnew_ref, carry_ref, ugs_ref, *, seqs, tm, n_col_blocks, apply_final_norm):
    t_idx = pl.program_id(1)
    cols = FF_COLS
    d_ff = n_col_blocks * cols
    hg_width = ohg_ref.shape[-1]
    rows = seqs * tm
    flat = lambda ref: ref[...].reshape(rows, ref.shape[-1])
    col_block = lambda j: slice(j * cols, (j + 1) * cols)
    up = lambda j: (_dot(xn, win_ref[:, col_block(j)]),
                    _dot(xn, win_ref[:, d_ff + j * cols:d_ff + (j + 1) * cols]))

    @pl.when(t_idx == 0)
    def _():
        carry_ref[...] = jnp.zeros(carry_ref.shape, carry_ref.dtype)
        for j in range(n_col_blocks):
            for b in range(seqs):
                carry_ref[j, b, SUBLANES - (CONV_W - 1):SUBLANES, :] = cvprev_ref[b, :, col_block(j)]

    h1 = (flat(h_ref) + _dot(flat(ohg_ref).astype(BF16), wout_ref[0:hg_width, :])
          + _dot(flat(orw_ref).astype(BF16), wout_ref[hg_width:, :]))
    xn = _rmsnorm(h1, nffn_ref[...]).astype(BF16)
    acc = jnp.zeros_like(h1)
    up_next = up(0)
    for j in range(n_col_blocks):
        ug, uv = up_next
        if j + 1 < n_col_blocks:
            up_next = up(j + 1)
        cw = cw_ref[:, col_block(j)]
        conv = []
        for b in range(seqs):
            ug_b = ug[b * tm:(b + 1) * tm]
            ugs_ref[b, 0:SUBLANES, :] = carry_ref[j, b]
            ugs_ref[b, SUBLANES:SUBLANES + tm, :] = ug_b
            carry_ref[j, b] = ug_b[tm - SUBLANES:tm, :]
            cvnew_ref[b, :, col_block(j)] = ug_b[tm - (CONV_W - 1):tm, :]
            conv.append(cw[0:1, :] * ugs_ref[b, SUBLANES - 2:SUBLANES - 2 + tm, :]
                        + cw[1:2, :] * ugs_ref[b, SUBLANES - 1:SUBLANES - 1 + tm, :])
        c = cb_ref[:, col_block(j)] + jnp.concatenate(conv, axis=0) + cw[2:3, :] * ug
        hid = (_gelu_exact(c) * uv).astype(BF16)
        acc = acc + _dot(hid, wo_ref[col_block(j), :])
    h2 = h1 + acc
    gate = _sigmoid(_dot(_rmsnorm(h2, nple_ref[...]).astype(BF16), pgate_ref[...]))
    h3 = h2 + gate * _dot(flat(p_ref).astype(BF16), pproj_ref[...])
    if apply_final_norm:
        h3 = _rmsnorm(h3, fnorm_ref[...])
    hout_ref[...] = h3.reshape(hout_ref.shape)


def _post(h, o_hg, o_rw, p_all, layer, cv_prev, w_out, norm_ffn, ffn_in, cw, cb, wo, norm_ple,
          ple_gate, ple_proj, final_norm, apply_final_norm):
    bsz, t_len, d = h.shape
    tm = min(POST_TILE, t_len)
    seqs = min(bsz, POST_TILE // tm)
    assert t_len % tm == 0 and tm % SUBLANES == 0 and bsz % seqs == 0
    d_ff = cb.shape[1]
    n_col_blocks = d_ff // FF_COLS
    assert d_ff % FF_COLS == 0
    kern = functools.partial(_post_kernel, seqs=seqs, tm=tm, n_col_blocks=n_col_blocks,
                             apply_final_norm=apply_final_norm)
    tok = lambda w: pl.BlockSpec((seqs, tm, w), lambda b, t: (b, t, 0))
    p_spec = pl.BlockSpec((None, seqs, tm, p_all.shape[-1]), lambda b, t: (layer, b, t, 0))
    cv_spec = pl.BlockSpec((seqs, CONV_W - 1, d_ff), lambda b, t: (b, 0, 0))
    consts = [w_out, norm_ffn, ffn_in, cw, cb, wo, norm_ple, ple_gate, ple_proj, final_norm]
    return pl.pallas_call(
        kern,
        grid=(bsz // seqs, t_len // tm),
        in_specs=[tok(d), tok(o_hg.shape[2]), tok(o_rw.shape[2]), p_spec, cv_spec]
                 + [_const_spec(a.shape) for a in consts],
        out_specs=[tok(d), cv_spec],
        out_shape=[jax.ShapeDtypeStruct(h.shape, F32), jax.ShapeDtypeStruct(cv_prev.shape, F32)],
        scratch_shapes=[pltpu.VMEM((n_col_blocks, seqs, SUBLANES, FF_COLS), F32),
                        pltpu.VMEM((seqs, SUBLANES + tm, FF_COLS), F32)],
        compiler_params=pltpu.CompilerParams(dimension_semantics=("parallel", "arbitrary"),
                                             vmem_limit_bytes=VMEM_LIMIT),
        name="post",
    )(h, o_hg, o_rw, p_all, cv_prev, *consts)


def _layer_params(i, hg_proj, lb_all, w):
    rw_width = w["rw_w0"].shape[1]
    row = lambda a: a.reshape(1, -1)
    lb = jnp.maximum(lb_all[i], 0.0)
    zeros_lora = jnp.zeros((RW_LORA_W, rw_width), F32)
    return dict(
        norm_mix=row(w["norm_mix"][i]),
        w_in=w["w_in"][i].astype(BF16),
        log_lb=row(jnp.log(lb)), one_m_lb=row(1.0 - lb), hg_norm=row(w["hg_norm"][i]),
        mu=row(w["rw_mu"][i]), w0=row(w["rw_w0"][i]),
        w2p=jnp.concatenate([w["rw_w2"][i], zeros_lora], axis=0).astype(BF16),
        a0=row(w["rw_a0"][i]),
        a2p=jnp.concatenate([zeros_lora, w["rw_a2"][i]], axis=0).astype(BF16),
        g2=w["rw_g2"][i].astype(BF16),
        k_k=row(w["rw_k_k"][i]), k_a=row(w["rw_k_a"][i]), r_k=row(w["rw_r_k"][i]),
        ln_w=row(w["rw_ln_w"][i]), ln_b=row(w["rw_ln_b"][i]),
        w_out=w["w_out"][i].astype(BF16),
        norm_ffn=row(w["norm_ffn"][i]),
        ffn_in=w["ffn_in"][i].astype(BF16),
        cw=w["ffn_conv_w"][i], cb=row(w["ffn_conv_b"][i]),
        wo=w["ffn_out"][i].astype(BF16),
        norm_ple=row(w["norm_ple"][i]),
        ple_gate=w["ple_gate"][i].astype(BF16), ple_proj=w["ple_proj"][i].astype(BF16),
        final_norm=row(w["final_norm"]),
    )


def _trunk(x, p, s_hg, s_rw, s_sh, s_cv, layers):
    bsz, t_len, d = x.shape
    depth = len(layers)
    h = x
    new_hg, new_rw, new_sh, new_cv = [], [], [], []
    for i, lp in enumerate(layers):
        z_hg, z_rw = _in_proj(h.reshape(bsz * t_len, d), lp["norm_mix"], lp["w_in"], 4 * lp["hg_norm"].shape[1])
        z_hg = z_hg.reshape(bsz, t_len, -1)
        z_rw = z_rw.reshape(bsz, t_len, -1)
        o_hg, n_hg, o_rw, n_rw = _mixers(
            z_hg, z_rw, lp["log_lb"], lp["one_m_lb"], lp["hg_norm"], s_hg[i], s_sh[i][:, None, :],
            s_rw[i], lp["mu"], lp["w0"], lp["w2p"], lp["a0"], lp["a2p"], lp["g2"], lp["k_k"],
            lp["k_a"], lp["r_k"], lp["ln_w"], lp["ln_b"])
        h, n_cv = _post(h, o_hg, o_rw, p, i, s_cv[i], lp["w_out"], lp["norm_ffn"],
                        lp["ffn_in"], lp["cw"], lp["cb"], lp["wo"], lp["norm_ple"],
                        lp["ple_gate"], lp["ple_proj"], lp["final_norm"], i == depth - 1)
        new_hg.append(n_hg)
        new_rw.append(n_rw)
        new_sh.append(z_rw[:, -1])
        new_cv.append(n_cv)
    return h, jnp.stack(new_hg), jnp.stack(new_rw), jnp.stack(new_sh), jnp.stack(new_cv)


def kernel(x_prompt, x_sample, state_hgrn, state_rwkv, state_shift, state_conv, p_prompt, p_sample,
           norm_mix, w_in, lb_raw, hg_norm, rw_mu, rw_w0, rw_w2, rw_a0, rw_a2, rw_g2, rw_k_k, rw_k_a,
           rw_r_k, rw_ln_w, rw_ln_b, w_out, norm_ffn, ffn_in, ffn_conv_w, ffn_conv_b, ffn_out,
           norm_ple, ple_gate, ple_proj, final_norm):
    w = dict(norm_mix=norm_mix, w_in=w_in, hg_norm=hg_norm, rw_mu=rw_mu, rw_w0=rw_w0, rw_w2=rw_w2,
             rw_a0=rw_a0, rw_a2=rw_a2, rw_g2=rw_g2, rw_k_k=rw_k_k, rw_k_a=rw_k_a,
             rw_r_k=rw_r_k.reshape(rw_r_k.shape[0], -1), rw_ln_w=rw_ln_w, rw_ln_b=rw_ln_b, w_out=w_out,
             norm_ffn=norm_ffn, ffn_in=ffn_in, ffn_conv_w=ffn_conv_w, ffn_conv_b=ffn_conv_b,
             ffn_out=ffn_out, norm_ple=norm_ple, ple_gate=ple_gate, ple_proj=ple_proj,
             final_norm=final_norm)
    depth = w_in.shape[0]
    hg_proj = 4 * hg_norm.shape[1]
    lb_all = jnp.cumsum(jax.nn.softmax(lb_raw.astype(F32), axis=0), axis=0)
    lb_all = lb_all - lb_all[:1]
    layers = [_layer_params(i, hg_proj, lb_all, w) for i in range(depth)]

    nb = x_prompt.shape[0]
    zeros_like_state = lambda s: jnp.zeros((depth, nb) + s.shape[2:], s.dtype)
    y_p, hg_p, rw_p, sh_p, cv_p = _trunk(x_prompt, p_prompt, zeros_like_state(state_hgrn),
                                         zeros_like_state(state_rwkv), zeros_like_state(state_shift),
                                         zeros_like_state(state_conv), layers)
    y_s, hg_s, rw_s, sh_s, cv_s = _trunk(x_sample, p_sample, state_hgrn, state_rwkv, state_shift,
                                         state_conv, layers)
    return (y_p, y_s, hg_p, rw_p, sh_p, cv_p, hg_s, rw_s, sh_s, cv_s)
```

```python
import functools

import jax
import jax.numpy as jnp
from jax import lax
from jax.experimental import pallas as pl
from jax.experimental.pallas import tpu as pltpu

F32 = jnp.float32
BF16 = jnp.bfloat16

NORM_EPS = 1e-6
RW_GN_EPS = 64e-5
HG_HEAD_DIM = 128
RW_HEAD_DIM = 64
RW_LORA_W = 64
CONV_W = 3
HG_BLOCK = 16
RW_CHUNK = 64
RW_SUB = 16
MIX_TILE = 256
FF_COLS = 256
POST_TILE = 512
LANES = 128
SUBLANES = 8
VMEM_LIMIT = 56 * 1024 * 1024

NN = ((1,), (0,))
NT = ((1,), (1,))
TN = ((0,), (0,))


def _dot(a, b, dims=NN):
    return lax.dot_general(a, b, (dims, ((), ())), preferred_element_type=F32)


def _dot1(a, b, dims=NN):
    return _dot(a.astype(BF16), b.astype(BF16), dims)


def _split2(x):
    hi = x.astype(BF16)
    lo = (x - hi.astype(F32)).astype(BF16)
    return hi, lo


def _dot_xl2(a_exact, x, dims=NN):
    hi, lo = _split2(x)
    return _dot(a_exact, hi, dims) + _dot(a_exact, lo, dims)


def _dot_xr2(x, b, dims=NN):
    hi, lo = _split2(x)
    b = b.astype(BF16)
    return _dot(hi, b, dims) + _dot(lo, b, dims)


def _head_sums(x, pair_ones):
    n, width = x.shape
    tiles = [x[:, p * LANES:(p + 1) * LANES] for p in range(width // LANES)]
    sums = _dot(jnp.concatenate(tiles, axis=0).astype(BF16), pair_ones)
    return jnp.concatenate([sums[p * n:(p + 1) * n] for p in range(width // LANES)], axis=1)


def _sigmoid(x):
    return 0.5 * jnp.tanh(0.5 * x) + 0.5


def _softplus(x):
    return jnp.maximum(x, 0.0) + jnp.log(1.0 + jnp.exp(-jnp.abs(x)))


def _rmsnorm(x, g):
    return x * lax.rsqrt(jnp.mean(x * x, axis=-1, keepdims=True) + NORM_EPS) * g


def _iota2(shape, axis):
    return lax.broadcasted_iota(jnp.int32, shape, axis)


def _const_spec(shape):
    zeros = (0,) * len(shape)
    return pl.BlockSpec(shape, lambda *_: zeros, pipeline_mode=pl.Buffered(1))


def _map(fn, *lists):
    return [fn(*xs) for xs in zip(*lists)]


def _inproj_kernel(x_ref, g_ref, w_ref, zhg_ref, zrw_ref):
    hg_proj = zhg_ref.shape[1]
    xn = _rmsnorm(x_ref[...], g_ref[...]).astype(BF16)
    zhg_ref[...] = _dot(xn, w_ref[:, 0:hg_proj])
    zrw_ref[...] = _dot(xn, w_ref[:, hg_proj:])


def _in_proj(x2, g, w_in, hg_proj):
    n, d = x2.shape
    tm = min(512, n)
    assert n % tm == 0 and hg_proj % LANES == 0
    rw_proj = w_in.shape[1] - hg_proj
    return pl.pallas_call(
        _inproj_kernel,
        grid=(n // tm,),
        in_specs=[pl.BlockSpec((tm, d), lambda i: (i, 0)),
                  _const_spec((1, d)),
                  _const_spec(w_in.shape)],
        out_specs=[pl.BlockSpec((tm, hg_proj), lambda i: (i, 0)),
                   pl.BlockSpec((tm, rw_proj), lambda i: (i, 0))],
        out_shape=[jax.ShapeDtypeStruct((n, hg_proj), F32),
                   jax.ShapeDtypeStruct((n, rw_proj), F32)],
        compiler_params=pltpu.CompilerParams(dimension_semantics=("parallel",),
                                             vmem_limit_bytes=VMEM_LIMIT),
        name="in_proj",
    )(x2, g, w_in)


HG_Q, HG_K, HG_BL, HG_V, HG_GATE = range(5)


def _hgrn_front(z_ref, loglb_ref, omlb_ref, gn_ref, pre_ref, p_ref, *, n_heads, tile):
    dk = HG_HEAD_DIM
    width = n_heads * dk
    blk = HG_BLOCK
    zq = z_ref[0, :, 0:width]
    zf = z_ref[0, :, width:2 * width]
    zg = z_ref[0, :, 3 * width:4 * width]
    pre_ref[HG_Q] = zq * _sigmoid(zq)
    pre_ref[HG_V] = z_ref[0, :, 2 * width:3 * width]
    pre_ref[HG_GATE] = gn_ref[...] * (zg * _sigmoid(zg))
    soft = jnp.log(1.0 + jnp.exp(-jnp.abs(zf)))
    log_sig_pos = jnp.minimum(zf, 0.0) - soft
    log_sig_neg = jnp.minimum(-zf, 0.0) - soft
    lo_term = loglb_ref[...] + log_sig_neg
    top = jnp.maximum(log_sig_pos, lo_term)
    logf = top + jnp.log(1.0 + jnp.exp(-jnp.abs(log_sig_pos - lo_term)))
    pre_ref[HG_K] = omlb_ref[...] * jnp.exp(log_sig_neg)
    row = _iota2((tile, tile), 0)
    col = _iota2((tile, tile), 1)
    tri = ((row // blk == col // blk) & (col <= row)).astype(BF16)
    pre_ref[HG_BL] = _dot_xl2(tri, logf)
    yield

    t_row = _iota2((blk, dk), 0)
    for i in range(tile // blk):
        rows = slice(i * blk, (i + 1) * blk)
        for h in range(n_heads):
            lanes = slice(h * dk, (h + 1) * dk)
            qb = pre_ref[HG_Q, rows, lanes]
            kb = pre_ref[HG_K, rows, lanes]
            bb = pre_ref[HG_BL, rows, lanes]
            p0 = (i * n_heads + h) * blk
            for s in range(blk):
                causal = t_row >= s
                ps = jnp.where(causal, qb * kb[s:s + 1, :] * jnp.exp(bb - bb[s:s + 1, :]), 0.0)
                p_ref[p0:p0 + blk, s * dk:(s + 1) * dk] = ps.astype(BF16)
            yield


def _hgrn_back(sel_ref, pre_ref, p_ref, a_ref, st_ref, o_ref, *, n_heads, tile):
    dk = HG_HEAD_DIM
    blk = HG_BLOCK
    n_blocks = tile // blk
    a_ref[...] = _dot(p_ref[...], sel_ref[...])
    yield

    block_rows = [slice(i * blk, (i + 1) * blk) for i in range(n_blocks)]
    e_end = [pre_ref[HG_BL, (i + 1) * blk - 1:(i + 1) * blk, :] for i in range(n_blocks)]
    start = [jnp.zeros_like(e_end[0])]
    for i in range(n_blocks):
        start.append(start[i] + e_end[i])
    qe = [pre_ref[HG_Q, r, :] * jnp.exp(pre_ref[HG_BL, r, :]) for r in block_rows]
    ke = [pre_ref[HG_K, r, :] * jnp.exp(e - pre_ref[HG_BL, r, :])
          for r, e in zip(block_rows, e_end)]
    vv = [pre_ref[HG_V, r, :] for r in block_rows]
    q_tile = jnp.concatenate([x * jnp.exp(s) for x, s in zip(qe, start)], axis=0)
    k_tile = jnp.concatenate([x * jnp.exp(start[n_blocks] - s) for x, s in zip(ke, start[1:])], axis=0)
    v_tile = pre_ref[HG_V]
    decay_tile = jnp.exp(start[n_blocks])
    yield

    heads = range(n_heads)
    lanes = [slice(h * dk, (h + 1) * dk) for h in heads]
    st = [st_ref[h] for h in heads]
    new_st = [st[h] * decay_tile[:, lanes[h]] + _dot1(v_tile[:, lanes[h]], k_tile[:, lanes[h]], TN)
              for h in heads]
    yield
    o_state = [_dot1(q_tile[:, lanes[h]], st[h], NT) for h in heads]
    o_blocks = [[o_state[h][r] for r in block_rows] for h in heads]
    yield
    span = n_blocks // 2
    while span >= 1:
        for lo in range(0, n_blocks, 2 * span):
            mid, hi = lo + span, lo + 2 * span
            q_r = jnp.concatenate([qe[i] if i == mid else qe[i] * jnp.exp(start[i] - start[mid])
                                   for i in range(mid, hi)], axis=0)
            k_l = jnp.concatenate([ke[j] if j == mid - 1 else ke[j] * jnp.exp(start[mid] - start[j + 1])
                                   for j in range(lo, mid)], axis=0)
            v_l = jnp.concatenate(vv[lo:mid], axis=0)
            scores = [_dot1(q_r[:, lanes[h]], k_l[:, lanes[h]], NT) for h in heads]
            upd = [_dot1(scores[h], v_l[:, lanes[h]]) for h in heads]
            for h in heads:
                for n, i in enumerate(range(mid, hi)):
                    o_blocks[h][i] = o_blocks[h][i] + upd[h][n * blk:(n + 1) * blk]
            yield
        span //= 2
    for i in range(n_blocks):
        for h in heads:
            p0 = (i * n_heads + h) * blk
            o_blocks[h][i] = o_blocks[h][i] + _dot1(a_ref[p0:p0 + blk, 0:blk], vv[i][:, lanes[h]])
        yield
    for h in heads:
        o = jnp.concatenate(o_blocks[h], axis=0)
        o = o * lax.rsqrt(jnp.mean(o * o, axis=-1, keepdims=True) + NORM_EPS)
        o_ref[0, :, lanes[h]] = (o * pre_ref[HG_GATE, :, lanes[h]]).astype(o_ref.dtype)
    for h in heads:
        st_ref[h] = new_st[h]


RW_KT, RW_RT, RW_V, RW_BH, RW_KH, RW_BL, RW_KL, RW_C, RW_RK, RW_G = range(10)


def _block_diag(y):
    lo = _iota2(y.shape, 1) < RW_HEAD_DIM
    y = y.astype(BF16)
    zero = jnp.zeros_like(y)
    return jnp.concatenate([jnp.where(lo, y, zero), jnp.where(lo, zero, y)], axis=0)


def _pair_dot(x, y):
    return _dot(x.astype(BF16), _block_diag(y))


def _pair_dot_nt(x, y):
    return _dot(x.astype(BF16), _block_diag(y), NT)


def _pair_dot2(x, y, z):
    return _dot(x.astype(BF16), jnp.concatenate([_block_diag(y), _block_diag(z)], axis=1))


def _pair_dot2_nt(x, y, z):
    return _dot(x.astype(BF16), jnp.concatenate([_block_diag(y), _block_diag(z)], axis=0), NT)


def _unit_lower_inverses(n_list, eye, same_sub):
    d = [jnp.where(same_sub, n, 0.0) for n in n_list]
    l_off = _map(lambda n, dd: n - dd, n_list, d)
    t_d = [eye - dd for dd in d]
    n = n_list[0].shape[0]
    d_pow = _map(_pair_dot, d, d)
    yield
    for _ in range(RW_SUB.bit_length() - 3):
        both = _map(lambda dp, t: _pair_dot(jnp.concatenate([dp, t], axis=0), dp), d_pow, t_d)
        t_d = _map(lambda t, b: t + b[n:], t_d, both)
        d_pow = [b[0:n] for b in both]
        yield
    t_d = _map(lambda t, dp: t + _pair_dot(t, dp), t_d, d_pow)
    yield
    p = _map(_pair_dot, t_d, l_off)
    yield
    p2 = _map(_pair_dot, p, p)
    yield
    t_p = _map(lambda pp, pp2: (eye - pp) + _pair_dot(eye - pp, pp2), p, p2)
    yield
    return _map(_pair_dot, t_p, t_d)


def _rwkv_front(z_ref, mu_ref, w0_ref, w2_ref, a0_ref, a2_ref, g2_ref, kk_ref, ka_ref, rk_ref,
                hones_ref, pre_ref, carry_ref, zs_ref, *, n_heads, chunk, tile):
    hd = RW_HEAD_DIM
    width = n_heads * hd
    z = z_ref[0]
    zs_ref[0:SUBLANES, :] = carry_ref[...]
    zs_ref[SUBLANES:SUBLANES + tile, :] = z
    carry_ref[...] = z[tile - SUBLANES:tile, :]
    z_prev = zs_ref[SUBLANES - 1:SUBLANES - 1 + tile, :]
    zm = z + (z_prev - z) * mu_ref[...]

    r = zm[:, 0:width]
    k = zm[:, width:2 * width]
    v = zm[:, 2 * width:3 * width]
    lora_in = zm[:, 3 * width:3 * width + 2 * RW_LORA_W]
    gd = zm[:, 3 * width + 2 * RW_LORA_W:]
    pre_ref[RW_V] = v
    pre_ref[RW_G] = _dot(_sigmoid(gd).astype(BF16), g2_ref[...])
    yield

    w_pre = w0_ref[...] + _dot_xr2(jnp.tanh(lora_in), w2_ref[...])
    w_raw = -_softplus(-w_pre) - 0.5
    logw = -jnp.exp(w_raw)
    a = _sigmoid(a0_ref[...] + _dot1(lora_in, a2_ref[...]))
    kk = k * kk_ref[...]
    kk = kk * jnp.minimum(lax.rsqrt(_head_sums(kk * kk, hones_ref[...])), 1e12)
    k2 = k * (1.0 + (a - 1.0) * ka_ref[...])
    beta = kk * a
    pre_ref[RW_RK] = r * k2 * rk_ref[...]
    yield

    trow = _iota2((tile, tile), 0)
    tcol = _iota2((tile, tile), 1)
    tri = ((trow // chunk == tcol // chunk) & (tcol <= trow)).astype(BF16)
    c = _dot_xl2(tri, logw)
    pre_ref[RW_C] = c
    e_neg = jnp.exp(-c)
    pre_ref[RW_KT] = kk * jnp.exp(c - logw)
    pre_ref[RW_RT] = r * jnp.exp(c)
    pre_ref[RW_BH] = beta * e_neg
    pre_ref[RW_KH] = k2 * e_neg
    yield
    for ci in range(tile // chunk):
        rows = slice(ci * chunk, (ci + 1) * chunk)
        e_end = jnp.exp(c[rows][chunk - 1:chunk, :] - c[rows])
        pre_ref[RW_BL, rows, :] = beta[rows] * e_end
        pre_ref[RW_KL, rows, :] = k2[rows] * e_end
    yield


def _rwkv_back(pre_ref, hones_ref, lnw_ref, lnb_ref, s_ref, y_ref, o_ref, *, n_heads, chunk, tile):
    hd = RW_HEAD_DIM
    n_chunks = tile // chunk
    pw = 2 * hd
    pairs = range(n_heads // 2)
    lanes = [slice(p * pw, (p + 1) * pw) for p in pairs]
    row = _iota2((chunk, pw), 0)
    lane = _iota2((chunk, pw), 1)
    lo_half = lane < hd
    col = jnp.where(lo_half, lane, lane - hd)
    incl2 = jnp.concatenate([col < row, col <= row], axis=0)
    incl4 = jnp.concatenate([incl2, incl2], axis=1)
    eye = (row == col).astype(F32)
    same_sub = (row // RW_SUB) == (col // RW_SUB)
    same_head = (_iota2((pw, pw), 0) // hd) == (_iota2((pw, pw), 1) // hd)

    chunk_rows = [slice(ci * chunk, (ci + 1) * chunk) for ci in range(n_chunks)]
    units = [(ci, p) for ci in range(n_chunks) for p in pairs]
    at = lambda name: [pre_ref[name, chunk_rows[ci], lanes[p]] for ci, p in units]
    kt, rt, vv, b_l, k_l = at(RW_KT), at(RW_RT), at(RW_V), at(RW_BL), at(RW_KL)
    g_last = [jnp.exp(pre_ref[RW_C, (ci + 1) * chunk - 1:(ci + 1) * chunk, lanes[p]]) for ci, p in units]
    lhs = _map(lambda x, y: jnp.concatenate([x, y], axis=0), kt, rt)
    g_bk = [jnp.where(incl4, _pair_dot2_nt(l, xb, xk), 0.0)
            for l, xb, xk in zip(lhs, at(RW_BH), at(RW_KH))]
    a_ab = [x[0:chunk, 0:pw] for x in g_bk]
    a_rb = [x[chunk:, 0:pw] for x in g_bk]
    yield
    gkv = [_pair_dot(x[:, pw:], v_) for x, v_ in zip(g_bk, vv)]
    x_loc = [x[0:chunk] for x in gkv]
    yield
    t_inv = yield from _unit_lower_inverses(a_ab, eye, same_sub)
    yield
    wu = _map(_pair_dot2, t_inv, kt, x_loc)
    w_t = [x[:, 0:pw] for x in wu]
    u_loc = [x[:, pw:] for x in wu]
    yield
    a_wu = _map(_pair_dot2, a_rb, w_t, u_loc)
    q_c = _map(lambda r_, x: r_ - x[:, 0:pw], rt, a_wu)
    y_loc = _map(lambda x, y: x[chunk:] - y[:, pw:], gkv, a_wu)
    yield
    m_off = _map(lambda w_, bl_: jnp.where(same_head, -_dot1(w_, bl_, TN), 0.0), w_t, b_l)
    yield
    c_full = _map(lambda v_, u_, kl_, bl_: _dot1(jnp.concatenate([v_, -u_], axis=0),
                                                 jnp.concatenate([kl_, bl_], axis=0), TN),
                  vv, u_loc, k_l, b_l)
    c_add = [jnp.where(lo_half, x[0:hd], x[hd:]) for x in c_full]
    yield
    state = [s_ref[p] for p in pairs]
    for idx, (ci, p) in enumerate(units):
        y_ref[chunk_rows[ci], lanes[p]] = _pair_dot_nt(q_c[idx], state[p]) + y_loc[idx]
        state[p] = state[p] * g_last[idx] + _dot1(state[p], m_off[idx]) + c_add[idx]
    for p in pairs:
        s_ref[p] = state[p]
    yield

    y = y_ref[...]
    head_ones = hones_ref[...]
    mean = _head_sums(y, head_ones) * (1.0 / hd)
    yc = y - mean
    var = _head_sums(yc * yc, head_ones) * (1.0 / hd)
    yn = yc * lax.rsqrt(var + RW_GN_EPS) * lnw_ref[...] + lnb_ref[...]
    bonus = _head_sums(pre_ref[RW_RK], head_ones) * pre_ref[RW_V]
    o_ref[0] = ((yn + bonus) * pre_ref[RW_G]).astype(o_ref.dtype)


_DONE = object()
MIX_STEPS = (3, 2, 1, 4)


def _mixer_kernel(zhg_ref, zrw_ref, loglb_ref, omlb_ref, gn_ref, sel_ref, s0hg_ref,
                  shift_ref, s0rw_ref, mu_ref, w0_ref, w2_ref, a0_ref, a2_ref, g2_ref,
                  kk_ref, ka_ref, rk_ref, lnw_ref, lnb_ref, hones_ref,
                  ohg_ref, sfin_hg_ref, orw_ref, sfin_rw_ref,
                  st_ref, s_ref, carry_ref, zs_ref, y_ref, a_ref,
                  hg_pre_a, hg_pre_b, p_a, p_b, rw_pre_a, rw_pre_b,
                  *, hg_heads, rw_heads, tile, n_tiles, pipelined):
    k = pl.program_id(0)
    k_back = k - 1 if pipelined else k
    hg = dict(n_heads=hg_heads, tile=tile)
    rw = dict(n_heads=rw_heads, chunk=RW_CHUNK, tile=tile)

    @pl.when(k % n_tiles == 0)
    def _():
        carry_ref[...] = jnp.broadcast_to(shift_ref[0], carry_ref.shape)

    @pl.when((k_back % n_tiles == 0) | (k == 0))
    def _():
        for h in range(hg_heads):
            st_ref[h] = s0hg_ref[0, h].T
        s_ref[...] = s0rw_ref[0]

    if pipelined:
        @pl.when(k == 0)
        def _():
            hg_pre_b[...] = jnp.zeros(hg_pre_b.shape, hg_pre_b.dtype)
            p_b[...] = jnp.zeros(p_b.shape, p_b.dtype)
            rw_pre_b[...] = jnp.zeros(rw_pre_b.shape, rw_pre_b.dtype)

    def fronts(hg_w, p_w, rw_w):
        return [_rwkv_front(zrw_ref, mu_ref, w0_ref, w2_ref, a0_ref, a2_ref, g2_ref, kk_ref, ka_ref,
                            rk_ref, hones_ref, rw_w, carry_ref, zs_ref, **rw),
                _hgrn_front(zhg_ref, loglb_ref, omlb_ref, gn_ref, hg_w, p_w, **hg)]

    def backs(hg_r, p_r, rw_r):
        return [_rwkv_back(rw_r, hones_ref, lnw_ref, lnb_ref, s_ref, y_ref, orw_ref, **rw),
                _hgrn_back(sel_ref, hg_r, p_r, a_ref, st_ref, ohg_ref, **hg)]

    def run(bodies, steps):
        live = [True] * len(bodies)
        while any(live):
            for n, body in enumerate(bodies):
                for _ in range(steps[n]):
                    if live[n]:
                        live[n] = next(body, _DONE) is not _DONE

    if not pipelined:
        run(fronts(hg_pre_a, p_a, rw_pre_a), MIX_STEPS[2:])
        run(backs(hg_pre_a, p_a, rw_pre_a), MIX_STEPS[:2])
    else:
        @pl.when(k % 2 == 0)
        def _():
            run(backs(hg_pre_b, p_b, rw_pre_b) + fronts(hg_pre_a, p_a, rw_pre_a), MIX_STEPS)

        @pl.when(k % 2 == 1)
        def _():
            run(backs(hg_pre_a, p_a, rw_pre_a) + fronts(hg_pre_b, p_b, rw_pre_b), MIX_STEPS)

    @pl.when((k_back >= 0) & (k_back % n_tiles == n_tiles - 1))
    def _():
        for h in range(hg_heads):
            sfin_hg_ref[0, h] = st_ref[h].T
        sfin_rw_ref[0] = s_ref[...]


def _mixers(z_hg, z_rw, log_lb, one_m_lb, hg_norm, s0_hg, shift_prev, s0_rw,
            mu, w0, w2p, a0, a2p, g2, k_k, k_a, r_k, ln_w, ln_b):
    bsz, t_len, hg_proj = z_hg.shape
    rw_proj = z_rw.shape[2]
    hg_width = hg_proj // 4
    hg_heads = hg_width // HG_HEAD_DIM
    rw_heads = s0_rw.shape[1]
    hd = RW_HEAD_DIM
    rw_width = rw_heads * hd
    tile = min(MIX_TILE, t_len)
    assert t_len % tile == 0 and tile % RW_CHUNK == 0 and tile % HG_BLOCK == 0 and rw_heads % 2 == 0
    n_tiles = t_len // tile
    n_pairs = rw_heads // 2
    s0_rw = s0_rw.reshape(bsz, n_pairs, 2, hd, hd).transpose(0, 1, 3, 2, 4).reshape(bsz, n_pairs, hd, 2 * hd)
    group = jnp.arange(HG_BLOCK * HG_HEAD_DIM, dtype=jnp.int32)[:, None] // HG_HEAD_DIM
    selector = (group == jnp.arange(LANES, dtype=jnp.int32)[None, :]).astype(BF16)
    head_of = jnp.arange(LANES, dtype=jnp.int32) // hd
    head_ones = (head_of[:, None] == head_of[None, :]).astype(BF16)

    assert RW_CHUNK == RW_HEAD_DIM
    pipelined = n_tiles > 1
    total = bsz * n_tiles
    front_k = lambda k: jnp.minimum(k, total - 1)
    back_k = (lambda k: jnp.maximum(k - 1, 0)) if pipelined else front_k
    tok_in = lambda w: pl.BlockSpec((1, tile, w), lambda k: (front_k(k) // n_tiles, front_k(k) % n_tiles, 0))
    tok_out = lambda w: pl.BlockSpec((1, tile, w), lambda k: (back_k(k) // n_tiles, back_k(k) % n_tiles, 0))
    per_seq = lambda a, which: pl.BlockSpec((1,) + a.shape[1:],
                                            lambda k: (which(k) // n_tiles,) + (0,) * (a.ndim - 1))
    consts_hg = [log_lb, one_m_lb, hg_norm, selector]
    consts_rw = [mu, w0, w2p, a0, a2p, g2, k_k, k_a, r_k, ln_w, ln_b, head_ones]
    kern = functools.partial(_mixer_kernel, hg_heads=hg_heads, rw_heads=rw_heads, tile=tile,
                             n_tiles=n_tiles, pipelined=pipelined)
    o_hg, n_hg, o_rw, n_rw = pl.pallas_call(
        kern,
        grid=(total + 1 if pipelined else total,),
        in_specs=[tok_in(hg_proj), tok_in(rw_proj)]
                 + [_const_spec(a.shape) for a in consts_hg] + [per_seq(s0_hg, back_k)]
                 + [per_seq(shift_prev, front_k), per_seq(s0_rw, back_k)]
                 + [_const_spec(a.shape) for a in consts_rw],
        out_specs=[tok_out(hg_width), per_seq(s0_hg, back_k), tok_out(rw_width), per_seq(s0_rw, back_k)],
        out_shape=[jax.ShapeDtypeStruct((bsz, t_len, hg_width), BF16),
                   jax.ShapeDtypeStruct(s0_hg.shape, F32),
                   jax.ShapeDtypeStruct((bsz, t_len, rw_width), BF16),
                   jax.ShapeDtypeStruct(s0_rw.shape, F32)],
        scratch_shapes=[pltpu.VMEM((hg_heads, HG_HEAD_DIM, HG_HEAD_DIM), F32),
                        pltpu.VMEM((n_pairs, hd, 2 * hd), F32),
                        pltpu.VMEM((SUBLANES, rw_proj), F32),
                        pltpu.VMEM((SUBLANES + tile, rw_proj), F32),
                        pltpu.VMEM((tile, rw_width), F32),
                        pltpu.VMEM((tile * hg_heads, LANES), F32)]
                       + [pltpu.VMEM((5, tile, hg_width), F32)] * 2
                       + [pltpu.VMEM((tile * hg_heads, HG_BLOCK * HG_HEAD_DIM), BF16)] * 2
                       + [pltpu.VMEM((10, tile, rw_width), F32)] * 2,
        compiler_params=pltpu.CompilerParams(dimension_semantics=("arbitrary",),
                                             vmem_limit_bytes=VMEM_LIMIT),
        name="mixers",
    )(z_hg, z_rw, *consts_hg, s0_hg, shift_prev, s0_rw, *consts_rw)
    n_rw = n_rw.reshape(bsz, n_pairs, hd, 2, hd).transpose(0, 1, 3, 2, 4).reshape(bsz, rw_heads, hd, hd)
    return o_hg, n_hg, o_rw, n_rw


def _gelu_exact(x):
    return 0.5 * x * (1.0 + lax.erf(x * (2.0 ** -0.5)))


def _post_kernel(h_ref, ohg_ref, orw_ref, p_ref, cvprev_ref,
                 wout_ref, nffn_ref, win_ref, cw_ref, cb_ref, wo_ref,
                 nple_ref, pgate_ref, pproj_ref, fnorm_ref,
                 hout_ref, cvnew_ref, carry_ref, ugs_ref, *, seqs, tm, n_col_blocks, apply_final_norm):
    t_idx = pl.program_id(1)
    cols = FF_COLS
    d_ff = n_col_blocks * cols
    hg_width = ohg_ref.shape[-1]
    rows = seqs * tm
    flat = lambda ref: ref[...].reshape(rows, ref.shape[-1])
    col_block = lambda j: slice(j * cols, (j + 1) * cols)
    up = lambda j: (_dot(xn, win_ref[:, col_block(j)]),
                    _dot(xn, win_ref[:, d_ff + j * cols:d_ff + (j + 1) * cols]))

    @pl.when(t_idx == 0)
    def _():
        carry_ref[...] = jnp.zeros(carry_ref.shape, carry_ref.dtype)
        for j in range(n_col_blocks):
            for b in range(seqs):
                carry_ref[j, b, SUBLANES - (CONV_W - 1):SUBLANES, :] = cvprev_ref[b, :, col_block(j)]

    h1 = (flat(h_ref) + _dot(flat(ohg_ref).astype(BF16), wout_ref[0:hg_width, :])
          + _dot(flat(orw_ref).astype(BF16), wout_ref[hg_width:, :]))
    xn = _rmsnorm(h1, nffn_ref[...]).astype(BF16)
    acc = jnp.zeros_like(h1)
    up_next = up(0)
    for j in range(n_col_blocks):
        ug, uv = up_next
        if j + 1 < n_col_blocks:
            up_next = up(j + 1)
        cw = cw_ref[:, col_block(j)]
        conv = []
        for b in range(seqs):
            ug_b = ug[b * tm:(b + 1) * tm]
            ugs_ref[b, 0:SUBLANES, :] = carry_ref[j, b]
            ugs_ref[b, SUBLANES:SUBLANES + tm, :] = ug_b
            carry_ref[j, b] = ug_b[tm - SUBLANES:tm, :]
            cvnew_ref[b, :, col_block(j)] = ug_b[tm - (CONV_W - 1):tm, :]
            conv.append(cw[0:1, :] * ugs_ref[b, SUBLANES - 2:SUBLANES - 2 + tm, :]
                        + cw[1:2, :] * ugs_ref[b, SUBLANES - 1:SUBLANES - 1 + tm, :])
        c = cb_ref[:, col_block(j)] + jnp.concatenate(conv, axis=0) + cw[2:3, :] * ug
        hid = (_gelu_exact(c) * uv).astype(BF16)
        acc = acc + _dot(hid, wo_ref[col_block(j), :])
    h2 = h1 + acc
    gate = _sigmoid(_dot(_rmsnorm(h2, nple_ref[...]).astype(BF16), pgate_ref[...]))
    h3 = h2 + gate * _dot(flat(p_ref).astype(BF16), pproj_ref[...])
    if apply_final_norm:
        h3 = _rmsnorm(h3, fnorm_ref[...])
    hout_ref[...] = h3.reshape(hout_ref.shape)


def _post(h, o_hg, o_rw, p_all, layer, cv_prev, w_out, norm_ffn, ffn_in, cw, cb, wo, norm_ple,
          ple_gate, ple_proj, final_norm, apply_final_norm):
    bsz, t_len, d = h.shape
    tm = min(POST_TILE, t_len)
    seqs = min(bsz, POST_TILE // tm)
    assert t_len % tm == 0 and tm % SUBLANES == 0 and bsz % seqs == 0
    d_ff = cb.shape[1]
    n_col_blocks = d_ff // FF_COLS
    assert d_ff % FF_COLS == 0
    kern = functools.partial(_post_kernel, seqs=seqs, tm=tm, n_col_blocks=n_col_blocks,
                             apply_final_norm=apply_final_norm)
    tok = lambda w: pl.BlockSpec((seqs, tm, w), lambda b, t: (b, t, 0))
    p_spec = pl.BlockSpec((None, seqs, tm, p_all.shape[-1]), lambda b, t: (layer, b, t, 0))
    cv_spec = pl.BlockSpec((seqs, CONV_W - 1, d_ff), lambda b, t: (b, 0, 0))
    consts = [w_out, norm_ffn, ffn_in, cw, cb, wo, norm_ple, ple_gate, ple_proj, final_norm]
    return pl.pallas_call(
        kern,
        grid=(bsz // seqs, t_len // tm),
        in_specs=[tok(d), tok(o_hg.shape[2]), tok(o_rw.shape[2]), p_spec, cv_spec]
                 + [_const_spec(a.shape) for a in consts],
        out_specs=[tok(d), cv_spec],
        out_shape=[jax.ShapeDtypeStruct(h.shape, F32), jax.ShapeDtypeStruct(cv_prev.shape, F32)],
        scratch_shapes=[pltpu.VMEM((n_col_blocks, seqs, SUBLANES, FF_COLS), F32),
                        pltpu.VMEM((seqs, SUBLANES + tm, FF_COLS), F32)],
        compiler_params=pltpu.CompilerParams(dimension_semantics=("parallel", "arbitrary"),
                                             vmem_limit_bytes=VMEM_LIMIT),
        name="post",
    )(h, o_hg, o_rw, p_all, cv_prev, *consts)


def _layer_params(i, hg_proj, lb_all, w):
    rw_width = w["rw_w0"].shape[1]
    row = lambda a: a.reshape(1, -1)
    lb = jnp.maximum(lb_all[i], 0.0)
    zeros_lora = jnp.zeros((RW_LORA_W, rw_width), F32)
    return dict(
        norm_mix=row(w["norm_mix"][i]),
        w_in=w["w_in"][i].astype(BF16),
        log_lb=row(jnp.log(lb)), one_m_lb=row(1.0 - lb), hg_norm=row(w["hg_norm"][i]),
        mu=row(w["rw_mu"][i]), w0=row(w["rw_w0"][i]),
        w2p=jnp.concatenate([w["rw_w2"][i], zeros_lora], axis=0).astype(BF16),
        a0=row(w["rw_a0"][i]),
        a2p=jnp.concatenate([zeros_lora, w["rw_a2"][i]], axis=0).astype(BF16),
        g2=w["rw_g2"][i].astype(BF16),
        k_k=row(w["rw_k_k"][i]), k_a=row(w["rw_k_a"][i]), r_k=row(w["rw_r_k"][i]),
        ln_w=row(w["rw_ln_w"][i]), ln_b=row(w["rw_ln_b"][i]),
        w_out=w["w_out"][i].astype(BF16),
        norm_ffn=row(w["norm_ffn"][i]),
        ffn_in=w["ffn_in"][i].astype(BF16),
        cw=w["ffn_conv_w"][i], cb=row(w["ffn_conv_b"][i]),
        wo=w["ffn_out"][i].astype(BF16),
        norm_ple=row(w["norm_ple"][i]),
        ple_gate=w["ple_gate"][i].astype(BF16), ple_proj=w["ple_proj"][i].astype(BF16),
        final_norm=row(w["final_norm"]),
    )


def _trunk(x, p, s_hg, s_rw, s_sh, s_cv, layers):
    bsz, t_len, d = x.shape
    depth = len(layers)
    h = x
    new_hg, new_rw, new_sh, new_cv = [], [], [], []
    for i, lp in enumerate(layers):
        z_hg, z_rw = _in_proj(h.reshape(bsz * t_len, d), lp["norm_mix"], lp["w_in"], 4 * lp["hg_norm"].shape[1])
        z_hg = z_hg.reshape(bsz, t_len, -1)
        z_rw = z_rw.reshape(bsz, t_len, -1)
        o_hg, n_hg, o_rw, n_rw = _mixers(
            z_hg, z_rw, lp["log_lb"], lp["one_m_lb"], lp["hg_norm"], s_hg[i], s_sh[i][:, None, :],
            s_rw[i], lp["mu"], lp["w0"], lp["w2p"], lp["a0"], lp["a2p"], lp["g2"], lp["k_k"],
            lp["k_a"], lp["r_k"], lp["ln_w"], lp["ln_b"])
        h, n_cv = _post(h, o_hg, o_rw, p, i, s_cv[i], lp["w_out"], lp["norm_ffn"],
                        lp["ffn_in"], lp["cw"], lp["cb"], lp["wo"], lp["norm_ple"],
                        lp["ple_gate"], lp["ple_proj"], lp["final_norm"], i == depth - 1)
        new_hg.append(n_hg)
        new_rw.append(n_rw)
        new_sh.append(z_rw[:, -1])
        new_cv.append(n_cv)
    return h, jnp.stack(new_hg), jnp.stack(new_rw), jnp.stack(new_sh), jnp.stack(new_cv)


def kernel(x_prompt, x_sample, state_hgrn, state_rwkv, state_shift, state_conv, p_prompt, p_sample,
           norm_mix, w_in, lb_raw, hg_norm, rw_mu, rw_w0, rw_w2, rw_a0, rw_a2, rw_g2, rw_k_k, rw_k_a,
           rw_r_k, rw_ln_w, rw_ln_b, w_out, norm_ffn, ffn_in, ffn_conv_w, ffn_conv_b, ffn_out,
           norm_ple, ple_gate, ple_proj, final_norm):
    w = dict(norm_mix=norm_mix, w_in=w_in, hg_norm=hg_norm, rw_mu=rw_mu, rw_w0=rw_w0, rw_w2=rw_w2,
             rw_a0=rw_a0, rw_a2=rw_a2, rw_g2=rw_g2, rw_k_k=rw_k_k, rw_k_a=rw_k_a,
             rw_r_k=rw_r_k.reshape(rw_r_k.shape[0], -1), rw_ln_w=rw_ln_w, rw_ln_b=rw_ln_b, w_out=w_out,
             norm_ffn=norm_ffn, ffn_in=ffn_in, ffn_conv_w=ffn_conv_w, ffn_conv_b=ffn_conv_b,
             ffn_out=ffn_out, norm_ple=norm_ple, ple_gate=ple_gate, ple_proj=ple_proj,
             final_norm=final_norm)
    depth = w_in.shape[0]
    hg_proj = 4 * hg_norm.shape[1]
    lb_all = jnp.cumsum(jax.nn.softmax(lb_raw.astype(F32), axis=0), axis=0)
    lb_all = lb_all - lb_all[:1]
    layers = [_layer_params(i, hg_proj, lb_all, w) for i in range(depth)]

    nb = x_prompt.shape[0]
    zeros_like_state = lambda s: jnp.zeros((depth, nb) + s.shape[2:], s.dtype)
    y_p, hg_p, rw_p, sh_p, cv_p = _trunk(x_prompt, p_prompt, zeros_like_state(state_hgrn),
                                         zeros_like_state(state_rwkv), zeros_like_state(state_shift),
                                         zeros_like_state(state_conv), layers)
    y_s, hg_s, rw_s, sh_s, cv_s = _trunk(x_sample, p_sample, state_hgrn, state_rwkv, state_shift,
                                         state_conv, layers)
    return (y_p, y_s, hg_p, rw_p, sh_p, cv_p, hg_s, rw_s, sh_s, cv_s)
```

```python
import functools

import jax
import jax.numpy as jnp
from jax import lax
from jax.experimental import pallas as pl
from jax.experimental.pallas import tpu as pltpu

F32 = jnp.float32
BF16 = jnp.bfloat16

NORM_EPS = 1e-6
RW_GN_EPS = 64e-5
HG_HEAD_DIM = 128
RW_HEAD_DIM = 64
RW_LORA_W = 64
CONV_W = 3
HG_BLOCK = 16
RW_CHUNK = 64
RW_SUB = 16
MIX_TILE = 256
MIX_SEQS = 4
FF_COLS = 256
POST_TILE = 512
LANES = 128
SUBLANES = 8
VMEM_LIMIT = 56 * 1024 * 1024

NN = ((1,), (0,))
NT = ((1,), (1,))
TN = ((0,), (0,))


def _dot(a, b, dims=NN):
    return lax.dot_general(a, b, (dims, ((), ())), preferred_element_type=F32)


def _dot1(a, b, dims=NN):
    return _dot(a.astype(BF16), b.astype(BF16), dims)


def _split2(x):
    hi = x.astype(BF16)
    lo = (x - hi.astype(F32)).astype(BF16)
    return hi, lo


def _dot_xl2(a_exact, x, dims=NN):
    hi, lo = _split2(x)
    return _dot(a_exact, hi, dims) + _dot(a_exact, lo, dims)


def _dot_xr2(x, b, dims=NN):
    hi, lo = _split2(x)
    b = b.astype(BF16)
    return _dot(hi, b, dims) + _dot(lo, b, dims)


def _head_sums(x, pair_ones):
    n, width = x.shape
    tiles = [x[:, p * LANES:(p + 1) * LANES] for p in range(width // LANES)]
    sums = _dot(jnp.concatenate(tiles, axis=0).astype(BF16), pair_ones)
    return jnp.concatenate([sums[p * n:(p + 1) * n] for p in range(width // LANES)], axis=1)


def _sigmoid(x):
    return 0.5 * jnp.tanh(0.5 * x) + 0.5


def _softplus(x):
    return jnp.maximum(x, 0.0) + jnp.log(1.0 + jnp.exp(-jnp.abs(x)))


def _rmsnorm(x, g):
    return x * lax.rsqrt(jnp.mean(x * x, axis=-1, keepdims=True) + NORM_EPS) * g


def _iota2(shape, axis):
    return lax.broadcasted_iota(jnp.int32, shape, axis)


def _const_spec(shape):
    zeros = (0,) * len(shape)
    return pl.BlockSpec(shape, lambda *_: zeros, pipeline_mode=pl.Buffered(1))


def _map(fn, *lists):
    return [fn(*xs) for xs in zip(*lists)]


def _inproj_kernel(x_ref, g_ref, w_ref, zhg_ref, zrw_ref):
    hg_proj = zhg_ref.shape[1]
    xn = _rmsnorm(x_ref[...], g_ref[...]).astype(BF16)
    zhg_ref[...] = _dot(xn, w_ref[:, 0:hg_proj])
    zrw_ref[...] = _dot(xn, w_ref[:, hg_proj:])


def _in_proj(x2, g, w_in, hg_proj):
    n, d = x2.shape
    tm = min(512, n)
    assert n % tm == 0 and hg_proj % LANES == 0
    rw_proj = w_in.shape[1] - hg_proj
    return pl.pallas_call(
        _inproj_kernel,
        grid=(n // tm,),
        in_specs=[pl.BlockSpec((tm, d), lambda i: (i, 0)),
                  _const_spec((1, d)),
                  _const_spec(w_in.shape)],
        out_specs=[pl.BlockSpec((tm, hg_proj), lambda i: (i, 0)),
                   pl.BlockSpec((tm, rw_proj), lambda i: (i, 0))],
        out_shape=[jax.ShapeDtypeStruct((n, hg_proj), F32),
                   jax.ShapeDtypeStruct((n, rw_proj), F32)],
        compiler_params=pltpu.CompilerParams(dimension_semantics=("parallel",),
                                             vmem_limit_bytes=VMEM_LIMIT),
        name="in_proj",
    )(x2, g, w_in)


HG_Q, HG_K, HG_BL, HG_V, HG_GATE = range(5)


def _hgrn_front(z_ref, loglb_ref, omlb_ref, gn_ref, pre_ref, p_ref, *, n_heads, tile):
    dk = HG_HEAD_DIM
    width = n_heads * dk
    blk = HG_BLOCK
    zq = z_ref[0, :, 0:width]
    zf = z_ref[0, :, width:2 * width]
    zg = z_ref[0, :, 3 * width:4 * width]
    pre_ref[HG_Q] = zq * _sigmoid(zq)
    pre_ref[HG_V] = z_ref[0, :, 2 * width:3 * width]
    pre_ref[HG_GATE] = gn_ref[...] * (zg * _sigmoid(zg))
    soft = jnp.log(1.0 + jnp.exp(-jnp.abs(zf)))
    log_sig_pos = jnp.minimum(zf, 0.0) - soft
    log_sig_neg = jnp.minimum(-zf, 0.0) - soft
    lo_term = loglb_ref[...] + log_sig_neg
    top = jnp.maximum(log_sig_pos, lo_term)
    logf = top + jnp.log(1.0 + jnp.exp(-jnp.abs(log_sig_pos - lo_term)))
    pre_ref[HG_K] = omlb_ref[...] * jnp.exp(log_sig_neg)
    row = _iota2((tile, tile), 0)
    col = _iota2((tile, tile), 1)
    tri = ((row // blk == col // blk) & (col <= row)).astype(BF16)
    pre_ref[HG_BL] = _dot_xl2(tri, logf)
    yield

    t_row = _iota2((blk, dk), 0)
    for i in range(tile // blk):
        rows = slice(i * blk, (i + 1) * blk)
        for h in range(n_heads):
            lanes = slice(h * dk, (h + 1) * dk)
            qb = pre_ref[HG_Q, rows, lanes]
            kb = pre_ref[HG_K, rows, lanes]
            bb = pre_ref[HG_BL, rows, lanes]
            p0 = (i * n_heads + h) * blk
            for s in range(blk):
                causal = t_row >= s
                ps = jnp.where(causal, qb * kb[s:s + 1, :] * jnp.exp(bb - bb[s:s + 1, :]), 0.0)
                p_ref[p0:p0 + blk, s * dk:(s + 1) * dk] = ps.astype(BF16)
            yield


def _hgrn_back(sel_ref, pre_ref, p_ref, a_ref, st_ref, o_ref, *, n_heads, tile):
    dk = HG_HEAD_DIM
    blk = HG_BLOCK
    n_blocks = tile // blk
    a_ref[...] = _dot(p_ref[...], sel_ref[...])
    yield

    block_rows = [slice(i * blk, (i + 1) * blk) for i in range(n_blocks)]
    e_end = [pre_ref[HG_BL, (i + 1) * blk - 1:(i + 1) * blk, :] for i in range(n_blocks)]
    start = [jnp.zeros_like(e_end[0])]
    for i in range(n_blocks):
        start.append(start[i] + e_end[i])
    qe = [pre_ref[HG_Q, r, :] * jnp.exp(pre_ref[HG_BL, r, :]) for r in block_rows]
    ke = [pre_ref[HG_K, r, :] * jnp.exp(e - pre_ref[HG_BL, r, :])
          for r, e in zip(block_rows, e_end)]
    vv = [pre_ref[HG_V, r, :] for r in block_rows]
    q_tile = jnp.concatenate([x * jnp.exp(s) for x, s in zip(qe, start)], axis=0)
    k_tile = jnp.concatenate([x * jnp.exp(start[n_blocks] - s) for x, s in zip(ke, start[1:])], axis=0)
    v_tile = pre_ref[HG_V]
    decay_tile = jnp.exp(start[n_blocks])
    yield

    heads = range(n_heads)
    lanes = [slice(h * dk, (h + 1) * dk) for h in heads]
    st = [st_ref[h] for h in heads]
    new_st = [st[h] * decay_tile[:, lanes[h]] + _dot1(v_tile[:, lanes[h]], k_tile[:, lanes[h]], TN)
              for h in heads]
    yield
    o_state = [_dot1(q_tile[:, lanes[h]], st[h], NT) for h in heads]
    o_blocks = [[o_state[h][r] for r in block_rows] for h in heads]
    yield
    span = n_blocks // 2
    while span >= 1:
        for lo in range(0, n_blocks, 2 * span):
            mid, hi = lo + span, lo + 2 * span
            q_r = jnp.concatenate([qe[i] if i == mid else qe[i] * jnp.exp(start[i] - start[mid])
                                   for i in range(mid, hi)], axis=0)
            k_l = jnp.concatenate([ke[j] if j == mid - 1 else ke[j] * jnp.exp(start[mid] - start[j + 1])
                                   for j in range(lo, mid)], axis=0)
            v_l = jnp.concatenate(vv[lo:mid], axis=0)
            scores = [_dot1(q_r[:, lanes[h]], k_l[:, lanes[h]], NT) for h in heads]
            upd = [_dot1(scores[h], v_l[:, lanes[h]]) for h in heads]
            for h in heads:
                for n, i in enumerate(range(mid, hi)):
                    o_blocks[h][i] = o_blocks[h][i] + upd[h][n * blk:(n + 1) * blk]
            yield
        span //= 2
    for i in range(n_blocks):
        for h in heads:
            p0 = (i * n_heads + h) * blk
            o_blocks[h][i] = o_blocks[h][i] + _dot1(a_ref[p0:p0 + blk, 0:blk], vv[i][:, lanes[h]])
        yield
    for h in heads:
        o = jnp.concatenate(o_blocks[h], axis=0)
        o = o * lax.rsqrt(jnp.mean(o * o, axis=-1, keepdims=True) + NORM_EPS)
        o_ref[0, :, lanes[h]] = (o * pre_ref[HG_GATE, :, lanes[h]]).astype(o_ref.dtype)
    for h in heads:
        st_ref[h] = new_st[h]


RW_KT, RW_RT, RW_V, RW_BH, RW_KH, RW_BL, RW_KL, RW_C, RW_RK, RW_G = range(10)


def _block_diag(y):
    lo = _iota2(y.shape, 1) < RW_HEAD_DIM
    y = y.astype(BF16)
    zero = jnp.zeros_like(y)
    return jnp.concatenate([jnp.where(lo, y, zero), jnp.where(lo, zero, y)], axis=0)


def _pair_dot3(x, y, z=None):
    n = x.shape[0]
    xh, xl = _split2(x)
    parts = [_split2(y)] if z is None else [_split2(y), _split2(z)]
    w_hi = jnp.concatenate([_block_diag(hi) for hi, _ in parts], axis=1)
    w_lo = jnp.concatenate([_block_diag(lo) for _, lo in parts], axis=1)
    both = _dot(jnp.concatenate([xh, xl], axis=0), w_hi)
    return both[0:n] + both[n:] + _dot(xh, w_lo)


def _pair_dot_nt(x, y):
    return _dot(x.astype(BF16), _block_diag(y), NT)


def _pair_dot2(x, y, z):
    return _dot(x.astype(BF16), jnp.concatenate([_block_diag(y), _block_diag(z)], axis=1))


def _pair_dot2_nt(x, y, z):
    return _dot(x.astype(BF16), jnp.concatenate([_block_diag(y), _block_diag(z)], axis=0), NT)


def _unit_lower_inverses(n_list, eye, same_sub):
    d = [jnp.where(same_sub, n, 0.0) for n in n_list]
    l_off = _map(lambda n, dd: n - dd, n_list, d)
    t_d = [eye - dd for dd in d]
    n = n_list[0].shape[0]
    d_pow = _map(_pair_dot3, d, d)
    yield
    for _ in range(RW_SUB.bit_length() - 3):
        both = _map(lambda dp, t: _pair_dot3(jnp.concatenate([dp, t], axis=0), dp), d_pow, t_d)
        t_d = _map(lambda t, b: t + b[n:], t_d, both)
        d_pow = [b[0:n] for b in both]
        yield
    t_d = _map(lambda t, dp: t + _pair_dot3(t, dp), t_d, d_pow)
    yield
    p = _map(_pair_dot3, t_d, l_off)
    yield
    p2 = _map(_pair_dot3, p, p)
    yield
    t_p = _map(lambda pp, pp2: (eye - pp) + _pair_dot3(eye - pp, pp2), p, p2)
    yield
    return _map(_pair_dot3, t_p, t_d)


def _rwkv_front(z_ref, mu_ref, w0_ref, w2_ref, a0_ref, a2_ref, g2_ref, kk_ref, ka_ref, rk_ref,
                hones_ref, pre_ref, carry_ref, zs_ref, *, n_heads, chunk, tile):
    hd = RW_HEAD_DIM
    width = n_heads * hd
    z = z_ref[0]
    zs_ref[0:SUBLANES, :] = carry_ref[...]
    zs_ref[SUBLANES:SUBLANES + tile, :] = z
    carry_ref[...] = z[tile - SUBLANES:tile, :]
    z_prev = zs_ref[SUBLANES - 1:SUBLANES - 1 + tile, :]
    zm = z + (z_prev - z) * mu_ref[...]

    r = zm[:, 0:width]
    k = zm[:, width:2 * width]
    v = zm[:, 2 * width:3 * width]
    lora_in = zm[:, 3 * width:3 * width + 2 * RW_LORA_W]
    gd = zm[:, 3 * width + 2 * RW_LORA_W:]
    pre_ref[RW_V] = v
    pre_ref[RW_G] = _dot(_sigmoid(gd).astype(BF16), g2_ref[...])
    yield

    w_pre = w0_ref[...] + _dot_xr2(jnp.tanh(lora_in), w2_ref[...])
    w_raw = -_softplus(-w_pre) - 0.5
    logw = -jnp.exp(w_raw)
    a = _sigmoid(a0_ref[...] + _dot1(lora_in, a2_ref[...]))
    kk = k * kk_ref[...]
    kk = kk * jnp.minimum(lax.rsqrt(_head_sums(kk * kk, hones_ref[...])), 1e12)
    k2 = k * (1.0 + (a - 1.0) * ka_ref[...])
    beta = kk * a
    pre_ref[RW_RK] = r * k2 * rk_ref[...]
    yield

    trow = _iota2((tile, tile), 0)
    tcol = _iota2((tile, tile), 1)
    tri = ((trow // chunk == tcol // chunk) & (tcol <= trow)).astype(BF16)
    c = _dot_xl2(tri, logw)
    pre_ref[RW_C] = c
    e_neg = jnp.exp(-c)
    pre_ref[RW_KT] = kk * jnp.exp(c - logw)
    pre_ref[RW_RT] = r * jnp.exp(c)
    pre_ref[RW_BH] = beta * e_neg
    pre_ref[RW_KH] = k2 * e_neg
    yield
    for ci in range(tile // chunk):
        rows = slice(ci * chunk, (ci + 1) * chunk)
        e_end = jnp.exp(c[rows][chunk - 1:chunk, :] - c[rows])
        pre_ref[RW_BL, rows, :] = beta[rows] * e_end
        pre_ref[RW_KL, rows, :] = k2[rows] * e_end
    yield


def _rwkv_back(pre_ref, hones_ref, lnw_ref, lnb_ref, s_ref, y_ref, o_ref, *, n_heads, chunk, tile):
    hd = RW_HEAD_DIM
    n_chunks = tile // chunk
    pw = 2 * hd
    pairs = range(n_heads // 2)
    lanes = [slice(p * pw, (p + 1) * pw) for p in pairs]
    row = _iota2((chunk, pw), 0)
    lane = _iota2((chunk, pw), 1)
    lo_half = lane < hd
    col = jnp.where(lo_half, lane, lane - hd)
    incl2 = jnp.concatenate([col < row, col <= row], axis=0)
    incl4 = jnp.concatenate([incl2, incl2], axis=1)
    eye = (row == col).astype(F32)
    same_sub = (row // RW_SUB) == (col // RW_SUB)
    same_head = (_iota2((pw, pw), 0) // hd) == (_iota2((pw, pw), 1) // hd)

    chunk_rows = [slice(ci * chunk, (ci + 1) * chunk) for ci in range(n_chunks)]
    units = [(ci, p) for ci in range(n_chunks) for p in pairs]
    at = lambda name: [pre_ref[name, chunk_rows[ci], lanes[p]] for ci, p in units]
    kt, rt, vv, b_l, k_l = at(RW_KT), at(RW_RT), at(RW_V), at(RW_BL), at(RW_KL)
    g_last = [jnp.exp(pre_ref[RW_C, (ci + 1) * chunk - 1:(ci + 1) * chunk, lanes[p]]) for ci, p in units]
    lhs = _map(lambda x, y: jnp.concatenate([x, y], axis=0), kt, rt)
    g_bk = [jnp.where(incl4, _pair_dot2_nt(l, xb, xk), 0.0)
            for l, xb, xk in zip(lhs, at(RW_BH), at(RW_KH))]
    a_ab = [x[0:chunk, 0:pw] for x in g_bk]
    a_rb = [x[chunk:, 0:pw] for x in g_bk]
    yield
    gkv = [_pair_dot3(x[:, pw:], v_) for x, v_ in zip(g_bk, vv)]
    x_loc = [x[0:chunk] for x in gkv]
    yield
    t_inv = yield from _unit_lower_inverses(a_ab, eye, same_sub)
    yield
    wu = _map(_pair_dot3, t_inv, kt, x_loc)
    w_t = [x[:, 0:pw] for x in wu]
    u_loc = [x[:, pw:] for x in wu]
    yield
    a_wu = _map(_pair_dot2, a_rb, w_t, u_loc)
    q_c = _map(lambda r_, x: r_ - x[:, 0:pw], rt, a_wu)
    y_loc = _map(lambda x, y: x[chunk:] - y[:, pw:], gkv, a_wu)
    yield
    m_off = _map(lambda w_, bl_: jnp.where(same_head, -_dot1(w_, bl_, TN), 0.0), w_t, b_l)
    yield
    c_full = _map(lambda v_, u_, kl_, bl_: _dot1(jnp.concatenate([v_, -u_], axis=0),
                                                 jnp.concatenate([kl_, bl_], axis=0), TN),
                  vv, u_loc, k_l, b_l)
    c_add = [jnp.where(lo_half, x[0:hd], x[hd:]) for x in c_full]
    yield
    state = [s_ref[p] for p in pairs]
    for idx, (ci, p) in enumerate(units):
        y_ref[chunk_rows[ci], lanes[p]] = _pair_dot_nt(q_c[idx], state[p]) + y_loc[idx]
        state[p] = state[p] * g_last[idx] + _dot1(state[p], m_off[idx]) + c_add[idx]
    for p in pairs:
        s_ref[p] = state[p]
    yield

    y = y_ref[...]
    head_ones = hones_ref[...]
    mean = _head_sums(y, head_ones) * (1.0 / hd)
    yc = y - mean
    var = _head_sums(yc * yc, head_ones) * (1.0 / hd)
    yn = yc * lax.rsqrt(var + RW_GN_EPS) * lnw_ref[...] + lnb_ref[...]
    bonus = _head_sums(pre_ref[RW_RK], head_ones) * pre_ref[RW_V]
    o_ref[0] = ((yn + bonus) * pre_ref[RW_G]).astype(o_ref.dtype)


_DONE = object()
MIX_STEPS = (3, 2, 1, 4)


def _mixer_kernel(zhg_ref, zrw_ref, loglb_ref, omlb_ref, gn_ref, sel_ref, s0hg_ref,
                  shift_ref, s0rw_ref, mu_ref, w0_ref, w2_ref, a0_ref, a2_ref, g2_ref,
                  kk_ref, ka_ref, rk_ref, lnw_ref, lnb_ref, hones_ref,
                  ohg_ref, sfin_hg_ref, orw_ref, sfin_rw_ref,
                  st_ref, s_ref, carry_ref, zs_ref, y_ref, a_ref,
                  hg_pre_a, hg_pre_b, p_a, p_b, rw_pre_a, rw_pre_b,
                  *, hg_heads, rw_heads, tile, n_tiles, pipelined, seqs):
    k = pl.program_id(0)
    k_back = k - 1 if pipelined else k
    hg = dict(n_heads=hg_heads, tile=tile)
    rw = dict(n_heads=rw_heads, chunk=RW_CHUNK, tile=tile)
    every = range(seqs)
    one = lambda ref, b: ref.at[b:b + 1]
    own = lambda ref, b: ref.at[b]

    @pl.when(k % n_tiles == 0)
    def _():
        for b in every:
            carry_ref[b] = jnp.broadcast_to(shift_ref[b], carry_ref.shape[1:])

    @pl.when((k_back % n_tiles == 0) | (k == 0))
    def _():
        for b in every:
            for h in range(hg_heads):
                st_ref[b, h] = s0hg_ref[b, h].T
            s_ref[b] = s0rw_ref[b]

    if pipelined:
        @pl.when(k == 0)
        def _():
            hg_pre_b[...] = jnp.zeros(hg_pre_b.shape, hg_pre_b.dtype)
            p_b[...] = jnp.zeros(p_b.shape, p_b.dtype)
            rw_pre_b[...] = jnp.zeros(rw_pre_b.shape, rw_pre_b.dtype)

    def fronts(b, hg_w, p_w, rw_w):
        return [_rwkv_front(one(zrw_ref, b), mu_ref, w0_ref, w2_ref, a0_ref, a2_ref, g2_ref, kk_ref,
                            ka_ref, rk_ref, hones_ref, own(rw_w, b), own(carry_ref, b),
                            own(zs_ref, b), **rw),
                _hgrn_front(one(zhg_ref, b), loglb_ref, omlb_ref, gn_ref, own(hg_w, b), own(p_w, b), **hg)]

    def backs(b, hg_r, p_r, rw_r):
        return [_rwkv_back(own(rw_r, b), hones_ref, lnw_ref, lnb_ref, own(s_ref, b), own(y_ref, b),
                           one(orw_ref, b), **rw),
                _hgrn_back(sel_ref, own(hg_r, b), own(p_r, b), own(a_ref, b), own(st_ref, b),
                           one(ohg_ref, b), **hg)]

    def run(bodies, steps):
        live = [True] * len(bodies)
        while any(live):
            for n, body in enumerate(bodies):
                for _ in range(steps[n % len(steps)]):
                    if live[n]:
                        live[n] = next(body, _DONE) is not _DONE

    if not pipelined:
        run([f for b in every for f in fronts(b, hg_pre_a, p_a, rw_pre_a)], MIX_STEPS[2:])
        run([f for b in every for f in backs(b, hg_pre_a, p_a, rw_pre_a)], MIX_STEPS[:2])
    else:
        @pl.when(k % 2 == 0)
        def _():
            run(backs(0, hg_pre_b, p_b, rw_pre_b) + fronts(0, hg_pre_a, p_a, rw_pre_a), MIX_STEPS)

        @pl.when(k % 2 == 1)
        def _():
            run(backs(0, hg_pre_a, p_a, rw_pre_a) + fronts(0, hg_pre_b, p_b, rw_pre_b), MIX_STEPS)

    @pl.when((k_back >= 0) & (k_back % n_tiles == n_tiles - 1))
    def _():
        for b in every:
            for h in range(hg_heads):
                sfin_hg_ref[b, h] = st_ref[b, h].T
            sfin_rw_ref[b] = s_ref[b]


def _mixers(z_hg, z_rw, log_lb, one_m_lb, hg_norm, s0_hg, shift_prev, s0_rw,
            mu, w0, w2p, a0, a2p, g2, k_k, k_a, r_k, ln_w, ln_b):
    bsz, t_len, hg_proj = z_hg.shape
    rw_proj = z_rw.shape[2]
    hg_width = hg_proj // 4
    hg_heads = hg_width // HG_HEAD_DIM
    rw_heads = s0_rw.shape[1]
    hd = RW_HEAD_DIM
    rw_width = rw_heads * hd
    tile = min(MIX_TILE, t_len)
    assert t_len % tile == 0 and tile % RW_CHUNK == 0 and tile % HG_BLOCK == 0 and rw_heads % 2 == 0
    n_tiles = t_len // tile
    n_pairs = rw_heads // 2
    s0_rw = s0_rw.reshape(bsz, n_pairs, 2, hd, hd).transpose(0, 1, 3, 2, 4).reshape(bsz, n_pairs, hd, 2 * hd)
    group = jnp.arange(HG_BLOCK * HG_HEAD_DIM, dtype=jnp.int32)[:, None] // HG_HEAD_DIM
    selector = (group == jnp.arange(LANES, dtype=jnp.int32)[None, :]).astype(BF16)
    head_of = jnp.arange(LANES, dtype=jnp.int32) // hd
    head_ones = (head_of[:, None] == head_of[None, :]).astype(BF16)

    assert RW_CHUNK == RW_HEAD_DIM
    pipelined = n_tiles > 1
    seqs = 1 if pipelined else min(bsz, MIX_SEQS)
    assert bsz % seqs == 0
    total = bsz * n_tiles // seqs
    front_k = lambda k: jnp.minimum(k, total - 1)
    back_k = (lambda k: jnp.maximum(k - 1, 0)) if pipelined else front_k
    tok_in = lambda w: pl.BlockSpec((seqs, tile, w), lambda k: (front_k(k) // n_tiles, front_k(k) % n_tiles, 0))
    tok_out = lambda w: pl.BlockSpec((seqs, tile, w), lambda k: (back_k(k) // n_tiles, back_k(k) % n_tiles, 0))
    per_seq = lambda a, which: pl.BlockSpec((seqs,) + a.shape[1:],
                                            lambda k: (which(k) // n_tiles,) + (0,) * (a.ndim - 1))
    per_seq_scratch = lambda shape, dtype: pltpu.VMEM((seqs,) + shape, dtype)
    consts_hg = [log_lb, one_m_lb, hg_norm, selector]
    consts_rw = [mu, w0, w2p, a0, a2p, g2, k_k, k_a, r_k, ln_w, ln_b, head_ones]
    kern = functools.partial(_mixer_kernel, hg_heads=hg_heads, rw_heads=rw_heads, tile=tile,
                             n_tiles=n_tiles, pipelined=pipelined, seqs=seqs)
    o_hg, n_hg, o_rw, n_rw = pl.pallas_call(
        kern,
        grid=(total + 1 if pipelined else total,),
        in_specs=[tok_in(hg_proj), tok_in(rw_proj)]
                 + [_const_spec(a.shape) for a in consts_hg] + [per_seq(s0_hg, back_k)]
                 + [per_seq(shift_prev, front_k), per_seq(s0_rw, back_k)]
                 + [_const_spec(a.shape) for a in consts_rw],
        out_specs=[tok_out(hg_width), per_seq(s0_hg, back_k), tok_out(rw_width), per_seq(s0_rw, back_k)],
        out_shape=[jax.ShapeDtypeStruct((bsz, t_len, hg_width), BF16),
                   jax.ShapeDtypeStruct(s0_hg.shape, F32),
                   jax.ShapeDtypeStruct((bsz, t_len, rw_width), BF16),
                   jax.ShapeDtypeStruct(s0_rw.shape, F32)],
        scratch_shapes=[per_seq_scratch((hg_heads, HG_HEAD_DIM, HG_HEAD_DIM), F32),
                        per_seq_scratch((n_pairs, hd, 2 * hd), F32),
                        per_seq_scratch((SUBLANES, rw_proj), F32),
                        per_seq_scratch((SUBLANES + tile, rw_proj), F32),
                        per_seq_scratch((tile, rw_width), F32),
                        per_seq_scratch((tile * hg_heads, LANES), F32)]
                       + [per_seq_scratch((5, tile, hg_width), F32)] * 2
                       + [per_seq_scratch((tile * hg_heads, HG_BLOCK * HG_HEAD_DIM), BF16)] * 2
                       + [per_seq_scratch((10, tile, rw_width), F32)] * 2,
        compiler_params=pltpu.CompilerParams(dimension_semantics=("arbitrary",),
                                             vmem_limit_bytes=VMEM_LIMIT),
        name="mixers",
    )(z_hg, z_rw, *consts_hg, s0_hg, shift_prev, s0_rw, *consts_rw)
    n_rw = n_rw.reshape(bsz, n_pairs, hd, 2, hd).transpose(0, 1, 3, 2, 4).reshape(bsz, rw_heads, hd, hd)
    return o_hg, n_hg, o_rw, n_rw


def _gelu_exact(x):
    return 0.5 * x * (1.0 + lax.erf(x * (2.0 ** -0.5)))


def _post_kernel(h_ref, ohg_ref, orw_ref, p_ref, cvprev_ref,
                 wout_ref, nffn_ref, win_ref, cw_ref, cb_ref, wo_ref,
                 nple_ref, pgate_ref, pproj_ref, fnorm_ref,
                 hout_ref, cvnew_ref, carry_ref, ugs_ref, *, seqs, tm, n_col_blocks, apply_final_norm):
    t_idx = pl.program_id(1)
    cols = FF_COLS
    d_ff = n_col_blocks * cols
    hg_width = ohg_ref.shape[-1]
    rows = seqs * tm
    flat = lambda ref: ref[...].reshape(rows, ref.shape[-1])
    col_block = lambda j: slice(j * cols, (j + 1) * cols)
    up = lambda j: (_dot(xn, win_ref[:, col_block(j)]),
                    _dot(xn, win_ref[:, d_ff + j * cols:d_ff + (j + 1) * cols]))

    @pl.when(t_idx == 0)
    def _():
        carry_ref[...] = jnp.zeros(carry_ref.shape, carry_ref.dtype)
        for j in range(n_col_blocks):
            for b in range(seqs):
                carry_ref[j, b, SUBLANES - (CONV_W - 1):SUBLANES, :] = cvprev_ref[b, :, col_block(j)]

    h1 = (flat(h_ref) + _dot(flat(ohg_ref).astype(BF16), wout_ref[0:hg_width, :])
          + _dot(flat(orw_ref).astype(BF16), wout_ref[hg_width:, :]))
    xn = _rmsnorm(h1, nffn_ref[...]).astype(BF16)
    acc = jnp.zeros_like(h1)
    up_next = up(0)
    for j in range(n_col_blocks):
        ug, uv = up_next
        if j + 1 < n_col_blocks:
            up_next = up(j + 1)
        cw = cw_ref[:, col_block(j)]
        conv = []
        for b in range(seqs):
            ug_b = ug[b * tm:(b + 1) * tm]
            ugs_ref[b, 0:SUBLANES, :] = carry_ref[j, b]
            ugs_ref[b, SUBLANES:SUBLANES + tm, :] = ug_b
            carry_ref[j, b] = ug_b[tm - SUBLANES:tm, :]
            cvnew_ref[b, :, col_block(j)] = ug_b[tm - (CONV_W - 1):tm, :]
            conv.append(cw[0:1, :] * ugs_ref[b, SUBLANES - 2:SUBLANES - 2 + tm, :]
                        + cw[1:2, :] * ugs_ref[b, SUBLANES - 1:SUBLANES - 1 + tm, :])
        c = cb_ref[:, col_block(j)] + jnp.concatenate(conv, axis=0) + cw[2:3, :] * ug
        hid = (_gelu_exact(c) * uv).astype(BF16)
        acc = acc + _dot(hid, wo_ref[col_block(j), :])
    h2 = h1 + acc
    gate = _sigmoid(_dot(_rmsnorm(h2, nple_ref[...]).astype(BF16), pgate_ref[...]))
    h3 = h2 + gate * _dot(flat(p_ref).astype(BF16), pproj_ref[...])
    if apply_final_norm:
        h3 = _rmsnorm(h3, fnorm_ref[...])
    hout_ref[...] = h3.reshape(hout_ref.shape)


def _post(h, o_hg, o_rw, p_all, layer, cv_prev, w_out, norm_ffn, ffn_in, cw, cb, wo, norm_ple,
          ple_gate, ple_proj, final_norm, apply_final_norm):
    bsz, t_len, d = h.shape
    tm = min(POST_TILE, t_len)
    seqs = min(bsz, POST_TILE // tm)
    assert t_len % tm == 0 and tm % SUBLANES == 0 and bsz % seqs == 0
    d_ff = cb.shape[1]
    n_col_blocks = d_ff // FF_COLS
    assert d_ff % FF_COLS == 0
    kern = functools.partial(_post_kernel, seqs=seqs, tm=tm, n_col_blocks=n_col_blocks,
                             apply_final_norm=apply_final_norm)
    tok = lambda w: pl.BlockSpec((seqs, tm, w), lambda b, t: (b, t, 0))
    p_spec = pl.BlockSpec((None, seqs, tm, p_all.shape[-1]), lambda b, t: (layer, b, t, 0))
    cv_spec = pl.BlockSpec((seqs, CONV_W - 1, d_ff), lambda b, t: (b, 0, 0))
    consts = [w_out, norm_ffn, ffn_in, cw, cb, wo, norm_ple, ple_gate, ple_proj, final_norm]
    return pl.pallas_call(
        kern,
        grid=(bsz // seqs, t_len // tm),
        in_specs=[tok(d), tok(o_hg.shape[2]), tok(o_rw.shape[2]), p_spec, cv_spec]
                 + [_const_spec(a.shape) for a in consts],
        out_specs=[tok(d), cv_spec],
        out_shape=[jax.ShapeDtypeStruct(h.shape, F32), jax.ShapeDtypeStruct(cv_prev.shape, F32)],
        scratch_shapes=[pltpu.VMEM((n_col_blocks, seqs, SUBLANES, FF_COLS), F32),
                        pltpu.VMEM((seqs, SUBLANES + tm, FF_COLS), F32)],
        compiler_params=pltpu.CompilerParams(dimension_semantics=("parallel", "arbitrary"),
                                             vmem_limit_bytes=VMEM_LIMIT),
        name="post",
    )(h, o_hg, o_rw, p_all, cv_prev, *consts)


def _layer_params(i, hg_proj, lb_all, w):
    rw_width = w["rw_w0"].shape[1]
    row = lambda a: a.reshape(1, -1)
    lb = jnp.maximum(lb_all[i], 0.0)
    zeros_lora = jnp.zeros((RW_LORA_W, rw_width), F32)
    return dict(
        norm_mix=row(w["norm_mix"][i]),
        w_in=w["w_in"][i].astype(BF16),
        log_lb=row(jnp.log(lb)), one_m_lb=row(1.0 - lb), hg_norm=row(w["hg_norm"][i]),
        mu=row(w["rw_mu"][i]), w0=row(w["rw_w0"][i]),
        w2p=jnp.concatenate([w["rw_w2"][i], zeros_lora], axis=0).astype(BF16),
        a0=row(w["rw_a0"][i]),
        a2p=jnp.concatenate([zeros_lora, w["rw_a2"][i]], axis=0).astype(BF16),
        g2=w["rw_g2"][i].astype(BF16),
        k_k=row(w["rw_k_k"][i]), k_a=row(w["rw_k_a"][i]), r_k=row(w["rw_r_k"][i]),
        ln_w=row(w["rw_ln_w"][i]), ln_b=row(w["rw_ln_b"][i]),
        w_out=w["w_out"][i].astype(BF16),
        norm_ffn=row(w["norm_ffn"][i]),
        ffn_in=w["ffn_in"][i].astype(BF16),
        cw=w["ffn_conv_w"][i], cb=row(w["ffn_conv_b"][i]),
        wo=w["ffn_out"][i].astype(BF16),
        norm_ple=row(w["norm_ple"][i]),
        ple_gate=w["ple_gate"][i].astype(BF16), ple_proj=w["ple_proj"][i].astype(BF16),
        final_norm=row(w["final_norm"]),
    )


def _trunk(x, p, s_hg, s_rw, s_sh, s_cv, layers):
    bsz, t_len, d = x.shape
    depth = len(layers)
    h = x
    new_hg, new_rw, new_sh, new_cv = [], [], [], []
    for i, lp in enumerate(layers):
        z_hg, z_rw = _in_proj(h.reshape(bsz * t_len, d), lp["norm_mix"], lp["w_in"], 4 * lp["hg_norm"].shape[1])
        z_hg = z_hg.reshape(bsz, t_len, -1)
        z_rw = z_rw.reshape(bsz, t_len, -1)
        o_hg, n_hg, o_rw, n_rw = _mixers(
            z_hg, z_rw, lp["log_lb"], lp["one_m_lb"], lp["hg_norm"], s_hg[i], s_sh[i][:, None, :],
            s_rw[i], lp["mu"], lp["w0"], lp["w2p"], lp["a0"], lp["a2p"], lp["g2"], lp["k_k"],
            lp["k_a"], lp["r_k"], lp["ln_w"], lp["ln_b"])
        h, n_cv = _post(h, o_hg, o_rw, p, i, s_cv[i], lp["w_out"], lp["norm_ffn"],
                        lp["ffn_in"], lp["cw"], lp["cb"], lp["wo"], lp["norm_ple"],
                        lp["ple_gate"], lp["ple_proj"], lp["final_norm"], i == depth - 1)
        new_hg.append(n_hg)
        new_rw.append(n_rw)
        new_sh.append(z_rw[:, -1])
        new_cv.append(n_cv)
    return h, jnp.stack(new_hg), jnp.stack(new_rw), jnp.stack(new_sh), jnp.stack(new_cv)


def kernel(x_prompt, x_sample, state_hgrn, state_rwkv, state_shift, state_conv, p_prompt, p_sample,
           norm_mix, w_in, lb_raw, hg_norm, rw_mu, rw_w0, rw_w2, rw_a0, rw_a2, rw_g2, rw_k_k, rw_k_a,
           rw_r_k, rw_ln_w, rw_ln_b, w_out, norm_ffn, ffn_in, ffn_conv_w, ffn_conv_b, ffn_out,
           norm_ple, ple_gate, ple_proj, final_norm):
    w = dict(norm_mix=norm_mix, w_in=w_in, hg_norm=hg_norm, rw_mu=rw_mu, rw_w0=rw_w0, rw_w2=rw_w2,
             rw_a0=rw_a0, rw_a2=rw_a2, rw_g2=rw_g2, rw_k_k=rw_k_k, rw_k_a=rw_k_a,
             rw_r_k=rw_r_k.reshape(rw_r_k.shape[0], -1), rw_ln_w=rw_ln_w, rw_ln_b=rw_ln_b, w_out=w_out,
             norm_ffn=norm_ffn, ffn_in=ffn_in, ffn_conv_w=ffn_conv_w, ffn_conv_b=ffn_conv_b,
             ffn_out=ffn_out, norm_ple=norm_ple, ple_gate=ple_gate, ple_proj=ple_proj,
             final_norm=final_norm)
    depth = w_in.shape[0]
    hg_proj = 4 * hg_norm.shape[1]
    lb_all = jnp.cumsum(jax.nn.softmax(lb_raw.astype(F32), axis=0), axis=0)
    lb_all = lb_all - lb_all[:1]
    layers = [_layer_params(i, hg_proj, lb_all, w) for i in range(depth)]

    nb = x_prompt.shape[0]
    zeros_like_state = lambda s: jnp.zeros((depth, nb) + s.shape[2:], s.dtype)
    y_p, hg_p, rw_p, sh_p, cv_p = _trunk(x_prompt, p_prompt, zeros_like_state(state_hgrn),
                                         zeros_like_state(state_rwkv), zeros_like_state(state_shift),
                                         zeros_like_state(state_conv), layers)
    y_s, hg_s, rw_s, sh_s, cv_s = _trunk(x_sample, p_sample, state_hgrn, state_rwkv, state_shift,
                                         state_conv, layers)
    return (y_p, y_s, hg_p, rw_p, sh_p, cv_p, hg_s, rw_s, sh_s, cv_s)
```

```python
import functools

import jax
import jax.numpy as jnp
from jax import lax
from jax.experimental import pallas as pl
from jax.experimental.pallas import tpu as pltpu

F32 = jnp.float32
BF16 = jnp.bfloat16

NORM_EPS = 1e-6
RW_GN_EPS = 64e-5
HG_HEAD_DIM = 128
RW_HEAD_DIM = 64
RW_LORA_W = 64
CONV_W = 3
HG_BLOCK = 16
RW_CHUNK = 64
RW_SUB = 16
MIX_TILE = 256
MIX_SEQS = 4
FF_COLS = 256
POST_TILE = 512
LANES = 128
SUBLANES = 8
VMEM_LIMIT = 56 * 1024 * 1024

NN = ((1,), (0,))
NT = ((1,), (1,))
TN = ((0,), (0,))


def _dot(a, b, dims=NN):
    return lax.dot_general(a, b, (dims, ((), ())), preferred_element_type=F32)


def _dot1(a, b, dims=NN):
    return _dot(a.astype(BF16), b.astype(BF16), dims)


def _split2(x):
    hi = x.astype(BF16)
    lo = (x - hi.astype(F32)).astype(BF16)
    return hi, lo


def _dot_xl2(a_exact, x, dims=NN):
    hi, lo = _split2(x)
    return _dot(a_exact, hi, dims) + _dot(a_exact, lo, dims)


def _dot_xr2(x, b, dims=NN):
    hi, lo = _split2(x)
    b = b.astype(BF16)
    return _dot(hi, b, dims) + _dot(lo, b, dims)


def _head_sums(x, pair_ones):
    n, width = x.shape
    tiles = [x[:, p * LANES:(p + 1) * LANES] for p in range(width // LANES)]
    sums = _dot(jnp.concatenate(tiles, axis=0).astype(BF16), pair_ones)
    return jnp.concatenate([sums[p * n:(p + 1) * n] for p in range(width // LANES)], axis=1)


def _sigmoid(x):
    return 0.5 * jnp.tanh(0.5 * x) + 0.5


def _softplus(x):
    return jnp.maximum(x, 0.0) + jnp.log(1.0 + jnp.exp(-jnp.abs(x)))


def _rmsnorm(x, g):
    return x * lax.rsqrt(jnp.mean(x * x, axis=-1, keepdims=True) + NORM_EPS) * g


def _iota2(shape, axis):
    return lax.broadcasted_iota(jnp.int32, shape, axis)


def _const_spec(shape):
    zeros = (0,) * len(shape)
    return pl.BlockSpec(shape, lambda *_: zeros, pipeline_mode=pl.Buffered(1))


def _map(fn, *lists):
    return [fn(*xs) for xs in zip(*lists)]


def _inproj_kernel(x_ref, g_ref, w_ref, zhg_ref, zrw_ref):
    hg_proj = zhg_ref.shape[1]
    xn = _rmsnorm(x_ref[...], g_ref[...]).astype(BF16)
    zhg_ref[...] = _dot(xn, w_ref[:, 0:hg_proj])
    zrw_ref[...] = _dot(xn, w_ref[:, hg_proj:])


def _in_proj(x2, g, w_in, hg_proj):
    n, d = x2.shape
    tm = min(512, n)
    assert n % tm == 0 and hg_proj % LANES == 0
    rw_proj = w_in.shape[1] - hg_proj
    return pl.pallas_call(
        _inproj_kernel,
        grid=(n // tm,),
        in_specs=[pl.BlockSpec((tm, d), lambda i: (i, 0)),
                  _const_spec((1, d)),
                  _const_spec(w_in.shape)],
        out_specs=[pl.BlockSpec((tm, hg_proj), lambda i: (i, 0)),
                   pl.BlockSpec((tm, rw_proj), lambda i: (i, 0))],
        out_shape=[jax.ShapeDtypeStruct((n, hg_proj), F32),
                   jax.ShapeDtypeStruct((n, rw_proj), F32)],
        compiler_params=pltpu.CompilerParams(dimension_semantics=("parallel",),
                                             vmem_limit_bytes=VMEM_LIMIT),
        name="in_proj",
    )(x2, g, w_in)


HG_Q, HG_K, HG_BL, HG_V, HG_GATE = range(5)


def _hgrn_front(z_ref, loglb_ref, omlb_ref, gn_ref, pre_ref, p_ref, *, n_heads, tile):
    dk = HG_HEAD_DIM
    width = n_heads * dk
    blk = HG_BLOCK
    zq = z_ref[0, :, 0:width]
    zf = z_ref[0, :, width:2 * width]
    zg = z_ref[0, :, 3 * width:4 * width]
    pre_ref[HG_Q] = zq * _sigmoid(zq)
    pre_ref[HG_V] = z_ref[0, :, 2 * width:3 * width]
    pre_ref[HG_GATE] = gn_ref[...] * (zg * _sigmoid(zg))
    soft = jnp.log(1.0 + jnp.exp(-jnp.abs(zf)))
    log_sig_pos = jnp.minimum(zf, 0.0) - soft
    log_sig_neg = jnp.minimum(-zf, 0.0) - soft
    lo_term = loglb_ref[...] + log_sig_neg
    top = jnp.maximum(log_sig_pos, lo_term)
    logf = top + jnp.log(1.0 + jnp.exp(-jnp.abs(log_sig_pos - lo_term)))
    pre_ref[HG_K] = omlb_ref[...] * jnp.exp(log_sig_neg)
    row = _iota2((tile, tile), 0)
    col = _iota2((tile, tile), 1)
    tri = ((row // blk == col // blk) & (col <= row)).astype(BF16)
    pre_ref[HG_BL] = _dot_xl2(tri, logf)
    yield

    t_row = _iota2((blk, dk), 0)
    for i in range(tile // blk):
        rows = slice(i * blk, (i + 1) * blk)
        for h in range(n_heads):
            lanes = slice(h * dk, (h + 1) * dk)
            qb = pre_ref[HG_Q, rows, lanes]
            kb = pre_ref[HG_K, rows, lanes]
            bb = pre_ref[HG_BL, rows, lanes]
            p0 = (i * n_heads + h) * blk
            for s in range(blk):
                causal = t_row >= s
                ps = jnp.where(causal, qb * kb[s:s + 1, :] * jnp.exp(bb - bb[s:s + 1, :]), 0.0)
                p_ref[p0:p0 + blk, s * dk:(s + 1) * dk] = ps.astype(BF16)
            yield


def _hgrn_back(sel_ref, pre_ref, p_ref, a_ref, st_ref, o_ref, *, n_heads, tile):
    dk = HG_HEAD_DIM
    blk = HG_BLOCK
    n_blocks = tile // blk
    a_ref[...] = _dot(p_ref[...], sel_ref[...])
    yield

    block_rows = [slice(i * blk, (i + 1) * blk) for i in range(n_blocks)]
    e_end = [pre_ref[HG_BL, (i + 1) * blk - 1:(i + 1) * blk, :] for i in range(n_blocks)]
    start = [jnp.zeros_like(e_end[0])]
    for i in range(n_blocks):
        start.append(start[i] + e_end[i])
    qe = [pre_ref[HG_Q, r, :] * jnp.exp(pre_ref[HG_BL, r, :]) for r in block_rows]
    ke = [pre_ref[HG_K, r, :] * jnp.exp(e - pre_ref[HG_BL, r, :])
          for r, e in zip(block_rows, e_end)]
    vv = [pre_ref[HG_V, r, :] for r in block_rows]
    q_tile = jnp.concatenate([x * jnp.exp(s) for x, s in zip(qe, start)], axis=0)
    k_tile = jnp.concatenate([x * jnp.exp(start[n_blocks] - s) for x, s in zip(ke, start[1:])], axis=0)
    v_tile = pre_ref[HG_V]
    decay_tile = jnp.exp(start[n_blocks])
    yield

    heads = range(n_heads)
    lanes = [slice(h * dk, (h + 1) * dk) for h in heads]
    st = [st_ref[h] for h in heads]
    new_st = [st[h] * decay_tile[:, lanes[h]] + _dot1(v_tile[:, lanes[h]], k_tile[:, lanes[h]], TN)
              for h in heads]
    yield
    o_state = [_dot1(q_tile[:, lanes[h]], st[h], NT) for h in heads]
    o_blocks = [[o_state[h][r] for r in block_rows] for h in heads]
    yield
    span = n_blocks // 2
    while span >= 1:
        for lo in range(0, n_blocks, 2 * span):
            mid, hi = lo + span, lo + 2 * span
            q_r = jnp.concatenate([qe[i] if i == mid else qe[i] * jnp.exp(start[i] - start[mid])
                                   for i in range(mid, hi)], axis=0)
            k_l = jnp.concatenate([ke[j] if j == mid - 1 else ke[j] * jnp.exp(start[mid] - start[j + 1])
                                   for j in range(lo, mid)], axis=0)
            v_l = jnp.concatenate(vv[lo:mid], axis=0)
            scores = [_dot1(q_r[:, lanes[h]], k_l[:, lanes[h]], NT) for h in heads]
            upd = [_dot1(scores[h], v_l[:, lanes[h]]) for h in heads]
            for h in heads:
                for n, i in enumerate(range(mid, hi)):
                    o_blocks[h][i] = o_blocks[h][i] + upd[h][n * blk:(n + 1) * blk]
            yield
        span //= 2
    for i in range(n_blocks):
        for h in heads:
            p0 = (i * n_heads + h) * blk
            o_blocks[h][i] = o_blocks[h][i] + _dot1(a_ref[p0:p0 + blk, 0:blk], vv[i][:, lanes[h]])
        yield
    for h in heads:
        o = jnp.concatenate(o_blocks[h], axis=0)
        o = o * lax.rsqrt(jnp.mean(o * o, axis=-1, keepdims=True) + NORM_EPS)
        o_ref[0, :, lanes[h]] = (o * pre_ref[HG_GATE, :, lanes[h]]).astype(o_ref.dtype)
    for h in heads:
        st_ref[h] = new_st[h]


RW_KT, RW_RT, RW_V, RW_BH, RW_KH, RW_BL, RW_KL, RW_C, RW_RK, RW_G = range(10)


def _block_diag(y):
    lo = _iota2(y.shape, 1) < RW_HEAD_DIM
    y = y.astype(BF16)
    zero = jnp.zeros_like(y)
    return jnp.concatenate([jnp.where(lo, y, zero), jnp.where(lo, zero, y)], axis=0)


def _pair_dot3(x, y, z=None):
    n = x.shape[0]
    xh, xl = _split2(x)
    parts = [_split2(y)] if z is None else [_split2(y), _split2(z)]
    w_hi = jnp.concatenate([_block_diag(hi) for hi, _ in parts], axis=1)
    w_lo = jnp.concatenate([_block_diag(lo) for _, lo in parts], axis=1)
    both = _dot(jnp.concatenate([xh, xl], axis=0), w_hi)
    return both[0:n] + both[n:] + _dot(xh, w_lo)


def _pair_dot(x, y):
    return _dot(x.astype(BF16), _block_diag(y))


def _pair_dot_nt(x, y):
    return _dot(x.astype(BF16), _block_diag(y), NT)


def _pair_dot2(x, y, z):
    return _dot(x.astype(BF16), jnp.concatenate([_block_diag(y), _block_diag(z)], axis=1))


def _pair_dot2_nt(x, y, z):
    return _dot(x.astype(BF16), jnp.concatenate([_block_diag(y), _block_diag(z)], axis=0), NT)


def _unit_lower_inverses(n_list, eye, same_sub):
    d = [jnp.where(same_sub, n, 0.0) for n in n_list]
    l_off = _map(lambda n, dd: n - dd, n_list, d)
    t_d = [eye - dd for dd in d]
    n = n_list[0].shape[0]
    d_pow = _map(_pair_dot3, d, d)
    yield
    for _ in range(RW_SUB.bit_length() - 3):
        both = _map(lambda dp, t: _pair_dot3(jnp.concatenate([dp, t], axis=0), dp), d_pow, t_d)
        t_d = _map(lambda t, b: t + b[n:], t_d, both)
        d_pow = [b[0:n] for b in both]
        yield
    t_d = _map(lambda t, dp: t + _pair_dot3(t, dp), t_d, d_pow)
    yield
    p = _map(_pair_dot, t_d, l_off)
    yield
    p2 = _map(_pair_dot, p, p)
    yield
    t_p = _map(lambda pp, pp2: (eye - pp) + _pair_dot(eye - pp, pp2), p, p2)
    yield
    return _map(_pair_dot, t_p, t_d)


def _rwkv_front(z_ref, mu_ref, w0_ref, w2_ref, a0_ref, a2_ref, g2_ref, kk_ref, ka_ref, rk_ref,
                hones_ref, pre_ref, carry_ref, zs_ref, *, n_heads, chunk, tile):
    hd = RW_HEAD_DIM
    width = n_heads * hd
    z = z_ref[0]
    zs_ref[0:SUBLANES, :] = carry_ref[...]
    zs_ref[SUBLANES:SUBLANES + tile, :] = z
    carry_ref[...] = z[tile - SUBLANES:tile, :]
    z_prev = zs_ref[SUBLANES - 1:SUBLANES - 1 + tile, :]
    zm = z + (z_prev - z) * mu_ref[...]

    r = zm[:, 0:width]
    k = zm[:, width:2 * width]
    v = zm[:, 2 * width:3 * width]
    lora_in = zm[:, 3 * width:3 * width + 2 * RW_LORA_W]
    gd = zm[:, 3 * width + 2 * RW_LORA_W:]
    pre_ref[RW_V] = v
    pre_ref[RW_G] = _dot(_sigmoid(gd).astype(BF16), g2_ref[...])
    yield

    w_pre = w0_ref[...] + _dot_xr2(jnp.tanh(lora_in), w2_ref[...])
    w_raw = -_softplus(-w_pre) - 0.5
    logw = -jnp.exp(w_raw)
    a = _sigmoid(a0_ref[...] + _dot1(lora_in, a2_ref[...]))
    kk = k * kk_ref[...]
    kk = kk * jnp.minimum(lax.rsqrt(_head_sums(kk * kk, hones_ref[...])), 1e12)
    k2 = k * (1.0 + (a - 1.0) * ka_ref[...])
    beta = kk * a
    pre_ref[RW_RK] = r * k2 * rk_ref[...]
    yield

    trow = _iota2((tile, tile), 0)
    tcol = _iota2((tile, tile), 1)
    tri = ((trow // chunk == tcol // chunk) & (tcol <= trow)).astype(BF16)
    c = _dot_xl2(tri, logw)
    pre_ref[RW_C] = c
    e_neg = jnp.exp(-c)
    pre_ref[RW_KT] = kk * jnp.exp(c - logw)
    pre_ref[RW_RT] = r * jnp.exp(c)
    pre_ref[RW_BH] = beta * e_neg
    pre_ref[RW_KH] = k2 * e_neg
    yield
    for ci in range(tile // chunk):
        rows = slice(ci * chunk, (ci + 1) * chunk)
        e_end = jnp.exp(c[rows][chunk - 1:chunk, :] - c[rows])
        pre_ref[RW_BL, rows, :] = beta[rows] * e_end
        pre_ref[RW_KL, rows, :] = k2[rows] * e_end
    yield


def _rwkv_back(pre_ref, hones_ref, lnw_ref, lnb_ref, s_ref, y_ref, o_ref, *, n_heads, chunk, tile):
    hd = RW_HEAD_DIM
    n_chunks = tile // chunk
    pw = 2 * hd
    pairs = range(n_heads // 2)
    lanes = [slice(p * pw, (p + 1) * pw) for p in pairs]
    row = _iota2((chunk, pw), 0)
    lane = _iota2((chunk, pw), 1)
    lo_half = lane < hd
    col = jnp.where(lo_half, lane, lane - hd)
    incl2 = jnp.concatenate([col < row, col <= row], axis=0)
    incl4 = jnp.concatenate([incl2, incl2], axis=1)
    eye = (row == col).astype(F32)
    same_sub = (row // RW_SUB) == (col // RW_SUB)
    same_head = (_iota2((pw, pw), 0) // hd) == (_iota2((pw, pw), 1) // hd)

    chunk_rows = [slice(ci * chunk, (ci + 1) * chunk) for ci in range(n_chunks)]
    units = [(ci, p) for ci in range(n_chunks) for p in pairs]
    at = lambda name: [pre_ref[name, chunk_rows[ci], lanes[p]] for ci, p in units]
    kt, rt, vv, b_l, k_l = at(RW_KT), at(RW_RT), at(RW_V), at(RW_BL), at(RW_KL)
    g_last = [jnp.exp(pre_ref[RW_C, (ci + 1) * chunk - 1:(ci + 1) * chunk, lanes[p]]) for ci, p in units]
    lhs = _map(lambda x, y: jnp.concatenate([x, y], axis=0), kt, rt)
    g_bk = [jnp.where(incl4, _pair_dot2_nt(l, xb, xk), 0.0)
            for l, xb, xk in zip(lhs, at(RW_BH), at(RW_KH))]
    a_ab = [x[0:chunk, 0:pw] for x in g_bk]
    a_rb = [x[chunk:, 0:pw] for x in g_bk]
    yield
    gkv = [_pair_dot3(x[:, pw:], v_) for x, v_ in zip(g_bk, vv)]
    x_loc = [x[0:chunk] for x in gkv]
    yield
    t_inv = yield from _unit_lower_inverses(a_ab, eye, same_sub)
    yield
    wu = _map(_pair_dot3, t_inv, kt, x_loc)
    w_t = [x[:, 0:pw] for x in wu]
    u_loc = [x[:, pw:] for x in wu]
    yield
    a_wu = _map(_pair_dot2, a_rb, w_t, u_loc)
    q_c = _map(lambda r_, x: r_ - x[:, 0:pw], rt, a_wu)
    y_loc = _map(lambda x, y: x[chunk:] - y[:, pw:], gkv, a_wu)
    yield
    m_off = _map(lambda w_, bl_: jnp.where(same_head, -_dot1(w_, bl_, TN), 0.0), w_t, b_l)
    yield
    c_full = _map(lambda v_, u_, kl_, bl_: _dot1(jnp.concatenate([v_, -u_], axis=0),
                                                 jnp.concatenate([kl_, bl_], axis=0), TN),
                  vv, u_loc, k_l, b_l)
    c_add = [jnp.where(lo_half, x[0:hd], x[hd:]) for x in c_full]
    yield
    state = [s_ref[p] for p in pairs]
    for idx, (ci, p) in enumerate(units):
        y_ref[chunk_rows[ci], lanes[p]] = _pair_dot_nt(q_c[idx], state[p]) + y_loc[idx]
        state[p] = state[p] * g_last[idx] + _dot1(state[p], m_off[idx]) + c_add[idx]
    for p in pairs:
        s_ref[p] = state[p]
    yield

    y = y_ref[...]
    head_ones = hones_ref[...]
    mean = _head_sums(y, head_ones) * (1.0 / hd)
    yc = y - mean
    var = _head_sums(yc * yc, head_ones) * (1.0 / hd)
    yn = yc * lax.rsqrt(var + RW_GN_EPS) * lnw_ref[...] + lnb_ref[...]
    bonus = _head_sums(pre_ref[RW_RK], head_ones) * pre_ref[RW_V]
    o_ref[0] = ((yn + bonus) * pre_ref[RW_G]).astype(o_ref.dtype)


_DONE = object()
MIX_STEPS = (3, 2, 1, 4)


def _mixer_kernel(zhg_ref, zrw_ref, loglb_ref, omlb_ref, gn_ref, sel_ref, s0hg_ref,
                  shift_ref, s0rw_ref, mu_ref, w0_ref, w2_ref, a0_ref, a2_ref, g2_ref,
                  kk_ref, ka_ref, rk_ref, lnw_ref, lnb_ref, hones_ref,
                  ohg_ref, sfin_hg_ref, orw_ref, sfin_rw_ref,
                  st_ref, s_ref, carry_ref, zs_ref, y_ref, a_ref,
                  hg_pre_a, hg_pre_b, p_a, p_b, rw_pre_a, rw_pre_b,
                  *, hg_heads, rw_heads, tile, n_tiles, pipelined, seqs):
    k = pl.program_id(0)
    k_back = k - 1 if pipelined else k
    hg = dict(n_heads=hg_heads, tile=tile)
    rw = dict(n_heads=rw_heads, chunk=RW_CHUNK, tile=tile)
    every = range(seqs)
    one = lambda ref, b: ref.at[b:b + 1]
    own = lambda ref, b: ref.at[b]

    @pl.when(k % n_tiles == 0)
    def _():
        for b in every:
            carry_ref[b] = jnp.broadcast_to(shift_ref[b], carry_ref.shape[1:])

    @pl.when((k_back % n_tiles == 0) | (k == 0))
    def _():
        for b in every:
            for h in range(hg_heads):
                st_ref[b, h] = s0hg_ref[b, h].T
            s_ref[b] = s0rw_ref[b]

    if pipelined:
        @pl.when(k == 0)
        def _():
            hg_pre_b[...] = jnp.zeros(hg_pre_b.shape, hg_pre_b.dtype)
            p_b[...] = jnp.zeros(p_b.shape, p_b.dtype)
            rw_pre_b[...] = jnp.zeros(rw_pre_b.shape, rw_pre_b.dtype)

    def fronts(b, hg_w, p_w, rw_w):
        return [_rwkv_front(one(zrw_ref, b), mu_ref, w0_ref, w2_ref, a0_ref, a2_ref, g2_ref, kk_ref,
                            ka_ref, rk_ref, hones_ref, own(rw_w, b), own(carry_ref, b),
                            own(zs_ref, b), **rw),
                _hgrn_front(one(zhg_ref, b), loglb_ref, omlb_ref, gn_ref, own(hg_w, b), own(p_w, b), **hg)]

    def backs(b, hg_r, p_r, rw_r):
        return [_rwkv_back(own(rw_r, b), hones_ref, lnw_ref, lnb_ref, own(s_ref, b), own(y_ref, b),
                           one(orw_ref, b), **rw),
                _hgrn_back(sel_ref, own(hg_r, b), own(p_r, b), own(a_ref, b), own(st_ref, b),
                           one(ohg_ref, b), **hg)]

    def run(bodies, steps):
        live = [True] * len(bodies)
        while any(live):
            for n, body in enumerate(bodies):
                for _ in range(steps[n % len(steps)]):
                    if live[n]:
                        live[n] = next(body, _DONE) is not _DONE

    if not pipelined:
        run([f for b in every for f in fronts(b, hg_pre_a, p_a, rw_pre_a)], MIX_STEPS[2:])
        run([f for b in every for f in backs(b, hg_pre_a, p_a, rw_pre_a)], MIX_STEPS[:2])
    else:
        @pl.when(k % 2 == 0)
        def _():
            run(backs(0, hg_pre_b, p_b, rw_pre_b) + fronts(0, hg_pre_a, p_a, rw_pre_a), MIX_STEPS)

        @pl.when(k % 2 == 1)
        def _():
            run(backs(0, hg_pre_a, p_a, rw_pre_a) + fronts(0, hg_pre_b, p_b, rw_pre_b), MIX_STEPS)

    @pl.when((k_back >= 0) & (k_back % n_tiles == n_tiles - 1))
    def _():
        for b in every:
            for h in range(hg_heads):
                sfin_hg_ref[b, h] = st_ref[b, h].T
            sfin_rw_ref[b] = s_ref[b]


def _mixers(z_hg, z_rw, log_lb, one_m_lb, hg_norm, s0_hg, shift_prev, s0_rw,
            mu, w0, w2p, a0, a2p, g2, k_k, k_a, r_k, ln_w, ln_b):
    bsz, t_len, hg_proj = z_hg.shape
    rw_proj = z_rw.shape[2]
    hg_width = hg_proj // 4
    hg_heads = hg_width // HG_HEAD_DIM
    rw_heads = s0_rw.shape[1]
    hd = RW_HEAD_DIM
    rw_width = rw_heads * hd
    tile = min(MIX_TILE, t_len)
    assert t_len % tile == 0 and tile % RW_CHUNK == 0 and tile % HG_BLOCK == 0 and rw_heads % 2 == 0
    n_tiles = t_len // tile
    n_pairs = rw_heads // 2
    s0_rw = s0_rw.reshape(bsz, n_pairs, 2, hd, hd).transpose(0, 1, 3, 2, 4).reshape(bsz, n_pairs, hd, 2 * hd)
    group = jnp.arange(HG_BLOCK * HG_HEAD_DIM, dtype=jnp.int32)[:, None] // HG_HEAD_DIM
    selector = (group == jnp.arange(LANES, dtype=jnp.int32)[None, :]).astype(BF16)
    head_of = jnp.arange(LANES, dtype=jnp.int32) // hd
    head_ones = (head_of[:, None] == head_of[None, :]).astype(BF16)

    assert RW_CHUNK == RW_HEAD_DIM
    pipelined = n_tiles > 1
    seqs = 1 if pipelined else min(bsz, MIX_SEQS)
    assert bsz % seqs == 0
    total = bsz * n_tiles // seqs
    front_k = lambda k: jnp.minimum(k, total - 1)
    back_k = (lambda k: jnp.maximum(k - 1, 0)) if pipelined else front_k
    tok_in = lambda w: pl.BlockSpec((seqs, tile, w), lambda k: (front_k(k) // n_tiles, front_k(k) % n_tiles, 0))
    tok_out = lambda w: pl.BlockSpec((seqs, tile, w), lambda k: (back_k(k) // n_tiles, back_k(k) % n_tiles, 0))
    per_seq = lambda a, which: pl.BlockSpec((seqs,) + a.shape[1:],
                                            lambda k: (which(k) // n_tiles,) + (0,) * (a.ndim - 1))
    per_seq_scratch = lambda shape, dtype: pltpu.VMEM((seqs,) + shape, dtype)
    consts_hg = [log_lb, one_m_lb, hg_norm, selector]
    consts_rw = [mu, w0, w2p, a0, a2p, g2, k_k, k_a, r_k, ln_w, ln_b, head_ones]
    kern = functools.partial(_mixer_kernel, hg_heads=hg_heads, rw_heads=rw_heads, tile=tile,
                             n_tiles=n_tiles, pipelined=pipelined, seqs=seqs)
    o_hg, n_hg, o_rw, n_rw = pl.pallas_call(
        kern,
        grid=(total + 1 if pipelined else total,),
        in_specs=[tok_in(hg_proj), tok_in(rw_proj)]
                 + [_const_spec(a.shape) for a in consts_hg] + [per_seq(s0_hg, back_k)]
                 + [per_seq(shift_prev, front_k), per_seq(s0_rw, back_k)]
                 + [_const_spec(a.shape) for a in consts_rw],
        out_specs=[tok_out(hg_width), per_seq(s0_hg, back_k), tok_out(rw_width), per_seq(s0_rw, back_k)],
        out_shape=[jax.ShapeDtypeStruct((bsz, t_len, hg_width), BF16),
                   jax.ShapeDtypeStruct(s0_hg.shape, F32),
                   jax.ShapeDtypeStruct((bsz, t_len, rw_width), BF16),
                   jax.ShapeDtypeStruct(s0_rw.shape, F32)],
        scratch_shapes=[per_seq_scratch((hg_heads, HG_HEAD_DIM, HG_HEAD_DIM), F32),
                        per_seq_scratch((n_pairs, hd, 2 * hd), F32),
                        per_seq_scratch((SUBLANES, rw_proj), F32),
                        per_seq_scratch((SUBLANES + tile, rw_proj), F32),
                        per_seq_scratch((tile, rw_width), F32),
                        per_seq_scratch((tile * hg_heads, LANES), F32)]
                       + [per_seq_scratch((5, tile, hg_width), F32)] * 2
                       + [per_seq_scratch((tile * hg_heads, HG_BLOCK * HG_HEAD_DIM), BF16)] * 2
                       + [per_seq_scratch((10, tile, rw_width), F32)] * 2,
        compiler_params=pltpu.CompilerParams(dimension_semantics=("arbitrary",),
                                             vmem_limit_bytes=VMEM_LIMIT),
        name="mixers",
    )(z_hg, z_rw, *consts_hg, s0_hg, shift_prev, s0_rw, *consts_rw)
    n_rw = n_rw.reshape(bsz, n_pairs, hd, 2, hd).transpose(0, 1, 3, 2, 4).reshape(bsz, rw_heads, hd, hd)
    return o_hg, n_hg, o_rw, n_rw


def _gelu_exact(x):
    return 0.5 * x * (1.0 + lax.erf(x * (2.0 ** -0.5)))


def _post_kernel(h_ref, ohg_ref, orw_ref, p_ref, cvprev_ref,
                 wout_ref, nffn_ref, win_ref, cw_ref, cb_ref, wo_ref,
                 nple_ref, pgate_ref, pproj_ref, fnorm_ref,
                 hout_ref, cvnew_ref, carry_ref, ugs_ref, *, seqs, tm, n_col_blocks, apply_final_norm):
    t_idx = pl.program_id(1)
    cols = FF_COLS
    d_ff = n_col_blocks * cols
    hg_width = ohg_ref.shape[-1]
    rows = seqs * tm
    flat = lambda ref: ref[...].reshape(rows, ref.shape[-1])
    col_block = lambda j: slice(j * cols, (j + 1) * cols)
    up = lambda j: (_dot(xn, win_ref[:, col_block(j)]),
                    _dot(xn, win_ref[:, d_ff + j * cols:d_ff + (j + 1) * cols]))

    @pl.when(t_idx == 0)
    def _():
        carry_ref[...] = jnp.zeros(carry_ref.shape, carry_ref.dtype)
        for j in range(n_col_blocks):
            for b in range(seqs):
                carry_ref[j, b, SUBLANES - (CONV_W - 1):SUBLANES, :] = cvprev_ref[b, :, col_block(j)]

    h1 = (flat(h_ref) + _dot(flat(ohg_ref).astype(BF16), wout_ref[0:hg_width, :])
          + _dot(flat(orw_ref).astype(BF16), wout_ref[hg_width:, :]))
    xn = _rmsnorm(h1, nffn_ref[...]).astype(BF16)
    acc = jnp.zeros_like(h1)
    up_next = up(0)
    for j in range(n_col_blocks):
        ug, uv = up_next
        if j + 1 < n_col_blocks:
            up_next = up(j + 1)
        cw = cw_ref[:, col_block(j)]
        conv = []
        for b in range(seqs):
            ug_b = ug[b * tm:(b + 1) * tm]
            ugs_ref[b, 0:SUBLANES, :] = carry_ref[j, b]
            ugs_ref[b, SUBLANES:SUBLANES + tm, :] = ug_b
            carry_ref[j, b] = ug_b[tm - SUBLANES:tm, :]
            cvnew_ref[b, :, col_block(j)] = ug_b[tm - (CONV_W - 1):tm, :]
            conv.append(cw[0:1, :] * ugs_ref[b, SUBLANES - 2:SUBLANES - 2 + tm, :]
                        + cw[1:2, :] * ugs_ref[b, SUBLANES - 1:SUBLANES - 1 + tm, :])
        c = cb_ref[:, col_block(j)] + jnp.concatenate(conv, axis=0) + cw[2:3, :] * ug
        hid = (_gelu_exact(c) * uv).astype(BF16)
        acc = acc + _dot(hid, wo_ref[col_block(j), :])
    h2 = h1 + acc
    gate = _sigmoid(_dot(_rmsnorm(h2, nple_ref[...]).astype(BF16), pgate_ref[...]))
    h3 = h2 + gate * _dot(flat(p_ref).astype(BF16), pproj_ref[...])
    if apply_final_norm:
        h3 = _rmsnorm(h3, fnorm_ref[...])
    hout_ref[...] = h3.reshape(hout_ref.shape)


def _post(h, o_hg, o_rw, p_all, layer, cv_prev, w_out, norm_ffn, ffn_in, cw, cb, wo, norm_ple,
          ple_gate, ple_proj, final_norm, apply_final_norm):
    bsz, t_len, d = h.shape
    tm = min(POST_TILE, t_len)
    seqs = min(bsz, POST_TILE // tm)
    assert t_len % tm == 0 and tm % SUBLANES == 0 and bsz % seqs == 0
    d_ff = cb.shape[1]
    n_col_blocks = d_ff // FF_COLS
    assert d_ff % FF_COLS == 0
    kern = functools.partial(_post_kernel, seqs=seqs, tm=tm, n_col_blocks=n_col_blocks,
                             apply_final_norm=apply_final_norm)
    tok = lambda w: pl.BlockSpec((seqs, tm, w), lambda b, t: (b, t, 0))
    p_spec = pl.BlockSpec((None, seqs, tm, p_all.shape[-1]), lambda b, t: (layer, b, t, 0))
    cv_spec = pl.BlockSpec((seqs, CONV_W - 1, d_ff), lambda b, t: (b, 0, 0))
    consts = [w_out, norm_ffn, ffn_in, cw, cb, wo, norm_ple, ple_gate, ple_proj, final_norm]
    return pl.pallas_call(
        kern,
        grid=(bsz // seqs, t_len // tm),
        in_specs=[tok(d), tok(o_hg.shape[2]), tok(o_rw.shape[2]), p_spec, cv_spec]
                 + [_const_spec(a.shape) for a in consts],
        out_specs=[tok(d), cv_spec],
        out_shape=[jax.ShapeDtypeStruct(h.shape, F32), jax.ShapeDtypeStruct(cv_prev.shape, F32)],
        scratch_shapes=[pltpu.VMEM((n_col_blocks, seqs, SUBLANES, FF_COLS), F32),
                        pltpu.VMEM((seqs, SUBLANES + tm, FF_COLS), F32)],
        compiler_params=pltpu.CompilerParams(dimension_semantics=("parallel", "arbitrary"),
                                             vmem_limit_bytes=VMEM_LIMIT),
        name="post",
    )(h, o_hg, o_rw, p_all, cv_prev, *consts)


def _layer_params(i, hg_proj, lb_all, w):
    rw_width = w["rw_w0"].shape[1]
    row = lambda a: a.reshape(1, -1)
    lb = jnp.maximum(lb_all[i], 0.0)
    zeros_lora = jnp.zeros((RW_LORA_W, rw_width), F32)
    return dict(
        norm_mix=row(w["norm_mix"][i]),
        w_in=w["w_in"][i].astype(BF16),
        log_lb=row(jnp.log(lb)), one_m_lb=row(1.0 - lb), hg_norm=row(w["hg_norm"][i]),
        mu=row(w["rw_mu"][i]), w0=row(w["rw_w0"][i]),
        w2p=jnp.concatenate([w["rw_w2"][i], zeros_lora], axis=0).astype(BF16),
        a0=row(w["rw_a0"][i]),
        a2p=jnp.concatenate([zeros_lora, w["rw_a2"][i]], axis=0).astype(BF16),
        g2=w["rw_g2"][i].astype(BF16),
        k_k=row(w["rw_k_k"][i]), k_a=row(w["rw_k_a"][i]), r_k=row(w["rw_r_k"][i]),
        ln_w=row(w["rw_ln_w"][i]), ln_b=row(w["rw_ln_b"][i]),
        w_out=w["w_out"][i].astype(BF16),
        norm_ffn=row(w["norm_ffn"][i]),
        ffn_in=w["ffn_in"][i].astype(BF16),
        cw=w["ffn_conv_w"][i], cb=row(w["ffn_conv_b"][i]),
        wo=w["ffn_out"][i].astype(BF16),
        norm_ple=row(w["norm_ple"][i]),
        ple_gate=w["ple_gate"][i].astype(BF16), ple_proj=w["ple_proj"][i].astype(BF16),
        final_norm=row(w["final_norm"]),
    )


def _trunk(x, p, s_hg, s_rw, s_sh, s_cv, layers):
    bsz, t_len, d = x.shape
    depth = len(layers)
    h = x
    new_hg, new_rw, new_sh, new_cv = [], [], [], []
    for i, lp in enumerate(layers):
        z_hg, z_rw = _in_proj(h.reshape(bsz * t_len, d), lp["norm_mix"], lp["w_in"], 4 * lp["hg_norm"].shape[1])
        z_hg = z_hg.reshape(bsz, t_len, -1)
        z_rw = z_rw.reshape(bsz, t_len, -1)
        o_hg, n_hg, o_rw, n_rw = _mixers(
            z_hg, z_rw, lp["log_lb"], lp["one_m_lb"], lp["hg_norm"], s_hg[i], s_sh[i][:, None, :],
            s_rw[i], lp["mu"], lp["w0"], lp["w2p"], lp["a0"], lp["a2p"], lp["g2"], lp["k_k"],
            lp["k_a"], lp["r_k"], lp["ln_w"], lp["ln_b"])
        h, n_cv = _post(h, o_hg, o_rw, p, i, s_cv[i], lp["w_out"], lp["norm_ffn"],
                        lp["ffn_in"], lp["cw"], lp["cb"], lp["wo"], lp["norm_ple"],
                        lp["ple_gate"], lp["ple_proj"], lp["final_norm"], i == depth - 1)
        new_hg.append(n_hg)
        new_rw.append(n_rw)
        new_sh.append(z_rw[:, -1])
        new_cv.append(n_cv)
    return h, jnp.stack(new_hg), jnp.stack(new_rw), jnp.stack(new_sh), jnp.stack(new_cv)


def kernel(x_prompt, x_sample, state_hgrn, state_rwkv, state_shift, state_conv, p_prompt, p_sample,
           norm_mix, w_in, lb_raw, hg_norm, rw_mu, rw_w0, rw_w2, rw_a0, rw_a2, rw_g2, rw_k_k, rw_k_a,
           rw_r_k, rw_ln_w, rw_ln_b, w_out, norm_ffn, ffn_in, ffn_conv_w, ffn_conv_b, ffn_out,
           norm_ple, ple_gate, ple_proj, final_norm):
    w = dict(norm_mix=norm_mix, w_in=w_in, hg_norm=hg_norm, rw_mu=rw_mu, rw_w0=rw_w0, rw_w2=rw_w2,
             rw_a0=rw_a0, rw_a2=rw_a2, rw_g2=rw_g2, rw_k_k=rw_k_k, rw_k_a=rw_k_a,
             rw_r_k=rw_r_k.reshape(rw_r_k.shape[0], -1), rw_ln_w=rw_ln_w, rw_ln_b=rw_ln_b, w_out=w_out,
             norm_ffn=norm_ffn, ffn_in=ffn_in, ffn_conv_w=ffn_conv_w, ffn_conv_b=ffn_conv_b,
             ffn_out=ffn_out, norm_ple=norm_ple, ple_gate=ple_gate, ple_proj=ple_proj,
             final_norm=final_norm)
    depth = w_in.shape[0]
    hg_proj = 4 * hg_norm.shape[1]
    lb_all = jnp.cumsum(jax.nn.softmax(lb_raw.astype(F32), axis=0), axis=0)
    lb_all = lb_all - lb_all[:1]
    layers = [_layer_params(i, hg_proj, lb_all, w) for i in range(depth)]

    nb = x_prompt.shape[0]
    zeros_like_state = lambda s: jnp.zeros((depth, nb) + s.shape[2:], s.dtype)
    y_p, hg_p, rw_p, sh_p, cv_p = _trunk(x_prompt, p_prompt, zeros_like_state(state_hgrn),
                                         zeros_like_state(state_rwkv), zeros_like_state(state_shift),
                                         zeros_like_state(state_conv), layers)
    y_s, hg_s, rw_s, sh_s, cv_s = _trunk(x_sample, p_sample, state_hgrn, state_rwkv, state_shift,
                                         state_conv, layers)
    return (y_p, y_s, hg_p, rw_p, sh_p, cv_p, hg_s, rw_s, sh_s, cv_s)
```

```python
import functools

import jax
import jax.numpy as jnp
from jax import lax
from jax.experimental import pallas as pl
from jax.experimental.pallas import tpu as pltpu

F32 = jnp.float32
BF16 = jnp.bfloat16

NORM_EPS = 1e-6
RW_GN_EPS = 64e-5
HG_HEAD_DIM = 128
RW_HEAD_DIM = 64
RW_LORA_W = 64
CONV_W = 3
HG_BLOCK = 16
RW_CHUNK = 64
RW_SUB = 16
MIX_TILE = 256
MIX_SEQS = 4
FF_COLS = 256
POST_TILE = 512
LANES = 128
SUBLANES = 8
VMEM_LIMIT = 56 * 1024 * 1024

NN = ((1,), (0,))
NT = ((1,), (1,))
TN = ((0,), (0,))


def _dot(a, b, dims=NN):
    return lax.dot_general(a, b, (dims, ((), ())), preferred_element_type=F32)


def _dot1(a, b, dims=NN):
    return _dot(a.astype(BF16), b.astype(BF16), dims)


def _split2(x):
    hi = x.astype(BF16)
    lo = (x - hi.astype(F32)).astype(BF16)
    return hi, lo


def _dot_xl2(a_exact, x, dims=NN):
    hi, lo = _split2(x)
    return _dot(a_exact, hi, dims) + _dot(a_exact, lo, dims)


def _dot_xr2(x, b, dims=NN):
    hi, lo = _split2(x)
    b = b.astype(BF16)
    return _dot(hi, b, dims) + _dot(lo, b, dims)


def _head_sums(x, pair_ones):
    n, width = x.shape
    tiles = [x[:, p * LANES:(p + 1) * LANES] for p in range(width // LANES)]
    sums = _dot(jnp.concatenate(tiles, axis=0).astype(BF16), pair_ones)
    return jnp.concatenate([sums[p * n:(p + 1) * n] for p in range(width // LANES)], axis=1)


def _sigmoid(x):
    return 0.5 * jnp.tanh(0.5 * x) + 0.5


def _softplus(x):
    return jnp.maximum(x, 0.0) + jnp.log(1.0 + jnp.exp(-jnp.abs(x)))


def _rmsnorm(x, g):
    return x * lax.rsqrt(jnp.mean(x * x, axis=-1, keepdims=True) + NORM_EPS) * g


def _iota2(shape, axis):
    return lax.broadcasted_iota(jnp.int32, shape, axis)


def _const_spec(shape):
    zeros = (0,) * len(shape)
    return pl.BlockSpec(shape, lambda *_: zeros, pipeline_mode=pl.Buffered(1))


def _map(fn, *lists):
    return [fn(*xs) for xs in zip(*lists)]


def _inproj_kernel(x_ref, g_ref, w_ref, zhg_ref, zrw_ref):
    hg_proj = zhg_ref.shape[1]
    xn = _rmsnorm(x_ref[...], g_ref[...]).astype(BF16)
    zhg_ref[...] = _dot(xn, w_ref[:, 0:hg_proj])
    zrw_ref[...] = _dot(xn, w_ref[:, hg_proj:])


def _in_proj(x2, g, w_in, hg_proj):
    n, d = x2.shape
    tm = min(512, n)
    assert n % tm == 0 and hg_proj % LANES == 0
    rw_proj = w_in.shape[1] - hg_proj
    return pl.pallas_call(
        _inproj_kernel,
        grid=(n // tm,),
        in_specs=[pl.BlockSpec((tm, d), lambda i: (i, 0)),
                  _const_spec((1, d)),
                  _const_spec(w_in.shape)],
        out_specs=[pl.BlockSpec((tm, hg_proj), lambda i: (i, 0)),
                   pl.BlockSpec((tm, rw_proj), lambda i: (i, 0))],
        out_shape=[jax.ShapeDtypeStruct((n, hg_proj), F32),
                   jax.ShapeDtypeStruct((n, rw_proj), F32)],
        compiler_params=pltpu.CompilerParams(dimension_semantics=("parallel",),
                                             vmem_limit_bytes=VMEM_LIMIT),
        name="in_proj",
    )(x2, g, w_in)


HG_Q, HG_K, HG_BL, HG_V, HG_GATE = range(5)


def _hgrn_front(z_ref, loglb_ref, omlb_ref, gn_ref, pre_ref, p_ref, *, n_heads, tile):
    dk = HG_HEAD_DIM
    width = n_heads * dk
    blk = HG_BLOCK
    zq = z_ref[0, :, 0:width]
    zf = z_ref[0, :, width:2 * width]
    zg = z_ref[0, :, 3 * width:4 * width]
    pre_ref[HG_Q] = zq * _sigmoid(zq)
    pre_ref[HG_V] = z_ref[0, :, 2 * width:3 * width]
    pre_ref[HG_GATE] = gn_ref[...] * (zg * _sigmoid(zg))
    soft = jnp.log(1.0 + jnp.exp(-jnp.abs(zf)))
    log_sig_pos = jnp.minimum(zf, 0.0) - soft
    log_sig_neg = jnp.minimum(-zf, 0.0) - soft
    lo_term = loglb_ref[...] + log_sig_neg
    top = jnp.maximum(log_sig_pos, lo_term)
    logf = top + jnp.log(1.0 + jnp.exp(-jnp.abs(log_sig_pos - lo_term)))
    pre_ref[HG_K] = omlb_ref[...] * jnp.exp(log_sig_neg)
    row = _iota2((tile, tile), 0)
    col = _iota2((tile, tile), 1)
    tri = ((row // blk == col // blk) & (col <= row)).astype(BF16)
    pre_ref[HG_BL] = _dot_xl2(tri, logf)
    yield

    t_row = _iota2((blk, dk), 0)
    for i in range(tile // blk):
        rows = slice(i * blk, (i + 1) * blk)
        for h in range(n_heads):
            lanes = slice(h * dk, (h + 1) * dk)
            qb = pre_ref[HG_Q, rows, lanes]
            kb = pre_ref[HG_K, rows, lanes]
            bb = pre_ref[HG_BL, rows, lanes]
            p0 = (i * n_heads + h) * blk
            for s in range(blk):
                causal = t_row >= s
                ps = jnp.where(causal, qb * kb[s:s + 1, :] * jnp.exp(bb - bb[s:s + 1, :]), 0.0)
                p_ref[p0:p0 + blk, s * dk:(s + 1) * dk] = ps.astype(BF16)
            yield


def _hgrn_back(sel_ref, pre_ref, p_ref, a_ref, st_ref, o_ref, *, n_heads, tile):
    dk = HG_HEAD_DIM
    blk = HG_BLOCK
    n_blocks = tile // blk
    a_ref[...] = _dot(p_ref[...], sel_ref[...])
    yield

    block_rows = [slice(i * blk, (i + 1) * blk) for i in range(n_blocks)]
    e_end = [pre_ref[HG_BL, (i + 1) * blk - 1:(i + 1) * blk, :] for i in range(n_blocks)]
    start = [jnp.zeros_like(e_end[0])]
    for i in range(n_blocks):
        start.append(start[i] + e_end[i])
    qe = [pre_ref[HG_Q, r, :] * jnp.exp(pre_ref[HG_BL, r, :]) for r in block_rows]
    ke = [pre_ref[HG_K, r, :] * jnp.exp(e - pre_ref[HG_BL, r, :])
          for r, e in zip(block_rows, e_end)]
    vv = [pre_ref[HG_V, r, :] for r in block_rows]
    q_tile = jnp.concatenate([x * jnp.exp(s) for x, s in zip(qe, start)], axis=0)
    k_tile = jnp.concatenate([x * jnp.exp(start[n_blocks] - s) for x, s in zip(ke, start[1:])], axis=0)
    v_tile = pre_ref[HG_V]
    decay_tile = jnp.exp(start[n_blocks])
    yield

    heads = range(n_heads)
    lanes = [slice(h * dk, (h + 1) * dk) for h in heads]
    st = [st_ref[h] for h in heads]
    new_st = [st[h] * decay_tile[:, lanes[h]] + _dot1(v_tile[:, lanes[h]], k_tile[:, lanes[h]], TN)
              for h in heads]
    yield
    o_state = [_dot1(q_tile[:, lanes[h]], st[h], NT) for h in heads]
    o_blocks = [[o_state[h][r] for r in block_rows] for h in heads]
    yield
    span = n_blocks // 2
    while span >= 1:
        for lo in range(0, n_blocks, 2 * span):
            mid, hi = lo + span, lo + 2 * span
            q_r = jnp.concatenate([qe[i] if i == mid else qe[i] * jnp.exp(start[i] - start[mid])
                                   for i in range(mid, hi)], axis=0)
            k_l = jnp.concatenate([ke[j] if j == mid - 1 else ke[j] * jnp.exp(start[mid] - start[j + 1])
                                   for j in range(lo, mid)], axis=0)
            v_l = jnp.concatenate(vv[lo:mid], axis=0)
            scores = [_dot1(q_r[:, lanes[h]], k_l[:, lanes[h]], NT) for h in heads]
            upd = [_dot1(scores[h], v_l[:, lanes[h]]) for h in heads]
            for h in heads:
                for n, i in enumerate(range(mid, hi)):
                    o_blocks[h][i] = o_blocks[h][i] + upd[h][n * blk:(n + 1) * blk]
            yield
        span //= 2
    for i in range(n_blocks):
        for h in heads:
            p0 = (i * n_heads + h) * blk
            o_blocks[h][i] = o_blocks[h][i] + _dot1(a_ref[p0:p0 + blk, 0:blk], vv[i][:, lanes[h]])
        yield
    for h in heads:
        o = jnp.concatenate(o_blocks[h], axis=0)
        o = o * lax.rsqrt(jnp.mean(o * o, axis=-1, keepdims=True) + NORM_EPS)
        o_ref[0, :, lanes[h]] = (o * pre_ref[HG_GATE, :, lanes[h]]).astype(o_ref.dtype)
    for h in heads:
        st_ref[h] = new_st[h]


RW_KT, RW_RT, RW_V, RW_BH, RW_KH, RW_BL, RW_KL, RW_C, RW_RK, RW_G = range(10)


def _block_diag(y):
    lo = _iota2(y.shape, 1) < RW_HEAD_DIM
    y = y.astype(BF16)
    zero = jnp.zeros_like(y)
    return jnp.concatenate([jnp.where(lo, y, zero), jnp.where(lo, zero, y)], axis=0)


def _pair_dot3(x, y, z=None):
    n = x.shape[0]
    xh, xl = _split2(x)
    parts = [_split2(y)] if z is None else [_split2(y), _split2(z)]
    w_hi = jnp.concatenate([_block_diag(hi) for hi, _ in parts], axis=1)
    w_lo = jnp.concatenate([_block_diag(lo) for _, lo in parts], axis=1)
    both = _dot(jnp.concatenate([xh, xl], axis=0), w_hi)
    return both[0:n] + both[n:] + _dot(xh, w_lo)


def _pair_dot(x, y):
    return _dot(x.astype(BF16), _block_diag(y))


def _pair_dot_nt(x, y):
    return _dot(x.astype(BF16), _block_diag(y), NT)


def _pair_dot2(x, y, z):
    return _dot(x.astype(BF16), jnp.concatenate([_block_diag(y), _block_diag(z)], axis=1))


def _pair_dot2_nt(x, y, z):
    return _dot(x.astype(BF16), jnp.concatenate([_block_diag(y), _block_diag(z)], axis=0), NT)


def _unit_lower_inverses(n_list, eye, same_sub):
    d = [jnp.where(same_sub, n, 0.0) for n in n_list]
    l_off = _map(lambda n, dd: n - dd, n_list, d)
    t_d = [eye - dd for dd in d]
    n = n_list[0].shape[0]
    d_pow = _map(_pair_dot3, d, d)
    yield
    for _ in range(RW_SUB.bit_length() - 3):
        both = _map(lambda dp, t: _pair_dot3(jnp.concatenate([dp, t], axis=0), dp), d_pow, t_d)
        t_d = _map(lambda t, b: t + b[n:], t_d, both)
        d_pow = [b[0:n] for b in both]
        yield
    t_d = _map(lambda t, dp: t + _pair_dot3(t, dp), t_d, d_pow)
    yield
    p = _map(_pair_dot, t_d, l_off)
    yield
    p2 = _map(_pair_dot, p, p)
    yield
    t_p = _map(lambda pp, pp2: (eye - pp) + _pair_dot(eye - pp, pp2), p, p2)
    yield
    return _map(_pair_dot, t_p, t_d)


def _rwkv_front(z_ref, mu_ref, w0_ref, w2_ref, a0_ref, a2_ref, g2_ref, kk_ref, ka_ref, rk_ref,
                hones_ref, pre_ref, carry_ref, zs_ref, *, n_heads, chunk, tile):
    hd = RW_HEAD_DIM
    width = n_heads * hd
    z = z_ref[0]
    zs_ref[0:SUBLANES, :] = carry_ref[...]
    zs_ref[SUBLANES:SUBLANES + tile, :] = z
    carry_ref[...] = z[tile - SUBLANES:tile, :]
    z_prev = zs_ref[SUBLANES - 1:SUBLANES - 1 + tile, :]
    zm = z + (z_prev - z) * mu_ref[...]

    r = zm[:, 0:width]
    k = zm[:, width:2 * width]
    v = zm[:, 2 * width:3 * width]
    lora_in = zm[:, 3 * width:3 * width + 2 * RW_LORA_W]
    gd = zm[:, 3 * width + 2 * RW_LORA_W:]
    pre_ref[RW_V] = v
    pre_ref[RW_G] = _dot(_sigmoid(gd).astype(BF16), g2_ref[...])
    yield

    w_pre = w0_ref[...] + _dot_xr2(jnp.tanh(lora_in), w2_ref[...])
    w_raw = -_softplus(-w_pre) - 0.5
    logw = -jnp.exp(w_raw)
    a = _sigmoid(a0_ref[...] + _dot1(lora_in, a2_ref[...]))
    kk = k * kk_ref[...]
    kk = kk * jnp.minimum(lax.rsqrt(_head_sums(kk * kk, hones_ref[...])), 1e12)
    k2 = k * (1.0 + (a - 1.0) * ka_ref[...])
    beta = kk * a
    pre_ref[RW_RK] = r * k2 * rk_ref[...]
    yield

    trow = _iota2((tile, tile), 0)
    tcol = _iota2((tile, tile), 1)
    tri = ((trow // chunk == tcol // chunk) & (tcol <= trow)).astype(BF16)
    c = _dot_xl2(tri, logw)
    pre_ref[RW_C] = c
    e_neg = jnp.exp(-c)
    pre_ref[RW_KT] = kk * jnp.exp(c - logw)
    pre_ref[RW_RT] = r * jnp.exp(c)
    pre_ref[RW_BH] = beta * e_neg
    pre_ref[RW_KH] = k2 * e_neg
    yield
    for ci in range(tile // chunk):
        rows = slice(ci * chunk, (ci + 1) * chunk)
        e_end = jnp.exp(c[rows][chunk - 1:chunk, :] - c[rows])
        pre_ref[RW_BL, rows, :] = beta[rows] * e_end
        pre_ref[RW_KL, rows, :] = k2[rows] * e_end
    yield


def _rwkv_back(pre_ref, hones_ref, lnw_ref, lnb_ref, s_ref, y_ref, o_ref, *, n_heads, chunk, tile):
    hd = RW_HEAD_DIM
    n_chunks = tile // chunk
    pw = 2 * hd
    pairs = range(n_heads // 2)
    lanes = [slice(p * pw, (p + 1) * pw) for p in pairs]
    row = _iota2((chunk, pw), 0)
    lane = _iota2((chunk, pw), 1)
    lo_half = lane < hd
    col = jnp.where(lo_half, lane, lane - hd)
    incl2 = jnp.concatenate([col < row, col <= row], axis=0)
    incl4 = jnp.concatenate([incl2, incl2], axis=1)
    eye = (row == col).astype(F32)
    same_sub = (row // RW_SUB) == (col // RW_SUB)
    same_head = (_iota2((pw, pw), 0) // hd) == (_iota2((pw, pw), 1) // hd)

    chunk_rows = [slice(ci * chunk, (ci + 1) * chunk) for ci in range(n_chunks)]
    units = [(ci, p) for ci in range(n_chunks) for p in pairs]
    at = lambda name: [pre_ref[name, chunk_rows[ci], lanes[p]] for ci, p in units]
    kt, rt, vv, b_l, k_l = at(RW_KT), at(RW_RT), at(RW_V), at(RW_BL), at(RW_KL)
    g_last = [jnp.exp(pre_ref[RW_C, (ci + 1) * chunk - 1:(ci + 1) * chunk, lanes[p]]) for ci, p in units]
    lhs = _map(lambda x, y: jnp.concatenate([x, y], axis=0), kt, rt)
    g_bk = [jnp.where(incl4, _pair_dot2_nt(l, xb, xk), 0.0)
            for l, xb, xk in zip(lhs, at(RW_BH), at(RW_KH))]
    a_ab = [x[0:chunk, 0:pw] for x in g_bk]
    a_rb = [x[chunk:, 0:pw] for x in g_bk]
    yield
    gkv = [_pair_dot3(x[:, pw:], v_) for x, v_ in zip(g_bk, vv)]
    x_loc = [x[0:chunk] for x in gkv]
    yield
    t_inv = yield from _unit_lower_inverses(a_ab, eye, same_sub)
    yield
    wu = _map(_pair_dot3, t_inv, kt, x_loc)
    w_t = [x[:, 0:pw] for x in wu]
    u_loc = [x[:, pw:] for x in wu]
    yield
    a_wu = _map(_pair_dot2, a_rb, w_t, u_loc)
    q_c = _map(lambda r_, x: r_ - x[:, 0:pw], rt, a_wu)
    y_loc = _map(lambda x, y: x[chunk:] - y[:, pw:], gkv, a_wu)
    yield
    m_off = _map(lambda w_, bl_: jnp.where(same_head, -_dot1(w_, bl_, TN), 0.0), w_t, b_l)
    yield
    c_full = _map(lambda v_, u_, kl_, bl_: _dot1(jnp.concatenate([v_, -u_], axis=0),
                                                 jnp.concatenate([kl_, bl_], axis=0), TN),
                  vv, u_loc, k_l, b_l)
    c_add = [jnp.where(lo_half, x[0:hd], x[hd:]) for x in c_full]
    yield
    state = [s_ref[p] for p in pairs]
    for idx, (ci, p) in enumerate(units):
        y_ref[chunk_rows[ci], lanes[p]] = _pair_dot_nt(q_c[idx], state[p]) + y_loc[idx]
        state[p] = state[p] * g_last[idx] + _dot1(state[p], m_off[idx]) + c_add[idx]
    for p in pairs:
        s_ref[p] = state[p]
    yield

    y = y_ref[...]
    head_ones = hones_ref[...]
    mean = _head_sums(y, head_ones) * (1.0 / hd)
    yc = y - mean
    var = _head_sums(yc * yc, head_ones) * (1.0 / hd)
    yn = yc * lax.rsqrt(var + RW_GN_EPS) * lnw_ref[...] + lnb_ref[...]
    bonus = _head_sums(pre_ref[RW_RK], head_ones) * pre_ref[RW_V]
    o_ref[0] = ((yn + bonus) * pre_ref[RW_G]).astype(o_ref.dtype)


_DONE = object()
MIX_STEPS = (3, 2, 1, 4)


def _mixer_kernel(zhg_ref, zrw_ref, loglb_ref, omlb_ref, gn_ref, sel_ref, s0hg_ref,
                  shift_ref, s0rw_ref, mu_ref, w0_ref, w2_ref, a0_ref, a2_ref, g2_ref,
                  kk_ref, ka_ref, rk_ref, lnw_ref, lnb_ref, hones_ref,
                  ohg_ref, sfin_hg_ref, orw_ref, sfin_rw_ref,
                  st_ref, s_ref, carry_ref, zs_ref, y_ref, a_ref,
                  hg_pre_a, hg_pre_b, p_a, p_b, rw_pre_a, rw_pre_b,
                  *, hg_heads, rw_heads, tile, n_tiles, pipelined, seqs):
    k = pl.program_id(0)
    k_back = k - 1 if pipelined else k
    hg = dict(n_heads=hg_heads, tile=tile)
    rw = dict(n_heads=rw_heads, chunk=RW_CHUNK, tile=tile)
    every = range(seqs)
    one = lambda ref, b: ref.at[b:b + 1]
    own = lambda ref, b: ref.at[b]

    @pl.when(k % n_tiles == 0)
    def _():
        for b in every:
            carry_ref[b] = jnp.broadcast_to(shift_ref[b], carry_ref.shape[1:])

    @pl.when((k_back % n_tiles == 0) | (k == 0))
    def _():
        for b in every:
            for h in range(hg_heads):
                st_ref[b, h] = s0hg_ref[b, h].T
            s_ref[b] = s0rw_ref[b]

    if pipelined:
        @pl.when(k == 0)
        def _():
            hg_pre_b[...] = jnp.zeros(hg_pre_b.shape, hg_pre_b.dtype)
            p_b[...] = jnp.zeros(p_b.shape, p_b.dtype)
            rw_pre_b[...] = jnp.zeros(rw_pre_b.shape, rw_pre_b.dtype)

    def fronts(b, hg_w, p_w, rw_w):
        return [_rwkv_front(one(zrw_ref, b), mu_ref, w0_ref, w2_ref, a0_ref, a2_ref, g2_ref, kk_ref,
                            ka_ref, rk_ref, hones_ref, own(rw_w, b), own(carry_ref, b),
                            own(zs_ref, b), **rw),
                _hgrn_front(one(zhg_ref, b), loglb_ref, omlb_ref, gn_ref, own(hg_w, b), own(p_w, b), **hg)]

    def backs(b, hg_r, p_r, rw_r):
        return [_rwkv_back(own(rw_r, b), hones_ref, lnw_ref, lnb_ref, own(s_ref, b), own(y_ref, b),
                           one(orw_ref, b), **rw),
                _hgrn_back(sel_ref, own(hg_r, b), own(p_r, b), own(a_ref, b), own(st_ref, b),
                           one(ohg_ref, b), **hg)]

    def run(bodies, steps):
        live = [True] * len(bodies)
        while any(live):
            for n, body in enumerate(bodies):
                for _ in range(steps[n % len(steps)]):
                    if live[n]:
                        live[n] = next(body, _DONE) is not _DONE

    if not pipelined:
        run([f for b in every for f in fronts(b, hg_pre_a, p_a, rw_pre_a)], MIX_STEPS[2:])
        run([f for b in every for f in backs(b, hg_pre_a, p_a, rw_pre_a)], MIX_STEPS[:2])
    else:
        @pl.when(k % 2 == 0)
        def _():
            run(backs(0, hg_pre_b, p_b, rw_pre_b) + fronts(0, hg_pre_a, p_a, rw_pre_a), MIX_STEPS)

        @pl.when(k % 2 == 1)
        def _():
            run(backs(0, hg_pre_a, p_a, rw_pre_a) + fronts(0, hg_pre_b, p_b, rw_pre_b), MIX_STEPS)

    @pl.when((k_back >= 0) & (k_back % n_tiles == n_tiles - 1))
    def _():
        for b in every:
            for h in range(hg_heads):
                sfin_hg_ref[b, h] = st_ref[b, h].T
            sfin_rw_ref[b] = s_ref[b]


def _mixers(z_hg, z_rw, log_lb, one_m_lb, hg_norm, s0_hg, shift_prev, s0_rw,
            mu, w0, w2p, a0, a2p, g2, k_k, k_a, r_k, ln_w, ln_b):
    bsz, t_len, hg_proj = z_hg.shape
    rw_proj = z_rw.shape[2]
    hg_width = hg_proj // 4
    hg_heads = hg_width // HG_HEAD_DIM
    rw_heads = s0_rw.shape[1]
    hd = RW_HEAD_DIM
    rw_width = rw_heads * hd
    tile = min(MIX_TILE, t_len)
    assert t_len % tile == 0 and tile % RW_CHUNK == 0 and tile % HG_BLOCK == 0 and rw_heads % 2 == 0
    n_tiles = t_len // tile
    n_pairs = rw_heads // 2
    s0_rw = s0_rw.reshape(bsz, n_pairs, 2, hd, hd).transpose(0, 1, 3, 2, 4).reshape(bsz, n_pairs, hd, 2 * hd)
    group = jnp.arange(HG_BLOCK * HG_HEAD_DIM, dtype=jnp.int32)[:, None] // HG_HEAD_DIM
    selector = (group == jnp.arange(LANES, dtype=jnp.int32)[None, :]).astype(BF16)
    head_of = jnp.arange(LANES, dtype=jnp.int32) // hd
    head_ones = (head_of[:, None] == head_of[None, :]).astype(BF16)

    assert RW_CHUNK == RW_HEAD_DIM
    pipelined = False
    seqs = min(bsz, MIX_SEQS) if n_tiles == 1 else 1
    assert bsz % seqs == 0
    total = bsz * n_tiles // seqs
    front_k = lambda k: jnp.minimum(k, total - 1)
    back_k = (lambda k: jnp.maximum(k - 1, 0)) if pipelined else front_k
    tok_in = lambda w: pl.BlockSpec((seqs, tile, w), lambda k: (front_k(k) // n_tiles, front_k(k) % n_tiles, 0))
    tok_out = lambda w: pl.BlockSpec((seqs, tile, w), lambda k: (back_k(k) // n_tiles, back_k(k) % n_tiles, 0))
    per_seq = lambda a, which: pl.BlockSpec((seqs,) + a.shape[1:],
                                            lambda k: (which(k) // n_tiles,) + (0,) * (a.ndim - 1))
    per_seq_scratch = lambda shape, dtype: pltpu.VMEM((seqs,) + shape, dtype)
    consts_hg = [log_lb, one_m_lb, hg_norm, selector]
    consts_rw = [mu, w0, w2p, a0, a2p, g2, k_k, k_a, r_k, ln_w, ln_b, head_ones]
    kern = functools.partial(_mixer_kernel, hg_heads=hg_heads, rw_heads=rw_heads, tile=tile,
                             n_tiles=n_tiles, pipelined=pipelined, seqs=seqs)
    o_hg, n_hg, o_rw, n_rw = pl.pallas_call(
        kern,
        grid=(total + 1 if pipelined else total,),
        in_specs=[tok_in(hg_proj), tok_in(rw_proj)]
                 + [_const_spec(a.shape) for a in consts_hg] + [per_seq(s0_hg, back_k)]
                 + [per_seq(shift_prev, front_k), per_seq(s0_rw, back_k)]
                 + [_const_spec(a.shape) for a in consts_rw],
        out_specs=[tok_out(hg_width), per_seq(s0_hg, back_k), tok_out(rw_width), per_seq(s0_rw, back_k)],
        out_shape=[jax.ShapeDtypeStruct((bsz, t_len, hg_width), BF16),
                   jax.ShapeDtypeStruct(s0_hg.shape, F32),
                   jax.ShapeDtypeStruct((bsz, t_len, rw_width), BF16),
                   jax.ShapeDtypeStruct(s0_rw.shape, F32)],
        scratch_shapes=[per_seq_scratch((hg_heads, HG_HEAD_DIM, HG_HEAD_DIM), F32),
                        per_seq_scratch((n_pairs, hd, 2 * hd), F32),
                        per_seq_scratch((SUBLANES, rw_proj), F32),
                        per_seq_scratch((SUBLANES + tile, rw_proj), F32),
                        per_seq_scratch((tile, rw_width), F32),
                        per_seq_scratch((tile * hg_heads, LANES), F32)]
                       + [per_seq_scratch((5, tile, hg_width), F32)] * 2
                       + [per_seq_scratch((tile * hg_heads, HG_BLOCK * HG_HEAD_DIM), BF16)] * 2
                       + [per_seq_scratch((10, tile, rw_width), F32)] * 2,
        compiler_params=pltpu.CompilerParams(dimension_semantics=("arbitrary",),
                                             vmem_limit_bytes=VMEM_LIMIT),
        name="mixers",
    )(z_hg, z_rw, *consts_hg, s0_hg, shift_prev, s0_rw, *consts_rw)
    n_rw = n_rw.reshape(bsz, n_pairs, hd, 2, hd).transpose(0, 1, 3, 2, 4).reshape(bsz, rw_heads, hd, hd)
    return o_hg, n_hg, o_rw, n_rw


def _gelu_exact(x):
    return 0.5 * x * (1.0 + lax.erf(x * (2.0 ** -0.5)))


def _post_kernel(h_ref, ohg_ref, orw_ref, p_ref, cvprev_ref,
                 wout_ref, nffn_ref, win_ref, cw_ref, cb_ref, wo_ref,
                 nple_ref, pgate_ref, pproj_ref, fnorm_ref,
                 hout_ref, cvnew_ref, carry_ref, ugs_ref, *, seqs, tm, n_col_blocks, apply_final_norm):
    t_idx = pl.program_id(1)
    cols = FF_COLS
    d_ff = n_col_blocks * cols
    hg_width = ohg_ref.shape[-1]
    rows = seqs * tm
    flat = lambda ref: ref[...].reshape(rows, ref.shape[-1])
    col_block = lambda j: slice(j * cols, (j + 1) * cols)
    up = lambda j: (_dot(xn, win_ref[:, col_block(j)]),
                    _dot(xn, win_ref[:, d_ff + j * cols:d_ff + (j + 1) * cols]))

    @pl.when(t_idx == 0)
    def _():
        carry_ref[...] = jnp.zeros(carry_ref.shape, carry_ref.dtype)
        for j in range(n_col_blocks):
            for b in range(seqs):
                carry_ref[j, b, SUBLANES - (CONV_W - 1):SUBLANES, :] = cvprev_ref[b, :, col_block(j)]

    h1 = (flat(h_ref) + _dot(flat(ohg_ref).astype(BF16), wout_ref[0:hg_width, :])
          + _dot(flat(orw_ref).astype(BF16), wout_ref[hg_width:, :]))
    xn = _rmsnorm(h1, nffn_ref[...]).astype(BF16)
    acc = jnp.zeros_like(h1)
    up_next = up(0)
    for j in range(n_col_blocks):
        ug, uv = up_next
        if j + 1 < n_col_blocks:
            up_next = up(j + 1)
        cw = cw_ref[:, col_block(j)]
        conv = []
        for b in range(seqs):
            ug_b = ug[b * tm:(b + 1) * tm]
            ugs_ref[b, 0:SUBLANES, :] = carry_ref[j, b]
            ugs_ref[b, SUBLANES:SUBLANES + tm, :] = ug_b
            carry_ref[j, b] = ug_b[tm - SUBLANES:tm, :]
            cvnew_ref[b, :, col_block(j)] = ug_b[tm - (CONV_W - 1):tm, :]
            conv.append(cw[0:1, :] * ugs_ref[b, SUBLANES - 2:SUBLANES - 2 + tm, :]
                        + cw[1:2, :] * ugs_ref[b, SUBLANES - 1:SUBLANES - 1 + tm, :])
        c = cb_ref[:, col_block(j)] + jnp.concatenate(conv, axis=0) + cw[2:3, :] * ug
        hid = (_gelu_exact(c) * uv).astype(BF16)
        acc = acc + _dot(hid, wo_ref[col_block(j), :])
    h2 = h1 + acc
    gate = _sigmoid(_dot(_rmsnorm(h2, nple_ref[...]).astype(BF16), pgate_ref[...]))
    h3 = h2 + gate * _dot(flat(p_ref).astype(BF16), pproj_ref[...])
    if apply_final_norm:
        h3 = _rmsnorm(h3, fnorm_ref[...])
    hout_ref[...] = h3.reshape(hout_ref.shape)


def _post(h, o_hg, o_rw, p_all, layer, cv_prev, w_out, norm_ffn, ffn_in, cw, cb, wo, norm_ple,
          ple_gate, ple_proj, final_norm, apply_final_norm):
    bsz, t_len, d = h.shape
    tm = min(POST_TILE, t_len)
    seqs = min(bsz, POST_TILE // tm)
    assert t_len % tm == 0 and tm % SUBLANES == 0 and bsz % seqs == 0
    d_ff = cb.shape[1]
    n_col_blocks = d_ff // FF_COLS
    assert d_ff % FF_COLS == 0
    kern = functools.partial(_post_kernel, seqs=seqs, tm=tm, n_col_blocks=n_col_blocks,
                             apply_final_norm=apply_final_norm)
    tok = lambda w: pl.BlockSpec((seqs, tm, w), lambda b, t: (b, t, 0))
    p_spec = pl.BlockSpec((None, seqs, tm, p_all.shape[-1]), lambda b, t: (layer, b, t, 0))
    cv_spec = pl.BlockSpec((seqs, CONV_W - 1, d_ff), lambda b, t: (b, 0, 0))
    consts = [w_out, norm_ffn, ffn_in, cw, cb, wo, norm_ple, ple_gate, ple_proj, final_norm]
    return pl.pallas_call(
        kern,
        grid=(bsz // seqs, t_len // tm),
        in_specs=[tok(d), tok(o_hg.shape[2]), tok(o_rw.shape[2]), p_spec, cv_spec]
                 + [_const_spec(a.shape) for a in consts],
        out_specs=[tok(d), cv_spec],
        out_shape=[jax.ShapeDtypeStruct(h.shape, F32), jax.ShapeDtypeStruct(cv_prev.shape, F32)],
        scratch_shapes=[pltpu.VMEM((n_col_blocks, seqs, SUBLANES, FF_COLS), F32),
                        pltpu.VMEM((seqs, SUBLANES + tm, FF_COLS), F32)],
        compiler_params=pltpu.CompilerParams(dimension_semantics=("parallel", "arbitrary"),
                                             vmem_limit_bytes=VMEM_LIMIT),
        name="post",
    )(h, o_hg, o_rw, p_all, cv_prev, *consts)


def _layer_params(i, hg_proj, lb_all, w):
    rw_width = w["rw_w0"].shape[1]
    row = lambda a: a.reshape(1, -1)
    lb = jnp.maximum(lb_all[i], 0.0)
    zeros_lora = jnp.zeros((RW_LORA_W, rw_width), F32)
    return dict(
        norm_mix=row(w["norm_mix"][i]),
        w_in=w["w_in"][i].astype(BF16),
        log_lb=row(jnp.log(lb)), one_m_lb=row(1.0 - lb), hg_norm=row(w["hg_norm"][i]),
        mu=row(w["rw_mu"][i]), w0=row(w["rw_w0"][i]),
        w2p=jnp.concatenate([w["rw_w2"][i], zeros_lora], axis=0).astype(BF16),
        a0=row(w["rw_a0"][i]),
        a2p=jnp.concatenate([zeros_lora, w["rw_a2"][i]], axis=0).astype(BF16),
        g2=w["rw_g2"][i].astype(BF16),
        k_k=row(w["rw_k_k"][i]), k_a=row(w["rw_k_a"][i]), r_k=row(w["rw_r_k"][i]),
        ln_w=row(w["rw_ln_w"][i]), ln_b=row(w["rw_ln_b"][i]),
        w_out=w["w_out"][i].astype(BF16),
        norm_ffn=row(w["norm_ffn"][i]),
        ffn_in=w["ffn_in"][i].astype(BF16),
        cw=w["ffn_conv_w"][i], cb=row(w["ffn_conv_b"][i]),
        wo=w["ffn_out"][i].astype(BF16),
        norm_ple=row(w["norm_ple"][i]),
        ple_gate=w["ple_gate"][i].astype(BF16), ple_proj=w["ple_proj"][i].astype(BF16),
        final_norm=row(w["final_norm"]),
    )


def _trunk(x, p, s_hg, s_rw, s_sh, s_cv, layers):
    bsz, t_len, d = x.shape
    depth = len(layers)
    h = x
    new_hg, new_rw, new_sh, new_cv = [], [], [], []
    for i, lp in enumerate(layers):
        z_hg, z_rw = _in_proj(h.reshape(bsz * t_len, d), lp["norm_mix"], lp["w_in"], 4 * lp["hg_norm"].shape[1])
        z_hg = z_hg.reshape(bsz, t_len, -1)
        z_rw = z_rw.reshape(bsz, t_len, -1)
        o_hg, n_hg, o_rw, n_rw = _mixers(
            z_hg, z_rw, lp["log_lb"], lp["one_m_lb"], lp["hg_norm"], s_hg[i], s_sh[i][:, None, :],
            s_rw[i], lp["mu"], lp["w0"], lp["w2p"], lp["a0"], lp["a2p"], lp["g2"], lp["k_k"],
            lp["k_a"], lp["r_k"], lp["ln_w"], lp["ln_b"])
        h, n_cv = _post(h, o_hg, o_rw, p, i, s_cv[i], lp["w_out"], lp["norm_ffn"],
                        lp["ffn_in"], lp["cw"], lp["cb"], lp["wo"], lp["norm_ple"],
                        lp["ple_gate"], lp["ple_proj"], lp["final_norm"], i == depth - 1)
        new_hg.append(n_hg)
        new_rw.append(n_rw)
        new_sh.append(z_rw[:, -1])
        new_cv.append(n_cv)
    return h, jnp.stack(new_hg), jnp.stack(new_rw), jnp.stack(new_sh), jnp.stack(new_cv)


def kernel(x_prompt, x_sample, state_hgrn, state_rwkv, state_shift, state_conv, p_prompt, p_sample,
           norm_mix, w_in, lb_raw, hg_norm, rw_mu, rw_w0, rw_w2, rw_a0, rw_a2, rw_g2, rw_k_k, rw_k_a,
           rw_r_k, rw_ln_w, rw_ln_b, w_out, norm_ffn, ffn_in, ffn_conv_w, ffn_conv_b, ffn_out,
           norm_ple, ple_gate, ple_proj, final_norm):
    w = dict(norm_mix=norm_mix, w_in=w_in, hg_norm=hg_norm, rw_mu=rw_mu, rw_w0=rw_w0, rw_w2=rw_w2,
             rw_a0=rw_a0, rw_a2=rw_a2, rw_g2=rw_g2, rw_k_k=rw_k_k, rw_k_a=rw_k_a,
             rw_r_k=rw_r_k.reshape(rw_r_k.shape[0], -1), rw_ln_w=rw_ln_w, rw_ln_b=rw_ln_b, w_out=w_out,
             norm_ffn=norm_ffn, ffn_in=ffn_in, ffn_conv_w=ffn_conv_w, ffn_conv_b=ffn_conv_b,
             ffn_out=ffn_out, norm_ple=norm_ple, ple_gate=ple_gate, ple_proj=ple_proj,
             final_norm=final_norm)
    depth = w_in.shape[0]
    hg_proj = 4 * hg_norm.shape[1]
    lb_all = jnp.cumsum(jax.nn.softmax(lb_raw.astype(F32), axis=0), axis=0)
    lb_all = lb_all - lb_all[:1]
    layers = [_layer_params(i, hg_proj, lb_all, w) for i in range(depth)]

    nb = x_prompt.shape[0]
    zeros_like_state = lambda s: jnp.zeros((depth, nb) + s.shape[2:], s.dtype)
    y_p, hg_p, rw_p, sh_p, cv_p = _trunk(x_prompt, p_prompt, zeros_like_state(state_hgrn),
                                         zeros_like_state(state_rwkv), zeros_like_state(state_shift),
                                         zeros_like_state(state_conv), layers)
    y_s, hg_s, rw_s, sh_s, cv_s = _trunk(x_sample, p_sample, state_hgrn, state_rwkv, state_shift,
                                         state_conv, layers)
    return (y_p, y_s, hg_p, rw_p, sh_p, cv_p, hg_s, rw_s, sh_s, cv_s)
```
